```python
import math
import jax
import jax.numpy as jnp
from jax import lax
import numpy as np

D_MODEL = 1024
BATCH = 4
SEQ = 8192
DEPTH = 1

MIX_WIDTH = D_MODEL
RWKV_WIDTH = MIX_WIDTH // 2
RWKV_HEAD_DIM = 64
RWKV_HEADS = RWKV_WIDTH // RWKV_HEAD_DIM
DECAY_LORA = 64
ICLR_LORA = 64
GATE_LORA = 128
RWKV_PROJ = 3 * RWKV_WIDTH + DECAY_LORA + ICLR_LORA + GATE_LORA
S5_WIDTH = MIX_WIDTH - RWKV_WIDTH
S5_GROUP = 16
S5_GROUPS = S5_WIDTH // S5_GROUP
S5_STATE = 64
IN_PROJ = RWKV_PROJ + S5_WIDTH
N_EXPERTS = 256
TOP_K = 8
ROUTE_GROUPS = 8
ROUTE_TOPK_GROUPS = 4
EXPERT_FF = 256
SHARED_FF = 256
ROUTE_SCALE = 2.5
DISPATCH_BLOCK = 128
NORM_EPS = 1e-6
LNX_EPS = 64e-5
L2_EPS = 1e-12
S5_MAX_REAL = -1e-4
N_MODS = 6

kernel_name = "hybrid_rwkv7_s5_moe_block"


def _f32(t):
    return t.astype(jnp.float32)


def rms_norm(x, gain):
    xf = _f32(x)
    y = xf * lax.rsqrt(jnp.mean(xf * xf, axis=-1, keepdims=True) + NORM_EPS)
    return (y * _f32(gain)).astype(x.dtype)


def token_shift(z):
    return jnp.pad(z, ((0, 0), (1, 0), (0, 0)))[:, :-1]


def rwkv7_recurrence(r, w, k, v, a, b):
    bsz, _, nh, n = r.shape

    def step(state, inp):
        r_t, w_t, k_t, v_t, a_t, b_t = inp
        sa = jnp.einsum("bhvk,bhk->bhv", state, a_t)
        state = (state * w_t[:, :, None, :] + sa[..., None] * b_t[:, :, None, :]
                 + v_t[..., None] * k_t[:, :, None, :])
        return state, jnp.einsum("bhvk,bhk->bhv", state, r_t)

    xs = tuple(jnp.swapaxes(z, 0, 1) for z in (r, w, k, v, a, b))
    s0 = jnp.zeros((bsz, nh, n, n), jnp.float32)
    _, y = lax.scan(step, s0, xs)
    return jnp.swapaxes(y, 0, 1)


def rwkv7_mixer(p, mu, w0, w2, a0, a2, g2, k_k, k_a, r_k, ln_w, ln_b):
    bsz, seq, _ = p.shape
    p = _f32(p)
    p = p + (token_shift(p) - p) * _f32(mu)
    cuts = [RWKV_WIDTH, 2 * RWKV_WIDTH, 3 * RWKV_WIDTH, 3 * RWKV_WIDTH + DECAY_LORA,
            3 * RWKV_WIDTH + DECAY_LORA + ICLR_LORA]
    r, k, v, w_lo, a_lo, g_lo = jnp.split(p, cuts, axis=-1)
    log_w = -jax.nn.softplus(-(_f32(w0) + jnp.tanh(w_lo) @ _f32(w2))) - 0.5
    decay = jnp.exp(-jnp.exp(log_w))
    iclr = jax.nn.sigmoid(_f32(a0) + a_lo @ _f32(a2))
    gate = jax.nn.sigmoid(g_lo) @ _f32(g2)

    def heads(t):
        return t.reshape(bsz, seq, RWKV_HEADS, RWKV_HEAD_DIM)

    kk = heads(k * _f32(k_k))
    kk = kk / jnp.maximum(jnp.sqrt(jnp.sum(kk * kk, axis=-1, keepdims=True)), L2_EPS)
    k = k * (1.0 + (iclr - 1.0) * _f32(k_a))
    r, k, v, decay, iclr = heads(r), heads(k), heads(v), heads(decay), heads(iclr)
    y = rwkv7_recurrence(r, decay, k, v, -kk, kk * iclr)
    mean = jnp.mean(y, axis=-1, keepdims=True)
    var = jnp.mean(jnp.square(y - mean), axis=-1, keepdims=True)
    y = ((y - mean) * lax.rsqrt(var + LNX_EPS)).reshape(bsz, seq, RWKV_WIDTH)
    y = y * _f32(ln_w) + _f32(ln_b)
    bonus = jnp.sum(r * k * _f32(r_k), axis=-1, keepdims=True) * v
    return (y + bonus.reshape(bsz, seq, RWKV_WIDTH)) * gate


def s5_mixer(u, log_dt, a_re, a_im, b_re, b_im, c_re, c_im, d, w_glu, b_glu):
    bsz, seq, _ = u.shape
    u = _f32(u)
    ug = u.reshape(bsz, seq, S5_GROUPS, S5_GROUP)
    dt = jnp.exp(_f32(log_dt))[:, None]
    lam_re = jnp.minimum(_f32(a_re), S5_MAX_REAL)
    lam_im = _f32(a_im)
    mag = jnp.exp(lam_re * dt)
    ang = lam_im * dt
    ab_re, ab_im = mag * jnp.cos(ang), mag * jnp.sin(ang)
    den = lam_re * lam_re + lam_im * lam_im
    n_re, n_im = ab_re - 1.0, ab_im
    q_re = (n_re * lam_re + n_im * lam_im) / den
    q_im = (n_im * lam_re - n_re * lam_im) / den
    br, bi = _f32(b_re), _f32(b_im)
    bb_re = q_re[..., None] * br - q_im[..., None] * bi
    bb_im = q_re[..., None] * bi + q_im[..., None] * br
    bu_re = jnp.einsum("bsgc,gpc->sbgp", ug, bb_re)
    bu_im = jnp.einsum("bsgc,gpc->sbgp", ug, bb_im)
    at_re = jnp.broadcast_to(ab_re, (seq, 1, S5_GROUPS, S5_STATE))
    at_im = jnp.broadcast_to(ab_im, (seq, 1, S5_GROUPS, S5_STATE))

    def combine(e1, e2):
        a1r, a1i, b1r, b1i = e1
        a2r, a2i, b2r, b2i = e2
        return (a2r * a1r - a2i * a1i, a2r * a1i + a2i * a1r,
                a2r * b1r - a2i * b1i + b2r, a2r * b1i + a2i * b1r + b2i)

    _, _, xr, xi = lax.associative_scan(combine, (at_re, at_im, bu_re, bu_im), axis=0)
    y = (jnp.einsum("sbgp,gcp->bsgc", xr, _f32(c_re))
         - jnp.einsum("sbgp,gcp->bsgc", xi, _f32(c_im)))
    y = y.reshape(bsz, seq, S5_WIDTH) + _f32(d) * u
    y = jax.nn.gelu(y)
    return y * jax.nn.sigmoid(y @ _f32(w_glu) + _f32(b_glu))


def route_tokens(h, w_router, router_bias):
    t = h.shape[0]
    scores = jax.nn.sigmoid(_f32(h) @ _f32(w_router))
    choice = scores + _f32(router_bias)
    grp = choice.reshape(t, ROUTE_GROUPS, N_EXPERTS // ROUTE_GROUPS)
    grp_score = jnp.sum(lax.top_k(grp, 2)[0], axis=-1)
    _, grp_idx = lax.top_k(grp_score, ROUTE_TOPK_GROUPS)
    grp_mask = jnp.any(grp_idx[..., None] == jnp.arange(ROUTE_GROUPS), axis=-2)
    exp_mask = jnp.repeat(grp_mask, N_EXPERTS // ROUTE_GROUPS, axis=-1)
    _, top_idx = lax.top_k(jnp.where(exp_mask, choice, -jnp.inf), TOP_K)
    top_w = jnp.take_along_axis(scores, top_idx, axis=-1)
    top_w = top_w / jnp.sum(top_w, axis=-1, keepdims=True) * ROUTE_SCALE
    return top_idx.astype(jnp.int32), top_w


def routed_experts(h, top_idx, top_w, w_gate, w_up, w_down):
    t, dm = h.shape
    n_assign = t * TOP_K
    flat_e = top_idx.reshape(n_assign)
    flat_w = top_w.reshape(n_assign)
    flat_tok = jnp.repeat(jnp.arange(t, dtype=jnp.int32), TOP_K)
    order = jnp.argsort(flat_e)
    se, stok, sw = flat_e[order], flat_tok[order], flat_w[order]
    counts = jnp.bincount(flat_e, length=N_EXPERTS).astype(jnp.int32)
    padded = (counts + DISPATCH_BLOCK - 1) // DISPATCH_BLOCK * DISPATCH_BLOCK
    pad_end = jnp.cumsum(padded)
    pad_start = pad_end - padded
    grp_start = jnp.cumsum(counts) - counts
    dest = pad_start[se] + jnp.arange(n_assign, dtype=jnp.int32) - grp_start[se]
    n_blocks = -(-n_assign // DISPATCH_BLOCK) + N_EXPERTS
    n_rows = n_blocks * DISPATCH_BLOCK
    row_tok = jnp.full((n_rows,), t, jnp.int32).at[dest].set(stok)
    row_w = jnp.zeros((n_rows,), jnp.float32).at[dest].set(sw)
    block_start = jnp.arange(n_blocks, dtype=jnp.int32) * DISPATCH_BLOCK
    block_e = jnp.minimum(jnp.searchsorted(pad_end, block_start, side="right"),
                          N_EXPERTS - 1).astype(jnp.int32)
    h_pad = jnp.concatenate([h, jnp.zeros((1, dm), h.dtype)], axis=0)

    def body(acc, blk):
        tok, wt, e = blk
        xb = h_pad[tok]
        hid = jax.nn.silu(xb @ w_gate[e]) * (xb @ w_up[e])
        yb = hid @ w_down[e]
        return acc.at[tok].add(_f32(yb) * wt[:, None]), None

    acc0 = jnp.zeros((t + 1, dm), jnp.float32)
    acc, _ = lax.scan(body, acc0, (row_tok.reshape(n_blocks, DISPATCH_BLOCK),
                                   row_w.reshape(n_blocks, DISPATCH_BLOCK), block_e))
    return acc[:t]


def moe_ffn(h, w_router, router_bias, w_gate, w_up, w_down, ws_gate, ws_up, ws_down):
    top_idx, top_w = route_tokens(h, w_router, router_bias)
    shared = (jax.nn.silu(h @ ws_gate) * (h @ ws_up)) @ ws_down
    routed = routed_experts(h, top_idx, top_w, w_gate, w_up, w_down)
    return (_f32(shared) + routed).astype(h.dtype)


def setup_inputs(seed: int = 0) -> dict:
    key = jax.random.key(seed)
    keys = jax.random.split(key, 48)
    counter = [0]

    def nk():
        k = keys[counter[0]]
        counter[0] += 1
        return k

    def nrm(shape, scale):
        return jax.random.normal(nk(), shape, jnp.float32) * scale

    def unif(shape, lo, hi):
        return jax.random.uniform(nk(), shape, jnp.float32, lo, hi)

    L, D = DEPTH, D_MODEL
    w0_base = jnp.tile(jnp.linspace(-6.0, -1.0, RWKV_HEAD_DIM, dtype=jnp.float32), RWKV_HEADS)
    a_im_base = jnp.pi * jnp.arange(S5_STATE, dtype=jnp.float32)
    return {
        "x": nrm((BATCH, SEQ, D), 1.0),
        "c": nrm((BATCH, D), 1.0),
        "w_ada": nrm((L, D, N_MODS * D), 0.5 * D ** -0.5),
        "b_ada": nrm((L, N_MODS * D), 0.01),
        "g_pre_mix": 1.0 + nrm((L, D), 0.05),
        "g_post_mix": 1.0 + nrm((L, D), 0.05),
        "g_pre_ffn": 1.0 + nrm((L, D), 0.05),
        "g_post_ffn": 1.0 + nrm((L, D), 0.05),
        "w_in": nrm((L, D, IN_PROJ), D ** -0.5),
        "mu_shift": unif((L, RWKV_PROJ), 0.0, 1.0),
        "rwkv_w0": w0_base + nrm((L, RWKV_WIDTH), 0.1),
        "rwkv_w2": nrm((L, DECAY_LORA, RWKV_WIDTH), 0.5 * DECAY_LORA ** -0.5),
        "rwkv_a0": nrm((L, RWKV_WIDTH), 0.1),
        "rwkv_a2": nrm((L, ICLR_LORA, RWKV_WIDTH), ICLR_LORA ** -0.5),
        "rwkv_g2": nrm((L, GATE_LORA, RWKV_WIDTH), GATE_LORA ** -0.5),
        "rwkv_k_k": 0.85 + nrm((L, RWKV_WIDTH), 0.05),
        "rwkv_k_a": 1.0 + nrm((L, RWKV_WIDTH), 0.05),
        "rwkv_r_k": nrm((L, RWKV_HEADS, RWKV_HEAD_DIM), 0.1),
        "rwkv_ln_w": 1.0 + nrm((L, RWKV_WIDTH), 0.05),
        "rwkv_ln_b": nrm((L, RWKV_WIDTH), 0.01),
        "s5_log_dt": unif((L, S5_GROUPS), math.log(1e-3), math.log(1e-1)),
        "s5_a_re": -0.5 + nrm((L, S5_GROUPS, S5_STATE), 0.01),
        "s5_a_im": a_im_base + nrm((L, S5_GROUPS, S5_STATE), 0.01),
        "s5_b_re": nrm((L, S5_GROUPS, S5_STATE, S5_GROUP), (2 * S5_GROUP) ** -0.5),
        "s5_b_im": nrm((L, S5_GROUPS, S5_STATE, S5_GROUP), (2 * S5_GROUP) ** -0.5),
        "s5_c_re": nrm((L, S5_GROUPS, S5_GROUP, S5_STATE), 0.7),
        "s5_c_im": nrm((L, S5_GROUPS, S5_GROUP, S5_STATE), 0.7),
        "s5_d": nrm((L, S5_WIDTH), 1.0),
        "s5_w_glu": nrm((L, S5_WIDTH, S5_WIDTH), S5_WIDTH ** -0.5),
        "s5_b_glu": nrm((L, S5_WIDTH), 0.01),
        "w_out": nrm((L, MIX_WIDTH, D), MIX_WIDTH ** -0.5),
        "w_router": nrm((L, D, N_EXPERTS), D ** -0.5),
        "router_bias": nrm((L, N_EXPERTS), 0.01),
        "w_gate": nrm((L, N_EXPERTS, D, EXPERT_FF), D ** -0.5),
        "w_up": nrm((L, N_EXPERTS, D, EXPERT_FF), D ** -0.5),
        "w_down": nrm((L, N_EXPERTS, EXPERT_FF, D), EXPERT_FF ** -0.5),
        "ws_gate": nrm((L, D, SHARED_FF), D ** -0.5),
        "ws_up": nrm((L, D, SHARED_FF), D ** -0.5),
        "ws_down": nrm((L, SHARED_FF, D), SHARED_FF ** -0.5),
    }


def reference(x, c, w_ada, b_ada, g_pre_mix, g_post_mix, g_pre_ffn, g_post_ffn, w_in,
              mu_shift, rwkv_w0, rwkv_w2, rwkv_a0, rwkv_a2, rwkv_g2, rwkv_k_k, rwkv_k_a,
              rwkv_r_k, rwkv_ln_w, rwkv_ln_b, s5_log_dt, s5_a_re, s5_a_im, s5_b_re, s5_b_im,
              s5_c_re, s5_c_im, s5_d, s5_w_glu, s5_b_glu, w_out, w_router, router_bias,
              w_gate, w_up, w_down, ws_gate, ws_up, ws_down):
    bsz, seq, dm = x.shape
    for l in range(DEPTH):
        mods = jax.nn.silu(c) @ w_ada[l] + b_ada[l]
        sh_m, sc_m, gt_m, sh_f, sc_f, gt_f = [m[:, None, :] for m in jnp.split(mods, N_MODS, axis=-1)]

        h = rms_norm(x, g_pre_mix[l]) * (1.0 + sc_m) + sh_m
        proj = h @ w_in[l]
        y_rwkv = rwkv7_mixer(proj[..., :RWKV_PROJ], mu_shift[l], rwkv_w0[l], rwkv_w2[l],
                             rwkv_a0[l], rwkv_a2[l], rwkv_g2[l], rwkv_k_k[l], rwkv_k_a[l],
                             rwkv_r_k[l], rwkv_ln_w[l], rwkv_ln_b[l])
        y_s5 = s5_mixer(proj[..., RWKV_PROJ:], s5_log_dt[l], s5_a_re[l], s5_a_im[l],
                        s5_b_re[l], s5_b_im[l], s5_c_re[l], s5_c_im[l], s5_d[l],
                        s5_w_glu[l], s5_b_glu[l])
        mixed = jnp.concatenate([y_rwkv, y_s5], axis=-1).astype(x.dtype) @ w_out[l]
        x = x + gt_m * rms_norm(mixed, g_post_mix[l])

        h = rms_norm(x, g_pre_ffn[l]) * (1.0 + sc_f) + sh_f
        y = moe_ffn(h.reshape(bsz * seq, dm), w_router[l], router_bias[l], w_gate[l],
                    w_up[l], w_down[l], ws_gate[l], ws_up[l], ws_down[l]).reshape(bsz, seq, dm)
        x = x + gt_f * rms_norm(y, g_post_ffn[l])
    return x
```

```python
import functools
import math

import jax
import jax.numpy as jnp
from jax import lax
from jax.experimental import pallas as pl
from jax.experimental.pallas import tpu as pltpu

NORM_EPS = 1e-6
LNX_EPS = 64e-5
L2_EPS = 1e-12
S5_MAX_REAL = -1e-4
ROUTE_SCALE = 2.5
N_MODS = 6

RWKV_HEAD_DIM = 64
DECAY_LORA = 64
ICLR_LORA = 64
GATE_LORA = 128
S5_GROUP = 16
S5_STATE = 64
TOP_K = 8
ROUTE_GROUPS = 8
ROUTE_TOPK_GROUPS = 4

RWKV_CHUNK = 64
S5_CHUNK = 32
EXPERT_ROW_TILE = 256

VMEM_LIMIT = 56 * 1024 * 1024

HI = lax.Precision.HIGHEST


def _cparams(*sem):
    return pltpu.CompilerParams(dimension_semantics=sem, vmem_limit_bytes=VMEM_LIMIT)


def _dot(a, b, precision=None):
    return jnp.dot(a, b, preferred_element_type=jnp.float32, precision=precision)


def _dot_nt(a, b, precision=None):
    return lax.dot_general(a, b, (((1,), (1,)), ((), ())),
                           preferred_element_type=jnp.float32, precision=precision)


def _dot_tn(a, b, precision=None):
    return lax.dot_general(a, b, (((0,), (0,)), ((), ())),
                           preferred_element_type=jnp.float32, precision=precision)


def _bf(x):
    return x.astype(jnp.bfloat16)


def _sigmoid(x):
    return 1.0 / (1.0 + jnp.exp(-x))


def _silu(x):
    return x * _sigmoid(x)


def _rms(x, gain):
    return x * lax.rsqrt(jnp.mean(x * x, axis=-1, keepdims=True) + NORM_EPS) * gain


def _ada_kernel(c_ref, w_ref, b_ref, o_ref):
    c = c_ref[...]
    o_ref[...] = _dot(_silu(c), w_ref[...], HI) + b_ref[...]


def ada_mods(c, w_ada, b_ada):
    bsz, dm = c.shape
    n = w_ada.shape[1]
    tn = dm
    return pl.pallas_call(
        _ada_kernel,
        grid=(n // tn,),
        in_specs=[pl.BlockSpec((bsz, dm), lambda j: (0, 0)),
                  pl.BlockSpec((dm, tn), lambda j: (0, j)),
                  pl.BlockSpec((1, tn), lambda j: (0, j))],
        out_specs=pl.BlockSpec((bsz, tn), lambda j: (0, j)),
        out_shape=jax.ShapeDtypeStruct((bsz, n), jnp.float32),
        compiler_params=_cparams("arbitrary"),
        name="ada",
    )(c, w_ada, b_ada.reshape(1, n))


def _inproj_kernel(x_ref, mods_ref, g_ref, w_ref, p_ref, u_ref):
    x = x_ref[0]
    m = mods_ref[0]
    h = _rms(x, g_ref[...]) * (1.0 + m[1:2, :]) + m[0:1, :]
    proj = _dot(_bf(h), w_ref[...])
    n_p = p_ref.shape[-1]
    p_ref[0] = proj[:, :n_p]
    u_ref[0] = proj[:, n_p:]


def in_proj(x, mods3, g_pre, w_in_bf, n_rwkv, tm):
    bsz, seq, dm = x.shape
    n = w_in_bf.shape[1]
    n_s5 = n - n_rwkv
    return pl.pallas_call(
        _inproj_kernel,
        grid=(bsz, seq // tm),
        in_specs=[pl.BlockSpec((1, tm, dm), lambda b, i: (b, i, 0)),
                  pl.BlockSpec((1, N_MODS, dm), lambda b, i: (b, 0, 0)),
                  pl.BlockSpec((1, dm), lambda b, i: (0, 0)),
                  pl.BlockSpec((dm, n), lambda b, i: (0, 0))],
        out_specs=[pl.BlockSpec((1, tm, n_rwkv), lambda b, i: (b, i, 0)),
                   pl.BlockSpec((1, tm, n_s5), lambda b, i: (b, i, 0))],
        out_shape=[jax.ShapeDtypeStruct((bsz, seq, n_rwkv), jnp.float32),
                   jax.ShapeDtypeStruct((bsz, seq, n_s5), jnp.float32)],
        compiler_params=_cparams("arbitrary", "arbitrary"),
        name="inproj",
    )(x, mods3, g_pre.reshape(1, dm), w_in_bf)


RWKV_PREC = HI


def _rwkv_kernel(p_ref, mu_ref, w0_ref, w2_ref, a0_ref, a2_ref, g2_ref, kk_ref, ka_ref,
                 rk_ref, lnw_ref, lnb_ref, bd_ref, o_ref, s_ref, carry_ref, *, width):
    prec = RWKV_PREC
    hd = RWKV_HEAD_DIM
    heads = width // hd
    i = pl.program_id(1)

    @pl.when(i == 0)
    def _():
        s_ref[...] = jnp.zeros_like(s_ref)
        carry_ref[...] = jnp.zeros_like(carry_ref)

    p = p_ref[0]
    n_t = p.shape[0]
    row = lax.broadcasted_iota(jnp.int32, p.shape, 0)
    prev = jnp.where(row == 0, carry_ref[0:1, :], pltpu.roll(p, 1, axis=0))
    carry_ref[0:1, :] = p[n_t - 1:n_t, :]
    pm = p + (prev - p) * mu_ref[...]

    r = pm[:, 0:width]
    k = pm[:, width:2 * width]
    v = pm[:, 2 * width:3 * width]
    c0 = 3 * width
    w_lo = pm[:, c0:c0 + DECAY_LORA]
    a_lo = pm[:, c0 + DECAY_LORA:c0 + DECAY_LORA + ICLR_LORA]
    g_lo = pm[:, c0 + DECAY_LORA + ICLR_LORA:]

    z = w0_ref[...] + _dot(jnp.tanh(w_lo), w2_ref[...], prec)
    softplus_neg = jnp.maximum(-z, 0.0) + jnp.log(1.0 + jnp.exp(-jnp.abs(z)))
    logd = -jnp.exp(-softplus_neg - 0.5)
    iclr = _sigmoid(a0_ref[...] + _dot(a_lo, a2_ref[...], prec))
    gate = _dot(_sigmoid(g_lo), g2_ref[...], prec)

    bd = bd_ref[...]
    kk = k * kk_ref[...]
    kk = kk / jnp.maximum(jnp.sqrt(_dot(kk * kk, bd, prec)), L2_EPS)
    k2 = k * (1.0 + (iclr - 1.0) * ka_ref[...])
    bonus = _dot(r * k2 * rk_ref[...], bd, prec) * v

    ti = lax.broadcasted_iota(jnp.int32, (n_t, n_t), 0)
    si = lax.broadcasted_iota(jnp.int32, (n_t, n_t), 1)
    low_incl = ti >= si
    low_strict = ti > si
    eye_t = (ti == si).astype(jnp.float32)
    cum = _dot(low_incl.astype(jnp.float32), logd, HI)
    e_pos = jnp.exp(cum)
    e_neg = jnp.exp(-cum)
    r_t = r * e_pos
    a_t = -kk * jnp.exp(cum - logd)
    b_t = kk * iclr * e_neg
    k_t = k2 * e_neg
    w_last = e_pos[n_t - 1:n_t, :]

    hi_ = lax.broadcasted_iota(jnp.int32, (hd, hd), 0)
    hj_ = lax.broadcasted_iota(jnp.int32, (hd, hd), 1)
    eye_h = (hi_ == hj_).astype(jnp.float32)

    ys = []
    for h in range(heads):
        sl = slice(h * hd, (h + 1) * hd)
        ah, bh, kh, rh, vh = a_t[:, sl], b_t[:, sl], k_t[:, sl], r_t[:, sl], v[:, sl]
        m_ab = jnp.where(low_strict, _dot_nt(ah, bh, prec), 0.0)
        m_ak = jnp.where(low_strict, _dot_nt(ah, kh, prec), 0.0)
        n_rb = jnp.where(low_incl, _dot_nt(rh, bh, prec), 0.0)
        n_rk = jnp.where(low_incl, _dot_nt(rh, kh, prec), 0.0)
        t_inv = eye_t + m_ab
        m_pow = m_ab
        steps = max(1, int(math.ceil(math.log2(n_t))) - 1)
        for _ in range(steps):
            m_pow = _dot(m_pow, m_pow, prec)
            t_inv = t_inv + _dot(t_inv, m_pow, prec)
        abar = _dot(t_inv, ah, prec)
        u0 = _dot(t_inv, _dot(m_ak, vh, prec), prec)
        rbar = rh + _dot(n_rb, abar, prec)
        y0 = _dot(n_rb, u0, prec) + _dot(n_rk, vh, prec)
        wl = w_last[:, sl]
        pmat = (eye_h + _dot_tn(abar, bh, prec)) * wl
        dmat = (_dot_tn(u0, bh, prec) + _dot_tn(vh, kh, prec)) * wl
        s0 = s_ref[h]
        ys.append(_dot_nt(rbar, s0, prec) + y0)
        s_ref[h] = _dot(s0, pmat, prec) + dmat
    y = jnp.concatenate(ys, axis=-1)

    inv_hd = 1.0 / hd
    mean = _dot(y, bd, prec) * inv_hd
    yc = y - mean
    var = _dot(yc * yc, bd, prec) * inv_hd
    yn = yc * lax.rsqrt(var + LNX_EPS) * lnw_ref[...] + lnb_ref[...]
    o_ref[0] = (yn + bonus) * gate


def rwkv_mixer(p, mu, w0, w2, a0, a2, g2, k_k, k_a, r_k, ln_w, ln_b):
    bsz, seq, n_p = p.shape
    width = w0.shape[-1]
    heads = width // RWKV_HEAD_DIM
    lc = min(RWKV_CHUNK, seq)
    hid = jnp.arange(width, dtype=jnp.int32) // RWKV_HEAD_DIM
    bd = (hid[:, None] == hid[None, :]).astype(jnp.float32)
    row = lambda t: t.reshape(1, -1)
    full = lambda a: pl.BlockSpec(a.shape, lambda b, i: (0,) * a.ndim)
    consts = [row(mu), row(w0), w2, row(a0), a2, g2, row(k_k), row(k_a), row(r_k),
              row(ln_w), row(ln_b), bd]
    return pl.pallas_call(
        functools.partial(_rwkv_kernel, width=width),
        grid=(bsz, seq // lc),
        in_specs=[pl.BlockSpec((1, lc, n_p), lambda b, i: (b, i, 0))] + [full(a) for a in consts],
        out_specs=pl.BlockSpec((1, lc, width), lambda b, i: (b, i, 0)),
        out_shape=jax.ShapeDtypeStruct((bsz, seq, width), jnp.float32),
        scratch_shapes=[pltpu.VMEM((heads, RWKV_HEAD_DIM, RWKV_HEAD_DIM), jnp.float32),
                        pltpu.VMEM((8, n_p), jnp.float32)],
        compiler_params=_cparams("arbitrary", "arbitrary"),
        name="rwkv",
    )(p, *consts)


S5_PREC = HI


def s5_operators(log_dt, a_re, a_im, b_re, b_im, c_re, c_im, n_chunks, lc):
    n_g, n_p = a_re.shape
    n_c = b_re.shape[-1]
    dt = jnp.exp(log_dt)[:, None]
    lam_re = jnp.minimum(a_re, S5_MAX_REAL)
    lam_im = a_im
    mag = jnp.exp(lam_re * dt)
    ang = lam_im * dt
    ab_re, ab_im = mag * jnp.cos(ang), mag * jnp.sin(ang)
    den = lam_re * lam_re + lam_im * lam_im
    n_re, n_im = ab_re - 1.0, ab_im
    q_re = (n_re * lam_re + n_im * lam_im) / den
    q_im = (n_im * lam_re - n_re * lam_im) / den
    bb_re = q_re[..., None] * b_re - q_im[..., None] * b_im
    bb_im = q_re[..., None] * b_im + q_im[..., None] * b_re
    pw_re, pw_im = [jnp.ones_like(ab_re)], [jnp.zeros_like(ab_im)]
    for _ in range(lc):
        pr, pi = pw_re[-1], pw_im[-1]
        pw_re.append(pr * ab_re - pi * ab_im)
        pw_im.append(pr * ab_im + pi * ab_re)
    pw_re, pw_im = jnp.stack(pw_re), jnp.stack(pw_im)
    ab_b_re = pw_re[:lc, :, :, None] * bb_re[None] - pw_im[:lc, :, :, None] * bb_im[None]
    ab_b_im = pw_re[:lc, :, :, None] * bb_im[None] + pw_im[:lc, :, :, None] * bb_re[None]
    kern = (jnp.einsum("gcp,lgpd->lgcd", c_re, ab_b_re, precision=HI)
            - jnp.einsum("gcp,lgpd->lgcd", c_im, ab_b_im, precision=HI))
    kern = jnp.concatenate([kern, jnp.zeros_like(kern[:1])], axis=0)
    s_i = jnp.arange(lc)[:, None]
    t_i = jnp.arange(lc)[None, :]
    lag = jnp.where(t_i >= s_i, t_i - s_i, lc)
    toep = kern[lag]
    toep = jnp.transpose(toep, (2, 0, 4, 1, 3)).reshape(n_g, lc * n_c, lc * n_c)
    rev_re, rev_im = ab_b_re[::-1], ab_b_im[::-1]
    pm = jnp.concatenate([jnp.transpose(rev_re, (1, 0, 3, 2)).reshape(n_g, lc * n_c, n_p),
                          jnp.transpose(rev_im, (1, 0, 3, 2)).reshape(n_g, lc * n_c, n_p)], axis=-1)
    ca_re = c_re[None] * pw_re[1:, :, None, :] - c_im[None] * pw_im[1:, :, None, :]
    ca_im = c_re[None] * pw_im[1:, :, None, :] + c_im[None] * pw_re[1:, :, None, :]
    q = jnp.concatenate([jnp.transpose(ca_re, (1, 3, 0, 2)).reshape(n_g, n_p, lc * n_c),
                         jnp.transpose(-ca_im, (1, 3, 0, 2)).reshape(n_g, n_p, lc * n_c)], axis=1)
    n_lvl = max(1, int(math.ceil(math.log2(n_chunks))))
    cr, ci = pw_re[lc], pw_im[lc]
    lv = []
    for _ in range(n_lvl):
        lv.append(jnp.stack([jnp.concatenate([cr, cr], -1), jnp.concatenate([-ci, ci], -1)], axis=1))
        cr, ci = cr * cr - ci * ci, 2.0 * cr * ci
    lvl = jnp.stack(lv, axis=1)
    return toep, pm, q, lvl


def _s5_kernel(u_ref, toep_ref, pm_ref, q_ref, lvl_ref, y_ref, *, n_chunks):
    prec = S5_PREC
    u = u_ref[0]
    e = _dot(u, pm_ref[0], prec)
    rows, two_p = e.shape
    half = two_p // 2
    cidx = lax.broadcasted_iota(jnp.int32, e.shape, 0) % n_chunks
    x = e
    n_lvl = lvl_ref.shape[1]
    for j in range(n_lvl):
        sh = 1 << j
        if sh >= n_chunks:
            break
        xs = jnp.where(cidx >= sh, pltpu.roll(x, sh, axis=0), 0.0)
        cf = lvl_ref[0, j]
        x = x + xs * cf[0:1, :] + pltpu.roll(xs, half, axis=1) * cf[1:2, :]
    x_in = jnp.where(cidx >= 1, pltpu.roll(x, 1, axis=0), 0.0)
    y_ref[0] = _dot(u, toep_ref[0], prec) + _dot(x_in, q_ref[0], prec)


def s5_core(u, log_dt, a_re, a_im, b_re, b_im, c_re, c_im):
    bsz, seq, width = u.shape
    n_g, n_p = a_re.shape
    n_c = width // n_g
    lc = min(S5_CHUNK, seq)
    n_chunks = seq // lc
    toep, pm, q, lvl = s5_operators(log_dt, a_re, a_im, b_re, b_im, c_re, c_im, n_chunks, lc)
    ug = u.reshape(bsz, n_chunks, lc, n_g, n_c)
    ug = jnp.transpose(ug, (3, 0, 1, 2, 4)).reshape(n_g, bsz * n_chunks, lc * n_c)
    rows, lw = bsz * n_chunks, lc * n_c
    yg = pl.pallas_call(
        functools.partial(_s5_kernel, n_chunks=n_chunks),
        grid=(n_g,),
        in_specs=[pl.BlockSpec((1, rows, lw), lambda g: (g, 0, 0)),
                  pl.BlockSpec((1, lw, lw), lambda g: (g, 0, 0)),
                  pl.BlockSpec((1, lw, 2 * n_p), lambda g: (g, 0, 0)),
                  pl.BlockSpec((1, 2 * n_p, lw), lambda g: (g, 0, 0)),
                  pl.BlockSpec((1,) + lvl.shape[1:], lambda g: (g, 0, 0, 0))],
        out_specs=pl.BlockSpec((1, rows, lw), lambda g: (g, 0, 0)),
        out_shape=jax.ShapeDtypeStruct((n_g, rows, lw), jnp.float32),
        compiler_params=_cparams("arbitrary"),
        name="s5",
    )(ug, toep, pm, q, lvl)
    yg = yg.reshape(n_g, bsz, n_chunks, lc, n_c)
    return jnp.transpose(yg, (1, 2, 3, 0, 4)).reshape(bsz, seq, width)


ROUTER_PREC = HI


def _gelu_tanh(y):
    return 0.5 * y * (1.0 + jnp.tanh(math.sqrt(2.0 / math.pi) * (y + 0.044715 * (y * y * y))))


def _outproj_kernel(yr_ref, ys_ref, u_ref, x_ref, mods_ref, d_ref, wglu_ref, bglu_ref, wout_ref,
                    gpost_ref, gpre_ref, wrt_ref, x1_ref, h2_ref, lg_ref):
    m = mods_ref[0]
    yr = yr_ref[...]
    y5 = _gelu_tanh(ys_ref[...] + d_ref[...] * u_ref[...])
    y5 = y5 * _sigmoid(_dot(_bf(y5), wglu_ref[...]) + bglu_ref[...])
    wr = yr.shape[-1]
    mixed = _dot(_bf(yr), wout_ref[0:wr, :]) + _dot(_bf(y5), wout_ref[wr:, :])
    x1 = x_ref[...] + m[2:3, :] * _rms(mixed, gpost_ref[...])
    x1_ref[...] = x1
    h2 = _rms(x1, gpre_ref[...]) * (1.0 + m[4:5, :]) + m[3:4, :]
    h2_ref[...] = h2
    lg_ref[...] = _dot_nt(wrt_ref[...], h2, ROUTER_PREC)


def out_proj(y_rwkv, y_s5, u, x, mods3, s5_d, w_glu_bf, b_glu, w_out_bf, g_post, g_pre, w_router_t,
             seq, tm):
    n_tok, dm = x.shape
    wr, ws = y_rwkv.shape[-1], y_s5.shape[-1]
    n_e = w_router_t.shape[0]
    tpb = seq // tm
    tok = lambda w: pl.BlockSpec((tm, w), lambda i: (i, 0))
    full = lambda a: pl.BlockSpec(a.shape, lambda i: (0,) * a.ndim)
    row = lambda t: t.reshape(1, -1)
    consts = [row(s5_d), w_glu_bf, row(b_glu), w_out_bf, row(g_post), row(g_pre), w_router_t]
    return pl.pallas_call(
        _outproj_kernel,
        grid=(n_tok // tm,),
        in_specs=[tok(wr), tok(ws), tok(ws), tok(dm),
                  pl.BlockSpec((1, N_MODS, dm), lambda i: (i // tpb, 0, 0))] + [full(a) for a in consts],
        out_specs=[tok(dm), tok(dm), pl.BlockSpec((n_e, tm), lambda i: (0, i))],
        out_shape=[jax.ShapeDtypeStruct((n_tok, dm), jnp.float32),
                   jax.ShapeDtypeStruct((n_tok, dm), jnp.float32),
                   jax.ShapeDtypeStruct((n_e, n_tok), jnp.float32)],
        compiler_params=_cparams("arbitrary"),
        name="outproj",
    )(y_rwkv, y_s5, u, x, mods3, *consts)


def _route_kernel(lg_ref, bias_ref, tri_ref, e_ref, w_ref, r_ref, cnt_ref, carry_ref):
    i = pl.program_id(0)

    @pl.when(i == 0)
    def _():
        carry_ref[...] = jnp.zeros_like(carry_ref)

    neg = -jnp.inf
    scores = _sigmoid(lg_ref[...])
    n_e, tm = scores.shape
    choice = scores + bias_ref[...]
    gsz = n_e // ROUTE_GROUPS
    c3 = choice.reshape(ROUTE_GROUPS, gsz, tm)
    io = lax.broadcasted_iota(jnp.int32, c3.shape, 1)
    m1 = jnp.max(c3, axis=1, keepdims=True)
    first = jnp.min(jnp.where(c3 == m1, io, gsz), axis=1, keepdims=True)
    m2 = jnp.max(jnp.where(io == first, neg, c3), axis=1, keepdims=True)
    gs = m1 + m2
    gi = lax.broadcasted_iota(jnp.int32, gs.shape, 0)
    rank = jnp.zeros(gs.shape, jnp.int32)
    for j in range(ROUTE_GROUPS):
        gj = gs[j:j + 1]
        beats = (gj > gs) | ((gj == gs) & (gi > j))
        rank = rank + beats.astype(jnp.int32)
    masked = jnp.where(rank < ROUTE_TOPK_GROUPS, c3, neg).reshape(n_e, tm)

    eio = lax.broadcasted_iota(jnp.int32, (n_e, tm), 0)
    ids, ws = [], []
    mhot = jnp.zeros((n_e, tm), jnp.float32)
    for _ in range(TOP_K):
        m = jnp.max(masked, axis=0, keepdims=True)
        idx = jnp.min(jnp.where(masked == m, eio, n_e), axis=0, keepdims=True)
        sel = eio == idx
        ws.append(jnp.sum(jnp.where(sel, scores, 0.0), axis=0, keepdims=True))
        ids.append(idx)
        masked = jnp.where(sel, neg, masked)
        mhot = jnp.where(sel, 1.0, mhot)
    wsum = ws[0]
    for t in ws[1:]:
        wsum = wsum + t
    before = _dot(_bf(mhot), tri_ref[...]) + carry_ref[...]
    ranks = [jnp.sum(jnp.where(eio == idx, before, 0.0), axis=0, keepdims=True) for idx in ids]
    e_ref[...] = jnp.concatenate(ids, axis=0)
    w_ref[...] = jnp.concatenate(ws, axis=0) / wsum * ROUTE_SCALE
    r_ref[...] = jnp.concatenate(ranks, axis=0).astype(jnp.int32)
    carry_ref[...] = carry_ref[...] + jnp.sum(mhot, axis=1, keepdims=True)
    cnt_ref[...] = carry_ref[...]


def route(logits_t, router_bias, tm):
    n_e, n_tok = logits_t.shape
    tri = (jnp.arange(tm)[:, None] < jnp.arange(tm)[None, :]).astype(jnp.bfloat16)
    kt = lambda: pl.BlockSpec((TOP_K, tm), lambda i: (0, i))
    return pl.pallas_call(
        _route_kernel,
        grid=(n_tok // tm,),
        in_specs=[pl.BlockSpec((n_e, tm), lambda i: (0, i)),
                  pl.BlockSpec((n_e, 1), lambda i: (0, 0)),
                  pl.BlockSpec((tm, tm), lambda i: (0, 0))],
        out_specs=[kt(), kt(), kt(), pl.BlockSpec((n_e, 1), lambda i: (0, 0))],
        out_shape=[jax.ShapeDtypeStruct((TOP_K, n_tok), jnp.int32),
                   jax.ShapeDtypeStruct((TOP_K, n_tok), jnp.float32),
                   jax.ShapeDtypeStruct((TOP_K, n_tok), jnp.int32),
                   jax.ShapeDtypeStruct((n_e, 1), jnp.float32)],
        scratch_shapes=[pltpu.VMEM((n_e, 1), jnp.float32)],
        compiler_params=_cparams("arbitrary"),
        name="route",
    )(logits_t, router_bias.reshape(n_e, 1), tri)


def _dispatch_kernel(pstart_ref, e_ref, r_ref, h_ref, xs_ref, dest_ref, sem):
    td = h_ref.shape[0]

    def row_copy(t, d):
        return pltpu.make_async_copy(h_ref.at[pl.ds(t, 1), :], xs_ref.at[pl.ds(d, 1), :], sem)

    def body(t, carry):
        for k in range(TOP_K):
            d = pstart_ref[e_ref[k, t]] + r_ref[k, t]
            dest_ref[k, t] = d
            row_copy(t, d).start()
        return carry

    lax.fori_loop(0, td, body, 0)

    def drain(t, carry):
        for k in range(TOP_K):
            row_copy(t, dest_ref[k, t]).wait()
        return carry

    lax.fori_loop(0, td, drain, 0)


def dispatch(pad_start, e_idx, rank, h2, n_rows, td):
    n_tok, dm = h2.shape
    smem_kt = lambda: pl.BlockSpec((TOP_K, td), lambda i, ps: (0, i), memory_space=pltpu.SMEM)
    return pl.pallas_call(
        _dispatch_kernel,
        grid_spec=pltpu.PrefetchScalarGridSpec(
            num_scalar_prefetch=1,
            grid=(n_tok // td,),
            in_specs=[smem_kt(), smem_kt(), pl.BlockSpec((td, dm), lambda i, ps: (i, 0))],
            out_specs=[pl.BlockSpec(memory_space=pl.ANY), smem_kt()],
            scratch_shapes=[pltpu.SemaphoreType.DMA(())]),
        out_shape=[jax.ShapeDtypeStruct((n_rows, dm), h2.dtype),
                   jax.ShapeDtypeStruct((TOP_K, n_tok), jnp.int32)],
        compiler_params=_cparams("arbitrary"),
        name="dispatch",
    )(pad_start, e_idx, rank, h2)


def _expert_kernel(te_ref, nu_ref, x_ref, wg_ref, wu_ref, wd_ref, o_ref):
    @pl.when(pl.program_id(0) < nu_ref[0])
    def _():
        x = _bf(x_ref[...])
        hid = _silu(_dot(x, _bf(wg_ref[0]))) * _dot(x, _bf(wu_ref[0]))
        o_ref[...] = _dot(_bf(hid), _bf(wd_ref[0]))


def expert_ffn(tile_e, n_used, xs, w_gate, w_up, w_down, bm):
    n_rows, dm = xs.shape
    ff = w_gate.shape[-1]
    rows = lambda: pl.BlockSpec((bm, dm), lambda i, te, nu: (jnp.minimum(i, nu[0] - 1), 0))
    return pl.pallas_call(
        _expert_kernel,
        grid_spec=pltpu.PrefetchScalarGridSpec(
            num_scalar_prefetch=2,
            grid=(n_rows // bm,),
            in_specs=[rows(),
                      pl.BlockSpec((1, dm, ff), lambda i, te, nu: (te[i], 0, 0)),
                      pl.BlockSpec((1, dm, ff), lambda i, te, nu: (te[i], 0, 0)),
                      pl.BlockSpec((1, ff, dm), lambda i, te, nu: (te[i], 0, 0))],
            out_specs=rows()),
        out_shape=jax.ShapeDtypeStruct((n_rows, dm), jnp.float32),
        compiler_params=_cparams("arbitrary"),
        name="expert",
    )(tile_e, n_used, xs, w_gate, w_up, w_down)


def _combine_kernel(dest_ref, w_ref, x1_ref, h2_ref, mods_ref, g_ref, wsg_ref, wsu_ref, wsd_ref,
                    ys_ref, o_ref, gbuf, sem):
    tc = x1_ref.shape[0]

    def row_copy(t, k):
        return pltpu.make_async_copy(ys_ref.at[pl.ds(dest_ref[k, t], 1), :],
                                     gbuf.at[k, pl.ds(t, 1), :], sem)

    def issue(t, carry):
        for k in range(TOP_K):
            row_copy(t, k).start()
        return carry

    lax.fori_loop(0, tc, issue, 0)

    h = _bf(h2_ref[...])
    hid = _silu(_dot(h, wsg_ref[...])) * _dot(h, wsu_ref[...])
    y = _dot(_bf(hid), wsd_ref[...])

    def drain(t, carry):
        for k in range(TOP_K):
            row_copy(t, k).wait()
        return carry

    lax.fori_loop(0, tc, drain, 0)

    w = w_ref[...]
    for k in range(TOP_K):
        y = y + w[:, k:k + 1] * gbuf[k]
    m = mods_ref[0]
    o_ref[...] = x1_ref[...] + m[5:6, :] * _rms(y, g_ref[...])


def combine(dest, w_tk, x1, h2, mods3, g_post, ws_gate_bf, ws_up_bf, ws_down_bf, ys, seq, tc):
    n_tok, dm = x1.shape
    tpb = seq // tc
    full = lambda a: pl.BlockSpec(a.shape, lambda i: (0,) * a.ndim)
    tok = lambda: pl.BlockSpec((tc, dm), lambda i: (i, 0))
    g2 = g_post.reshape(1, dm)
    return pl.pallas_call(
        _combine_kernel,
        grid=(n_tok // tc,),
        in_specs=[pl.BlockSpec((TOP_K, tc), lambda i: (0, i), memory_space=pltpu.SMEM),
                  pl.BlockSpec((tc, TOP_K), lambda i: (i, 0)),
                  tok(), tok(),
                  pl.BlockSpec((1, N_MODS, dm), lambda i: (i // tpb, 0, 0)),
                  full(g2), full(ws_gate_bf), full(ws_up_bf), full(ws_down_bf),
                  pl.BlockSpec(memory_space=pl.ANY)],
        out_specs=tok(),
        out_shape=jax.ShapeDtypeStruct((n_tok, dm), jnp.float32),
        scratch_shapes=[pltpu.VMEM((TOP_K, tc, dm), jnp.float32), pltpu.SemaphoreType.DMA(())],
        compiler_params=_cparams("arbitrary"),
        name="combine",
    )(dest, w_tk, x1, h2, mods3, g2, ws_gate_bf, ws_up_bf, ws_down_bf, ys)


def moe_ffn(h2, x1, logits_t, mods3, router_bias, w_gate, w_up, w_down, ws_gate, ws_up, ws_down,
            g_post, seq):
    n_tok, dm = h2.shape
    n_e = w_gate.shape[0]
    bm = EXPERT_ROW_TILE
    e_idx, w_kt, rank, cnt = route(logits_t, router_bias, min(512, n_tok))
    counts = cnt[:, 0].astype(jnp.int32)
    padded = (counts + bm - 1) // bm * bm
    pad_end = jnp.cumsum(padded)
    pad_start = (pad_end - padded).astype(jnp.int32)
    n_tiles = -(-(n_tok * TOP_K) // bm) + n_e
    n_used = (pad_end[-1] // bm).astype(jnp.int32)
    tile_e = jnp.minimum(jnp.searchsorted(pad_end, jnp.arange(n_tiles, dtype=jnp.int32) * bm, side="right"),
                         n_e - 1).astype(jnp.int32)
    tile_e = jnp.where(jnp.arange(n_tiles) < n_used, tile_e, tile_e[n_used - 1])
    xs, dest = dispatch(pad_start, e_idx, rank, h2, n_tiles * bm, min(256, n_tok))
    ys = expert_ffn(tile_e, n_used.reshape(1), xs, w_gate, w_up, w_down, bm)
    bf = lambda a: a.astype(jnp.bfloat16)
    return combine(dest, w_kt.T, x1, h2, mods3, g_post, bf(ws_gate), bf(ws_up), bf(ws_down), ys,
                   seq, min(256, n_tok))


def kernel(x, c, w_ada, b_ada, g_pre_mix, g_post_mix, g_pre_ffn, g_post_ffn, w_in, mu_shift, rwkv_w0, rwkv_w2, rwkv_a0, rwkv_a2, rwkv_g2, rwkv_k_k, rwkv_k_a, rwkv_r_k, rwkv_ln_w, rwkv_ln_b, s5_log_dt, s5_a_re, s5_a_im, s5_b_re, s5_b_im, s5_c_re, s5_c_im, s5_d, s5_w_glu, s5_b_glu, w_out, w_router, router_bias, w_gate, w_up, w_down, ws_gate, ws_up, ws_down):
    bsz, seq, dm = x.shape
    depth = w_ada.shape[0]
    bf = lambda a: a.astype(jnp.bfloat16)
    tm = min(512, seq)
    for l in range(depth):
        mods3 = ada_mods(c, w_ada[l], b_ada[l]).reshape(bsz, N_MODS, dm)
        n_rwkv = mu_shift.shape[-1]
        p, u = in_proj(x, mods3, g_pre_mix[l], bf(w_in[l]), n_rwkv, tm)
        y_rwkv = rwkv_mixer(p, mu_shift[l], rwkv_w0[l], rwkv_w2[l], rwkv_a0[l], rwkv_a2[l],
                            rwkv_g2[l], rwkv_k_k[l], rwkv_k_a[l], rwkv_r_k[l].reshape(-1),
                            rwkv_ln_w[l], rwkv_ln_b[l])
        y_s5 = s5_core(u, s5_log_dt[l], s5_a_re[l], s5_a_im[l], s5_b_re[l], s5_b_im[l],
                       s5_c_re[l], s5_c_im[l])
        flat = lambda a: a.reshape(bsz * seq, a.shape[-1])
        x1, h2, logits_t = out_proj(flat(y_rwkv), flat(y_s5), flat(u), flat(x), mods3, s5_d[l],
                                    bf(s5_w_glu[l]), s5_b_glu[l], bf(w_out[l]), g_post_mix[l],
                                    g_pre_ffn[l], w_router[l].T, seq, tm)
        out = moe_ffn(h2, x1, logits_t, mods3, router_bias[l], w_gate[l], w_up[l], w_down[l],
                      ws_gate[l], ws_up[l], ws_down[l], g_post_ffn[l], seq)
        x = out.reshape(bsz, seq, dm)
    return x
```

```python
import functools
import math

import jax
import jax.numpy as jnp
from jax import lax
from jax.experimental import pallas as pl
from jax.experimental.pallas import tpu as pltpu

NORM_EPS = 1e-6
LNX_EPS = 64e-5
L2_EPS = 1e-12
S5_MAX_REAL = -1e-4
ROUTE_SCALE = 2.5
N_MODS = 6

RWKV_HEAD_DIM = 64
DECAY_LORA = 64
ICLR_LORA = 64
GATE_LORA = 128
S5_GROUP = 16
S5_STATE = 64
TOP_K = 8
ROUTE_GROUPS = 8
ROUTE_TOPK_GROUPS = 4

RWKV_CHUNK = 64
S5_CHUNK = 32
EXPERT_ROW_TILE = 256

VMEM_LIMIT = 56 * 1024 * 1024

HI = lax.Precision.HIGHEST


def _cparams(*sem):
    return pltpu.CompilerParams(dimension_semantics=sem, vmem_limit_bytes=VMEM_LIMIT)


def _dot(a, b, precision=None):
    return jnp.dot(a, b, preferred_element_type=jnp.float32, precision=precision)


def _dot_nt(a, b, precision=None):
    return lax.dot_general(a, b, (((1,), (1,)), ((), ())),
                           preferred_element_type=jnp.float32, precision=precision)


def _dot_tn(a, b, precision=None):
    return lax.dot_general(a, b, (((0,), (0,)), ((), ())),
                           preferred_element_type=jnp.float32, precision=precision)


def _bdot(a, b):
    return lax.dot_general(a, b, (((2,), (1,)), ((0,), (0,))), preferred_element_type=jnp.float32)


def _bdot_nt(a, b):
    return lax.dot_general(a, b, (((2,), (2,)), ((0,), (0,))), preferred_element_type=jnp.float32)


def _bf(x):
    return x.astype(jnp.bfloat16)


def _pack_bf16_pair(hi, lo):
    hb = lax.bitcast_convert_type(_bf(hi).astype(jnp.float32), jnp.uint32)
    lb = lax.bitcast_convert_type(_bf(lo).astype(jnp.float32), jnp.uint32)
    return (hb & jnp.uint32(0xFFFF0000)) | (lb >> 16)


def _unpack_bf16_pair(w):
    hi = lax.bitcast_convert_type(w & jnp.uint32(0xFFFF0000), jnp.float32)
    lo = lax.bitcast_convert_type(w << 16, jnp.float32)
    return _bf(hi), _bf(lo)


LANES = 128


def _row_tile_spec(rows, width, row_block):
    return pl.BlockSpec((rows * (width // LANES), LANES), lambda *a: (row_block(*a), 0))


def _store_row_tiles(ref, x):
    rows = x.shape[0]
    n_s = ref.shape[0] // rows
    for s in range(n_s):
        ref[pl.ds(s, rows, stride=n_s), :] = x[:, s * LANES:(s + 1) * LANES]


def _load_row_tiles(ref, rows):
    n_s = ref.shape[0] // rows
    return jnp.concatenate([ref[pl.ds(s, rows, stride=n_s), :] for s in range(n_s)], axis=1)


def _row(ref, r, n_s):
    return ref.at[pl.ds(pl.multiple_of(r * n_s, n_s), n_s), :]


def _sigmoid(x):
    return 1.0 / (1.0 + jnp.exp(-x))


def _silu(x):
    return x * _sigmoid(x)


def _rms(x, gain):
    return x * lax.rsqrt(jnp.mean(x * x, axis=-1, keepdims=True) + NORM_EPS) * gain


def _ada_kernel(c_ref, w_ref, b_ref, o_ref):
    c = c_ref[...]
    o_ref[...] = _dot(_silu(c), w_ref[...], HI) + b_ref[...]


def ada_mods(c, w_ada, b_ada):
    bsz, dm = c.shape
    n = w_ada.shape[1]
    tn = dm
    return pl.pallas_call(
        _ada_kernel,
        grid=(n // tn,),
        in_specs=[pl.BlockSpec((bsz, dm), lambda j: (0, 0)),
                  pl.BlockSpec((dm, tn), lambda j: (0, j)),
                  pl.BlockSpec((1, tn), lambda j: (0, j))],
        out_specs=pl.BlockSpec((bsz, tn), lambda j: (0, j)),
        out_shape=jax.ShapeDtypeStruct((bsz, n), jnp.float32),
        compiler_params=_cparams("arbitrary"),
        name="ada",
    )(c, w_ada, b_ada.reshape(1, n))


def _inproj_kernel(x_ref, mods_ref, g_ref, w_ref, p_ref, u_ref):
    x = x_ref[0]
    m = mods_ref[0]
    h = _rms(x, g_ref[...]) * (1.0 + m[1:2, :]) + m[0:1, :]
    proj = _dot(_bf(h), w_ref[...])
    n_p = p_ref.shape[-1]
    p_ref[0] = proj[:, :n_p]
    u_ref[0] = proj[:, n_p:]


def in_proj(x, mods3, g_pre, w_in_bf, n_rwkv, tm):
    bsz, seq, dm = x.shape
    n = w_in_bf.shape[1]
    n_s5 = n - n_rwkv
    return pl.pallas_call(
        _inproj_kernel,
        grid=(bsz, seq // tm),
        in_specs=[pl.BlockSpec((1, tm, dm), lambda b, i: (b, i, 0)),
                  pl.BlockSpec((1, N_MODS, dm), lambda b, i: (b, 0, 0)),
                  pl.BlockSpec((1, dm), lambda b, i: (0, 0)),
                  pl.BlockSpec((dm, n), lambda b, i: (0, 0))],
        out_specs=[pl.BlockSpec((1, tm, n_rwkv), lambda b, i: (b, i, 0)),
                   pl.BlockSpec((1, tm, n_s5), lambda b, i: (b, i, 0))],
        out_shape=[jax.ShapeDtypeStruct((bsz, seq, n_rwkv), jnp.float32),
                   jax.ShapeDtypeStruct((bsz, seq, n_s5), jnp.float32)],
        compiler_params=_cparams("arbitrary", "arbitrary"),
        name="inproj",
    )(x, mods3, g_pre.reshape(1, dm), w_in_bf)


RWKV_TILE = 256


def _split_dot(x, m01, terms):
    acc = None
    rem = x
    for _ in range(terms):
        piece = _bf(rem)
        part = _dot(piece, m01)
        acc = part if acc is None else acc + part
        rem = rem - piece.astype(jnp.float32)
    return acc


def _rwkv_kernel(p_ref, mu_ref, w0_ref, w2_ref, a0_ref, a2_ref, g2_ref, kk_ref, ka_ref,
                  rk_ref, lnw_ref, lnb_ref, bd_ref, o_ref,
                  s_ref, carry_ref, rt_ref, at_ref, bt_ref, kt_ref, v_ref, wl_ref, y_ref,
                  *, width, chunk):
    hd = RWKV_HEAD_DIM
    pw = 2 * hd
    pairs = width // pw
    i = pl.program_id(1)

    @pl.when(i == 0)
    def _():
        s_ref[...] = jnp.zeros_like(s_ref)
        carry_ref[...] = jnp.zeros_like(carry_ref)

    p = p_ref[0]
    n_t = p.shape[0]
    n_chunks = n_t // chunk
    row = lax.broadcasted_iota(jnp.int32, p.shape, 0)
    prev = jnp.where(row == 0, carry_ref[0:1, :], pltpu.roll(p, 1, axis=0))
    carry_ref[0:1, :] = p[n_t - 1:n_t, :]
    pm = p + (prev - p) * mu_ref[...]

    r = pm[:, 0:width]
    k = pm[:, width:2 * width]
    v = pm[:, 2 * width:3 * width]
    c0 = 3 * width
    w_lo = pm[:, c0:c0 + DECAY_LORA]
    a_lo = pm[:, c0 + DECAY_LORA:c0 + DECAY_LORA + ICLR_LORA]
    g_lo = pm[:, c0 + DECAY_LORA + ICLR_LORA:]

    z = w0_ref[...] + _dot(_bf(jnp.tanh(w_lo)), w2_ref[...])
    softplus_neg = jnp.maximum(-z, 0.0) + jnp.log(1.0 + jnp.exp(-jnp.abs(z)))
    logd = -jnp.exp(-softplus_neg - 0.5)
    iclr = _sigmoid(a0_ref[...] + _dot(_bf(a_lo), a2_ref[...]))
    gate = _dot(_bf(_sigmoid(g_lo)), g2_ref[...])

    bd = bd_ref[...]
    kk = k * kk_ref[...]
    kk = kk / jnp.maximum(jnp.sqrt(_split_dot(kk * kk, bd, 2)), L2_EPS)
    k2 = k * (1.0 + (iclr - 1.0) * ka_ref[...])
    bonus = _split_dot(r * k2 * rk_ref[...], bd, 2) * v

    ti = lax.broadcasted_iota(jnp.int32, (n_t, n_t), 0)
    si = lax.broadcasted_iota(jnp.int32, (n_t, n_t), 1)
    tri = jnp.where((ti >= si) & (ti // chunk == si // chunk), 1.0, 0.0).astype(jnp.bfloat16)
    cum = _split_dot_lhs(tri, logd, 3)
    e_pos = jnp.exp(cum)
    e_neg = jnp.exp(-cum)
    rt_ref[...] = _bf(r * e_pos)
    at_ref[...] = _bf(-kk * jnp.exp(cum - logd))
    bt_ref[...] = _bf(kk * iclr * e_neg)
    kt_ref[...] = _bf(k2 * e_neg)
    v_ref[...] = _bf(v)
    for c in range(n_chunks):
        wl_ref[c:c + 1, :] = e_pos[(c + 1) * chunk - 1:(c + 1) * chunk, :]

    two_l = 2 * chunk
    lane = lax.broadcasted_iota(jnp.int32, (chunk, pw), 1)
    lane0 = lane < hd
    bi = lax.broadcasted_iota(jnp.int32, (two_l, two_l), 0)
    bj = lax.broadcasted_iota(jnp.int32, (two_l, two_l), 1)
    same = (bi // chunk) == (bj // chunk)
    low_strict = same & (bi > bj)
    low_incl = same & (bi >= bj)
    eye_t = jnp.where(bi == bj, 1.0, 0.0)
    pi_ = lax.broadcasted_iota(jnp.int32, (pw, pw), 0)
    pj_ = lax.broadcasted_iota(jnp.int32, (pw, pw), 1)
    eye_p = jnp.where(pi_ == pj_, 1.0, 0.0)
    n_sq = max(1, int(math.ceil(math.log2(chunk))) - 1)
    zero = jnp.zeros((), jnp.bfloat16)

    def stack(x):
        return jnp.concatenate([jnp.where(lane0, x, zero), jnp.where(lane0, zero, x)], axis=0)

    def gather(ref):
        return jnp.stack([stack(ref[c * chunk:(c + 1) * chunk, hp * pw:(hp + 1) * pw])
                          for c in range(n_chunks) for hp in range(pairs)], axis=0)

    a_s, b_s, k_s, r_s, v_s = (gather(ref) for ref in (at_ref, bt_ref, kt_ref, rt_ref, v_ref))
    wl = jnp.stack([wl_ref[c:c + 1, hp * pw:(hp + 1) * pw]
                    for c in range(n_chunks) for hp in range(pairs)], axis=0)
    gram = _bdot_nt(jnp.concatenate([a_s, r_s], axis=1), jnp.concatenate([b_s, k_s], axis=1))
    m_ab = jnp.where(low_strict, gram[:, :two_l, :two_l], 0.0)
    m_ak = jnp.where(low_strict, gram[:, :two_l, two_l:], 0.0)
    n_rb = jnp.where(low_incl, gram[:, two_l:, :two_l], 0.0)
    n_rk = jnp.where(low_incl, gram[:, two_l:, two_l:], 0.0)
    t_inv = eye_t + m_ab
    m_pow = _bf(m_ab)
    for _ in range(n_sq):
        m_pow = _bf(_bdot(m_pow, m_pow))
        t_inv = t_inv + _bdot(_bf(t_inv), m_pow)
    makv = _bdot(_bf(m_ak), v_s)
    tx_bf = _bf(_bdot(_bf(t_inv), jnp.concatenate([a_s, _bf(makv)], axis=2)))
    nx = _bdot(_bf(n_rb), tx_bf)
    rbar = _bf(r_s.astype(jnp.float32) + nx[:, :, :pw])
    y0 = nx[:, :, pw:] + _bdot(_bf(n_rk), v_s)
    tb = _bdot(jnp.swapaxes(tx_bf, 1, 2), b_s)
    pmat = _bf((eye_p + tb[:, :pw, :]) * wl)
    dmat = (tb[:, pw:, :] + _bdot(jnp.swapaxes(v_s, 1, 2), k_s)) * wl

    s = s_ref[...]
    for c in range(n_chunks):
        sel = slice(c * pairs, (c + 1) * pairs)
        s_bf = _bf(s)
        ys = _bdot_nt(rbar[sel], s_bf) + y0[sel]
        yc_ = ys[:, :chunk, :] + ys[:, chunk:, :]
        for hp in range(pairs):
            y_ref[c * chunk:(c + 1) * chunk, hp * pw:(hp + 1) * pw] = yc_[hp]
        s = _bdot(s_bf, pmat[sel]) + dmat[sel]
    s_ref[...] = s

    y = y_ref[...]
    inv_hd = 1.0 / hd
    mean = _split_dot(y, bd, 2) * inv_hd
    yc = y - mean
    var = _split_dot(yc * yc, bd, 2) * inv_hd
    yn = yc * lax.rsqrt(var + LNX_EPS) * lnw_ref[...] + lnb_ref[...]
    o_ref[0] = (yn + bonus) * gate


def _split_dot_lhs(m01, x, terms):
    acc = None
    rem = x
    for _ in range(terms):
        piece = _bf(rem)
        part = _dot(m01, piece)
        acc = part if acc is None else acc + part
        rem = rem - piece.astype(jnp.float32)
    return acc


def rwkv_mixer(p, mu, w0, w2, a0, a2, g2, k_k, k_a, r_k, ln_w, ln_b):
    bsz, seq, n_p = p.shape
    width = w0.shape[-1]
    pairs = width // (2 * RWKV_HEAD_DIM)
    chunk = min(RWKV_CHUNK, seq)
    tile = min(RWKV_TILE, seq)
    hid = jnp.arange(width, dtype=jnp.int32) // RWKV_HEAD_DIM
    bd = (hid[:, None] == hid[None, :]).astype(jnp.bfloat16)
    row = lambda t: t.reshape(1, -1)
    full = lambda a: pl.BlockSpec(a.shape, lambda b, i: (0,) * a.ndim)
    consts = [row(mu), row(w0), _bf(w2), row(a0), _bf(a2), _bf(g2), row(k_k), row(k_a), row(r_k),
              row(ln_w), row(ln_b), bd]
    act = lambda: pltpu.VMEM((tile, width), jnp.bfloat16)
    return pl.pallas_call(
        functools.partial(_rwkv_kernel, width=width, chunk=chunk),
        grid=(bsz, seq // tile),
        in_specs=[pl.BlockSpec((1, tile, n_p), lambda b, i: (b, i, 0))] + [full(a) for a in consts],
        out_specs=pl.BlockSpec((1, tile, width), lambda b, i: (b, i, 0)),
        out_shape=jax.ShapeDtypeStruct((bsz, seq, width), jnp.float32),
        scratch_shapes=[pltpu.VMEM((pairs, 2 * RWKV_HEAD_DIM, 2 * RWKV_HEAD_DIM), jnp.float32),
                        pltpu.VMEM((8, n_p), jnp.float32),
                        act(), act(), act(), act(), act(),
                        pltpu.VMEM((max(8, tile // chunk), width), jnp.float32),
                        pltpu.VMEM((tile, width), jnp.float32)],
        compiler_params=_cparams("arbitrary", "arbitrary"),
        name="rwkv",
    )(p, *consts)


S5_PREC = HI


def s5_operators(log_dt, a_re, a_im, b_re, b_im, c_re, c_im, n_chunks, lc):
    n_g, n_p = a_re.shape
    n_c = b_re.shape[-1]
    dt = jnp.exp(log_dt)[:, None]
    lam_re = jnp.minimum(a_re, S5_MAX_REAL)
    lam_im = a_im
    mag = jnp.exp(lam_re * dt)
    ang = lam_im * dt
    ab_re, ab_im = mag * jnp.cos(ang), mag * jnp.sin(ang)
    den = lam_re * lam_re + lam_im * lam_im
    n_re, n_im = ab_re - 1.0, ab_im
    q_re = (n_re * lam_re + n_im * lam_im) / den
    q_im = (n_im * lam_re - n_re * lam_im) / den
    bb_re = q_re[..., None] * b_re - q_im[..., None] * b_im
    bb_im = q_re[..., None] * b_im + q_im[..., None] * b_re
    pw_re, pw_im = [jnp.ones_like(ab_re)], [jnp.zeros_like(ab_im)]
    for _ in range(lc):
        pr, pi = pw_re[-1], pw_im[-1]
        pw_re.append(pr * ab_re - pi * ab_im)
        pw_im.append(pr * ab_im + pi * ab_re)
    pw_re, pw_im = jnp.stack(pw_re), jnp.stack(pw_im)
    ab_b_re = pw_re[:lc, :, :, None] * bb_re[None] - pw_im[:lc, :, :, None] * bb_im[None]
    ab_b_im = pw_re[:lc, :, :, None] * bb_im[None] + pw_im[:lc, :, :, None] * bb_re[None]
    kern = (jnp.einsum("gcp,lgpd->lgcd", c_re, ab_b_re, precision=HI)
            - jnp.einsum("gcp,lgpd->lgcd", c_im, ab_b_im, precision=HI))
    kern = jnp.concatenate([kern, jnp.zeros_like(kern[:1])], axis=0)
    s_i = jnp.arange(lc)[:, None]
    t_i = jnp.arange(lc)[None, :]
    lag = jnp.where(t_i >= s_i, t_i - s_i, lc)
    toep = kern[lag]
    toep = jnp.transpose(toep, (2, 0, 4, 1, 3)).reshape(n_g, lc * n_c, lc * n_c)
    rev_re, rev_im = ab_b_re[::-1], ab_b_im[::-1]
    pm = jnp.concatenate([jnp.transpose(rev_re, (1, 0, 3, 2)).reshape(n_g, lc * n_c, n_p),
                          jnp.transpose(rev_im, (1, 0, 3, 2)).reshape(n_g, lc * n_c, n_p)], axis=-1)
    ca_re = c_re[None] * pw_re[1:, :, None, :] - c_im[None] * pw_im[1:, :, None, :]
    ca_im = c_re[None] * pw_im[1:, :, None, :] + c_im[None] * pw_re[1:, :, None, :]
    q = jnp.concatenate([jnp.transpose(ca_re, (1, 3, 0, 2)).reshape(n_g, n_p, lc * n_c),
                         jnp.transpose(-ca_im, (1, 3, 0, 2)).reshape(n_g, n_p, lc * n_c)], axis=1)
    n_lvl = max(1, int(math.ceil(math.log2(n_chunks))))
    cr, ci = pw_re[lc], pw_im[lc]
    lv = []
    for _ in range(n_lvl):
        lv.append(jnp.stack([jnp.concatenate([cr, cr], -1), jnp.concatenate([-ci, ci], -1)], axis=1))
        cr, ci = cr * cr - ci * ci, 2.0 * cr * ci
    lvl = jnp.stack(lv, axis=1)
    return toep, pm, q, lvl


def _s5_kernel(u_ref, toep_ref, pm_ref, q_ref, lvl_ref, y_ref, *, n_chunks):
    prec = S5_PREC
    u = u_ref[0]
    e = _dot(u, pm_ref[0], prec)
    rows, two_p = e.shape
    half = two_p // 2
    cidx = lax.broadcasted_iota(jnp.int32, e.shape, 0) % n_chunks
    x = e
    n_lvl = lvl_ref.shape[1]
    for j in range(n_lvl):
        sh = 1 << j
        if sh >= n_chunks:
            break
        xs = jnp.where(cidx >= sh, pltpu.roll(x, sh, axis=0), 0.0)
        cf = lvl_ref[0, j]
        x = x + xs * cf[0:1, :] + pltpu.roll(xs, half, axis=1) * cf[1:2, :]
    x_in = jnp.where(cidx >= 1, pltpu.roll(x, 1, axis=0), 0.0)
    y_ref[0] = _dot(u, toep_ref[0], prec) + _dot(x_in, q_ref[0], prec)


def s5_core(u, log_dt, a_re, a_im, b_re, b_im, c_re, c_im):
    bsz, seq, width = u.shape
    n_g, n_p = a_re.shape
    n_c = width // n_g
    lc = min(S5_CHUNK, seq)
    n_chunks = seq // lc
    toep, pm, q, lvl = s5_operators(log_dt, a_re, a_im, b_re, b_im, c_re, c_im, n_chunks, lc)
    ug = u.reshape(bsz, n_chunks, lc, n_g, n_c)
    ug = jnp.transpose(ug, (3, 0, 1, 2, 4)).reshape(n_g, bsz * n_chunks, lc * n_c)
    rows, lw = bsz * n_chunks, lc * n_c
    yg = pl.pallas_call(
        functools.partial(_s5_kernel, n_chunks=n_chunks),
        grid=(n_g,),
        in_specs=[pl.BlockSpec((1, rows, lw), lambda g: (g, 0, 0)),
                  pl.BlockSpec((1, lw, lw), lambda g: (g, 0, 0)),
                  pl.BlockSpec((1, lw, 2 * n_p), lambda g: (g, 0, 0)),
                  pl.BlockSpec((1, 2 * n_p, lw), lambda g: (g, 0, 0)),
                  pl.BlockSpec((1,) + lvl.shape[1:], lambda g: (g, 0, 0, 0))],
        out_specs=pl.BlockSpec((1, rows, lw), lambda g: (g, 0, 0)),
        out_shape=jax.ShapeDtypeStruct((n_g, rows, lw), jnp.float32),
        compiler_params=_cparams("arbitrary"),
        name="s5",
    )(ug, toep, pm, q, lvl)
    yg = yg.reshape(n_g, bsz, n_chunks, lc, n_c)
    return jnp.transpose(yg, (1, 2, 3, 0, 4)).reshape(bsz, seq, width)


ROUTER_PREC = HI


def _gelu_tanh(y):
    return 0.5 * y * (1.0 + jnp.tanh(math.sqrt(2.0 / math.pi) * (y + 0.044715 * (y * y * y))))


def _outproj_kernel(yr_ref, ys_ref, u_ref, x_ref, mods_ref, d_ref, wglu_ref, bglu_ref, wout_ref,
                    gpost_ref, gpre_ref, wrt_ref, x1_ref, h2_ref, lg_ref):
    m = mods_ref[0]
    yr = yr_ref[...]
    y5 = _gelu_tanh(ys_ref[...] + d_ref[...] * u_ref[...])
    y5 = y5 * _sigmoid(_dot(_bf(y5), wglu_ref[...]) + bglu_ref[...])
    wr = yr.shape[-1]
    mixed = _dot(_bf(yr), wout_ref[0:wr, :]) + _dot(_bf(y5), wout_ref[wr:, :])
    x1 = x_ref[...] + m[2:3, :] * _rms(mixed, gpost_ref[...])
    x1_ref[...] = x1
    h2 = _rms(x1, gpre_ref[...]) * (1.0 + m[4:5, :]) + m[3:4, :]
    half = h2.shape[-1] // 2
    _store_row_tiles(h2_ref, _pack_bf16_pair(h2[:, :half], h2[:, half:]))
    lg_ref[...] = _dot_nt(wrt_ref[...], h2, ROUTER_PREC)


def out_proj(y_rwkv, y_s5, u, x, mods3, s5_d, w_glu_bf, b_glu, w_out_bf, g_post, g_pre, w_router_t,
             seq, tm):
    n_tok, dm = x.shape
    wr, ws = y_rwkv.shape[-1], y_s5.shape[-1]
    n_e = w_router_t.shape[0]
    tpb = seq // tm
    tok = lambda w: pl.BlockSpec((tm, w), lambda i: (i, 0))
    full = lambda a: pl.BlockSpec(a.shape, lambda i: (0,) * a.ndim)
    row = lambda t: t.reshape(1, -1)
    consts = [row(s5_d), w_glu_bf, row(b_glu), w_out_bf, row(g_post), row(g_pre), w_router_t]
    return pl.pallas_call(
        _outproj_kernel,
        grid=(n_tok // tm,),
        in_specs=[tok(wr), tok(ws), tok(ws), tok(dm),
                  pl.BlockSpec((1, N_MODS, dm), lambda i: (i // tpb, 0, 0))] + [full(a) for a in consts],
        out_specs=[tok(dm), _row_tile_spec(tm, dm // 2, lambda i: i),
                   pl.BlockSpec((n_e, tm), lambda i: (0, i))],
        out_shape=[jax.ShapeDtypeStruct((n_tok, dm), jnp.float32),
                   jax.ShapeDtypeStruct((n_tok * (dm // 2 // LANES), LANES), jnp.uint32),
                   jax.ShapeDtypeStruct((n_e, n_tok), jnp.float32)],
        compiler_params=_cparams("arbitrary"),
        name="outproj",
    )(y_rwkv, y_s5, u, x, mods3, *consts)


def _route_kernel(lg_ref, bias_ref, tri_ref, e_ref, w_ref, r_ref, cnt_ref, carry_ref):
    i = pl.program_id(0)

    @pl.when(i == 0)
    def _():
        carry_ref[...] = jnp.zeros_like(carry_ref)

    neg = -jnp.inf
    scores = _sigmoid(lg_ref[...])
    n_e, tm = scores.shape
    choice = scores + bias_ref[...]
    gsz = n_e // ROUTE_GROUPS
    c3 = choice.reshape(ROUTE_GROUPS, gsz, tm)
    io = lax.broadcasted_iota(jnp.int32, c3.shape, 1)
    m1 = jnp.max(c3, axis=1, keepdims=True)
    first = jnp.min(jnp.where(c3 == m1, io, gsz), axis=1, keepdims=True)
    m2 = jnp.max(jnp.where(io == first, neg, c3), axis=1, keepdims=True)
    gs = m1 + m2
    gi = lax.broadcasted_iota(jnp.int32, gs.shape, 0)
    rank = jnp.zeros(gs.shape, jnp.int32)
    for j in range(ROUTE_GROUPS):
        gj = gs[j:j + 1]
        beats = (gj > gs) | ((gj == gs) & (gi > j))
        rank = rank + beats.astype(jnp.int32)
    masked = jnp.where(rank < ROUTE_TOPK_GROUPS, c3, neg).reshape(n_e, tm)

    eio = lax.broadcasted_iota(jnp.int32, (n_e, tm), 0)
    ids, ws = [], []
    mhot = jnp.zeros((n_e, tm), jnp.float32)
    for _ in range(TOP_K):
        m = jnp.max(masked, axis=0, keepdims=True)
        idx = jnp.min(jnp.where(masked == m, eio, n_e), axis=0, keepdims=True)
        sel = eio == idx
        ws.append(jnp.sum(jnp.where(sel, scores, 0.0), axis=0, keepdims=True))
        ids.append(idx)
        masked = jnp.where(sel, neg, masked)
        mhot = jnp.where(sel, 1.0, mhot)
    wsum = ws[0]
    for t in ws[1:]:
        wsum = wsum + t
    before = _dot(_bf(mhot), tri_ref[...]) + carry_ref[...]
    ranks = [jnp.sum(jnp.where(eio == idx, before, 0.0), axis=0, keepdims=True) for idx in ids]
    e_ref[...] = jnp.concatenate(ids, axis=0)
    w_ref[...] = jnp.concatenate(ws, axis=0) / wsum * ROUTE_SCALE
    r_ref[...] = jnp.concatenate(ranks, axis=0).astype(jnp.int32)
    carry_ref[...] = carry_ref[...] + jnp.sum(mhot, axis=1, keepdims=True)
    cnt_ref[...] = carry_ref[...]


def route(logits_t, router_bias, tm):
    n_e, n_tok = logits_t.shape
    tri = (jnp.arange(tm)[:, None] < jnp.arange(tm)[None, :]).astype(jnp.bfloat16)
    kt = lambda: pl.BlockSpec((TOP_K, tm), lambda i: (0, i))
    return pl.pallas_call(
        _route_kernel,
        grid=(n_tok // tm,),
        in_specs=[pl.BlockSpec((n_e, tm), lambda i: (0, i)),
                  pl.BlockSpec((n_e, 1), lambda i: (0, 0)),
                  pl.BlockSpec((tm, tm), lambda i: (0, 0))],
        out_specs=[kt(), kt(), kt(), pl.BlockSpec((n_e, 1), lambda i: (0, 0))],
        out_shape=[jax.ShapeDtypeStruct((TOP_K, n_tok), jnp.int32),
                   jax.ShapeDtypeStruct((TOP_K, n_tok), jnp.float32),
                   jax.ShapeDtypeStruct((TOP_K, n_tok), jnp.int32),
                   jax.ShapeDtypeStruct((n_e, 1), jnp.float32)],
        scratch_shapes=[pltpu.VMEM((n_e, 1), jnp.float32)],
        compiler_params=_cparams("arbitrary"),
        name="route",
    )(logits_t, router_bias.reshape(n_e, 1), tri)


def _dest_kernel(e_ref, r_ref, ps_ref, d_ref):
    n_e = ps_ref.shape[0]
    tm = e_ref.shape[1]
    eio = lax.broadcasted_iota(jnp.int32, (n_e, tm), 0)
    ps = ps_ref[...]
    rows = [jnp.sum(jnp.where(eio == e_ref[k:k + 1, :], ps, 0.0), axis=0, keepdims=True)
            for k in range(TOP_K)]
    d_ref[...] = jnp.concatenate(rows, axis=0).astype(jnp.int32) + r_ref[...]


def dest_rows(e_idx, rank, pad_start, tm):
    n_tok = e_idx.shape[1]
    n_e = pad_start.shape[0]
    kt = lambda: pl.BlockSpec((TOP_K, tm), lambda i: (0, i))
    return pl.pallas_call(
        _dest_kernel,
        grid=(n_tok // tm,),
        in_specs=[kt(), kt(), pl.BlockSpec((n_e, 1), lambda i: (0, 0))],
        out_specs=kt(),
        out_shape=jax.ShapeDtypeStruct((TOP_K, n_tok), jnp.int32),
        compiler_params=_cparams("arbitrary"),
        name="dest",
    )(e_idx, rank, pad_start.astype(jnp.float32).reshape(n_e, 1))


def _dispatch_kernel(fill_start_ref, fill_len_ref, nu_ref, dest_ref, h_ref, xs_ref, zeros, sem, zsem,
                     *, n_s, bm, experts_per_step, tiles_per_step, n_tiles):
    i = pl.program_id(0)
    td = dest_ref.shape[1]

    @pl.when(i == 0)
    def _():
        zeros[...] = jnp.zeros_like(zeros)

    def body(t, carry):
        for k in range(TOP_K):
            pltpu.make_async_copy(_row(h_ref, t, n_s), _row(xs_ref, dest_ref[k, t], n_s),
                                  sem).start(priority=k % 2)
        return carry

    lax.fori_loop(0, td, body, 0)

    for j in range(experts_per_step):
        e = i * experts_per_step + j
        start, length = fill_start_ref[e], fill_len_ref[e]
        piece = bm // 2
        while piece >= 1:
            @pl.when((length & piece) != 0)
            def _(piece=piece):
                off = start + (length & ~(2 * piece - 1))
                cp = pltpu.make_async_copy(zeros.at[pl.ds(0, piece * n_s), :],
                                           xs_ref.at[pl.ds(off * n_s, piece * n_s), :], zsem)
                cp.start()
                cp.wait()
            piece //= 2

    for j in range(tiles_per_step):
        tile = i * tiles_per_step + j

        @pl.when((tile >= nu_ref[0]) & (tile < n_tiles))
        def _(tile=tile):
            cp = pltpu.make_async_copy(zeros, xs_ref.at[pl.ds(tile * (bm * n_s), bm * n_s), :], zsem)
            cp.start()
            cp.wait()

    for _ in range(TOP_K):
        pltpu.make_async_copy(h_ref, xs_ref.at[pl.ds(0, td * n_s), :], sem).wait()


def dispatch(dest, h2p, fill_start, fill_len, n_used, n_tok, n_tiles, td, bm):
    n_s = h2p.shape[0] // n_tok
    n_steps = n_tok // td
    n_e = fill_start.shape[0]
    experts_per_step = -(-n_e // n_steps)
    tiles_per_step = -(-n_tiles // n_steps)
    pad = n_steps * experts_per_step - n_e
    fill_start = jnp.pad(fill_start, (0, pad))
    fill_len = jnp.pad(fill_len, (0, pad))
    return pl.pallas_call(
        functools.partial(_dispatch_kernel, n_s=n_s, bm=bm, experts_per_step=experts_per_step,
                          tiles_per_step=tiles_per_step, n_tiles=n_tiles),
        grid_spec=pltpu.PrefetchScalarGridSpec(
            num_scalar_prefetch=3,
            grid=(n_steps,),
            in_specs=[pl.BlockSpec((TOP_K, td), lambda i, *_: (0, i), memory_space=pltpu.SMEM),
                      _row_tile_spec(td, n_s * LANES, lambda i, *_: i)],
            out_specs=pl.BlockSpec(memory_space=pl.ANY),
            scratch_shapes=[pltpu.VMEM((bm * n_s, LANES), h2p.dtype),
                            pltpu.SemaphoreType.DMA(()), pltpu.SemaphoreType.DMA(())]),
        out_shape=jax.ShapeDtypeStruct((n_tiles * bm * n_s, LANES), h2p.dtype),
        compiler_params=_cparams("arbitrary"),
        name="dispatch",
    )(fill_start, fill_len, n_used, dest, h2p)


def _expert_kernel(te_ref, nu_ref, x_ref, wg_ref, wu_ref, wd_ref, o_ref, wgu_bf, wd_bf, *, bm):
    i = pl.program_id(0)
    ff = wd_bf.shape[0]

    @pl.when(i < nu_ref[0])
    def _():
        @pl.when((i == 0) | (te_ref[i] != te_ref[jnp.maximum(i - 1, 0)]))
        def _():
            wgu_bf[:, :ff] = _bf(wg_ref[0])
            wgu_bf[:, ff:] = _bf(wu_ref[0])
            wd_bf[...] = _bf(wd_ref[0])

        hi, lo = _unpack_bf16_pair(_load_row_tiles(x_ref, bm))
        x = jnp.concatenate([hi, lo], axis=1)
        gu = _dot(x, wgu_bf[...])
        hid = _silu(gu[:, :ff]) * gu[:, ff:]
        _store_row_tiles(o_ref, _dot(_bf(hid), wd_bf[...]))

    @pl.when(i >= nu_ref[0])
    def _():
        o_ref[...] = jnp.zeros_like(o_ref)


def expert_ffn(tile_e, n_used, xs, n_rows, w_gate, w_up, w_down, bm):
    dm = w_gate.shape[1]
    half = dm // 2
    ff = w_gate.shape[-1]
    rows_in = _row_tile_spec(bm, half, lambda i, te, nu: jnp.minimum(i, nu[0] - 1))
    rows_out = _row_tile_spec(bm, dm, lambda i, te, nu: i)
    return pl.pallas_call(
        functools.partial(_expert_kernel, bm=bm),
        grid_spec=pltpu.PrefetchScalarGridSpec(
            num_scalar_prefetch=2,
            grid=(n_rows // bm,),
            in_specs=[rows_in,
                      pl.BlockSpec((1, dm, ff), lambda i, te, nu: (te[i], 0, 0)),
                      pl.BlockSpec((1, dm, ff), lambda i, te, nu: (te[i], 0, 0)),
                      pl.BlockSpec((1, ff, dm), lambda i, te, nu: (te[i], 0, 0))],
            out_specs=rows_out,
            scratch_shapes=[pltpu.VMEM((dm, 2 * ff), jnp.bfloat16),
                            pltpu.VMEM((ff, dm), jnp.bfloat16)]),
        out_shape=jax.ShapeDtypeStruct((n_rows * (dm // LANES), LANES), jnp.float32),
        compiler_params=_cparams("arbitrary"),
        name="expert",
    )(tile_e, n_used, xs, w_gate, w_up, w_down)


def _combine_kernel(dest_ref, dnext_ref, w_ref, x1_ref, h2_ref, mods_ref, g_ref, wsg_ref, wsu_ref,
                    wsd_ref, ys_ref, o_ref, gbuf, rbuf, wrep, sems, *, n_s):
    i = pl.program_id(0)
    n_steps = pl.num_programs(0)
    tc = x1_ref.shape[0]
    slot = i % 2

    def gather(d_ref, slot_, t):
        for k in range(TOP_K):
            pltpu.make_async_copy(_row(ys_ref, d_ref[k, t], n_s), _row(gbuf.at[slot_, k], t, n_s),
                                  sems.at[slot_]).start(priority=k % 2)

    def weighted(t):
        acc = None
        for k in range(TOP_K):
            wk = jnp.broadcast_to(wrep[k, pl.ds(t, 1), :], (n_s, LANES))
            term = wk * _row(gbuf.at[slot, k], t, n_s)[...]
            acc = term if acc is None else acc + term
        _row(rbuf, t, n_s)[...] = acc

    @pl.when(i == 0)
    def _():
        def body(t, carry):
            gather(dest_ref, 0, t)
            return carry
        lax.fori_loop(0, tc, body, 0)

    w = w_ref[...]
    for k in range(TOP_K):
        wrep[k] = jnp.broadcast_to(w[:, k:k + 1], (tc, LANES))

    for k in range(TOP_K):
        pltpu.make_async_copy(ys_ref.at[pl.ds(0, tc * n_s), :], gbuf.at[slot, k], sems.at[slot]).wait()

    @pl.when(i + 1 < n_steps)
    def _():
        def body(t, carry):
            gather(dnext_ref, 1 - slot, t)
            weighted(t)
            return carry
        lax.fori_loop(0, tc, body, 0)

    @pl.when(i + 1 == n_steps)
    def _():
        def body(t, carry):
            weighted(t)
            return carry
        lax.fori_loop(0, tc, body, 0)

    hi, lo = _unpack_bf16_pair(_load_row_tiles(h2_ref, tc))
    h = jnp.concatenate([hi, lo], axis=1)
    hid = _silu(_dot(h, wsg_ref[...])) * _dot(h, wsu_ref[...])
    y = _dot(_bf(hid), wsd_ref[...]) + _load_row_tiles(rbuf, tc)
    m = mods_ref[0]
    o_ref[...] = x1_ref[...] + m[5:6, :] * _rms(y, g_ref[...])


def combine(dest, w_tk, x1, h2p, mods3, g_post, ws_gate_bf, ws_up_bf, ws_down_bf, ys, seq, tc):
    n_tok, dm = x1.shape
    n_s = dm // LANES
    n_steps = n_tok // tc
    tpb = seq // tc
    full = lambda a: pl.BlockSpec(a.shape, lambda i: (0,) * a.ndim)
    tok = lambda w: pl.BlockSpec((tc, w), lambda i: (i, 0))
    g2 = g_post.reshape(1, dm)
    return pl.pallas_call(
        functools.partial(_combine_kernel, n_s=n_s),
        grid=(n_steps,),
        in_specs=[pl.BlockSpec((TOP_K, tc), lambda i: (0, i), memory_space=pltpu.SMEM),
                  pl.BlockSpec((TOP_K, tc), lambda i: (0, jnp.minimum(i + 1, n_steps - 1)),
                               memory_space=pltpu.SMEM),
                  tok(TOP_K),
                  tok(dm), _row_tile_spec(tc, dm // 2, lambda i: i),
                  pl.BlockSpec((1, N_MODS, dm), lambda i: (i // tpb, 0, 0)),
                  full(g2), full(ws_gate_bf), full(ws_up_bf), full(ws_down_bf),
                  pl.BlockSpec(memory_space=pl.ANY)],
        out_specs=tok(dm),
        out_shape=jax.ShapeDtypeStruct((n_tok, dm), jnp.float32),
        scratch_shapes=[pltpu.VMEM((2, TOP_K, tc * n_s, LANES), jnp.float32),
                        pltpu.VMEM((tc * n_s, LANES), jnp.float32),
                        pltpu.VMEM((TOP_K, tc, LANES), jnp.float32),
                        pltpu.SemaphoreType.DMA((2,))],
        compiler_params=_cparams("arbitrary"),
        name="combine",
    )(dest, dest, w_tk, x1, h2p, mods3, g2, ws_gate_bf, ws_up_bf, ws_down_bf, ys)


def moe_ffn(h2, x1, logits_t, mods3, router_bias, w_gate, w_up, w_down, ws_gate, ws_up, ws_down,
            g_post, seq):
    n_tok, dm = x1.shape
    n_e = w_gate.shape[0]
    bm = EXPERT_ROW_TILE
    e_idx, w_kt, rank, cnt = route(logits_t, router_bias, min(512, n_tok))
    counts = cnt[:, 0].astype(jnp.int32)
    padded = (counts + bm - 1) // bm * bm
    pad_end = jnp.cumsum(padded)
    pad_start = (pad_end - padded).astype(jnp.int32)
    n_tiles = -(-(n_tok * TOP_K) // bm) + n_e
    n_used = (pad_end[-1] // bm).astype(jnp.int32)
    tile_e = jnp.minimum(jnp.searchsorted(pad_end, jnp.arange(n_tiles, dtype=jnp.int32) * bm, side="right"),
                         n_e - 1).astype(jnp.int32)
    tile_e = jnp.where(jnp.arange(n_tiles) < n_used, tile_e, tile_e[n_used - 1])
    dest = dest_rows(e_idx, rank, pad_start, min(512, n_tok))
    n_used1 = n_used.reshape(1)
    xs = dispatch(dest, h2, pad_start + counts, padded - counts, n_used1, n_tok, n_tiles,
                  min(256, n_tok), bm)
    ys = expert_ffn(tile_e, n_used1, xs, n_tiles * bm, w_gate, w_up, w_down, bm)
    bf = lambda a: a.astype(jnp.bfloat16)
    return combine(dest, w_kt.T, x1, h2, mods3, g_post, bf(ws_gate), bf(ws_up), bf(ws_down), ys,
                   seq, min(256, n_tok))


def kernel(x, c, w_ada, b_ada, g_pre_mix, g_post_mix, g_pre_ffn, g_post_ffn, w_in, mu_shift, rwkv_w0, rwkv_w2, rwkv_a0, rwkv_a2, rwkv_g2, rwkv_k_k, rwkv_k_a, rwkv_r_k, rwkv_ln_w, rwkv_ln_b, s5_log_dt, s5_a_re, s5_a_im, s5_b_re, s5_b_im, s5_c_re, s5_c_im, s5_d, s5_w_glu, s5_b_glu, w_out, w_router, router_bias, w_gate, w_up, w_down, ws_gate, ws_up, ws_down):
    bsz, seq, dm = x.shape
    depth = w_ada.shape[0]
    bf = lambda a: a.astype(jnp.bfloat16)
    tm = min(512, seq)
    for l in range(depth):
        mods3 = ada_mods(c, w_ada[l], b_ada[l]).reshape(bsz, N_MODS, dm)
        n_rwkv = mu_shift.shape[-1]
        p, u = in_proj(x, mods3, g_pre_mix[l], bf(w_in[l]), n_rwkv, tm)
        y_rwkv = rwkv_mixer(p, mu_shift[l], rwkv_w0[l], rwkv_w2[l], rwkv_a0[l], rwkv_a2[l],
                            rwkv_g2[l], rwkv_k_k[l], rwkv_k_a[l], rwkv_r_k[l].reshape(-1),
                            rwkv_ln_w[l], rwkv_ln_b[l])
        y_s5 = s5_core(u, s5_log_dt[l], s5_a_re[l], s5_a_im[l], s5_b_re[l], s5_b_im[l],
                       s5_c_re[l], s5_c_im[l])
        flat = lambda a: a.reshape(bsz * seq, a.shape[-1])
        x1, h2, logits_t = out_proj(flat(y_rwkv), flat(y_s5), flat(u), flat(x), mods3, s5_d[l],
                                    bf(s5_w_glu[l]), s5_b_glu[l], bf(w_out[l]), g_post_mix[l],
                                    g_pre_ffn[l], w_router[l].T, seq, tm)
        out = moe_ffn(h2, x1, logits_t, mods3, router_bias[l], w_gate[l], w_up[l], w_down[l],
                      ws_gate[l], ws_up[l], ws_down[l], g_post_ffn[l], seq)
        x = out.reshape(bsz, seq, dm)
    return x
```

```python
import functools
import math

import jax
import jax.numpy as jnp
from jax import lax
from jax.experimental import pallas as pl
from jax.experimental.pallas import tpu as pltpu

NORM_EPS = 1e-6
LNX_EPS = 64e-5
L2_EPS = 1e-12
S5_MAX_REAL = -1e-4
ROUTE_SCALE = 2.5
N_MODS = 6

RWKV_HEAD_DIM = 64
DECAY_LORA = 64
ICLR_LORA = 64
GATE_LORA = 128
S5_GROUP = 16
S5_STATE = 64
TOP_K = 8
ROUTE_GROUPS = 8
ROUTE_TOPK_GROUPS = 4

RWKV_CHUNK = 64
S5_CHUNK = 32
EXPERT_ROW_TILE = 256

VMEM_LIMIT = 56 * 1024 * 1024

HI = lax.Precision.HIGHEST


def _cparams(*sem):
    return pltpu.CompilerParams(dimension_semantics=sem, vmem_limit_bytes=VMEM_LIMIT)


def _dot(a, b, precision=None):
    return jnp.dot(a, b, preferred_element_type=jnp.float32, precision=precision)


def _dot_nt(a, b, precision=None):
    return lax.dot_general(a, b, (((1,), (1,)), ((), ())),
                           preferred_element_type=jnp.float32, precision=precision)


def _dot_tn(a, b, precision=None):
    return lax.dot_general(a, b, (((0,), (0,)), ((), ())),
                           preferred_element_type=jnp.float32, precision=precision)


def _bdot(a, b):
    return lax.dot_general(a, b, (((2,), (1,)), ((0,), (0,))), preferred_element_type=jnp.float32)


def _bdot_nt(a, b):
    return lax.dot_general(a, b, (((2,), (2,)), ((0,), (0,))), preferred_element_type=jnp.float32)


def _bf(x):
    return x.astype(jnp.bfloat16)


def _pack_bf16_pair(hi, lo):
    hb = lax.bitcast_convert_type(_bf(hi).astype(jnp.float32), jnp.uint32)
    lb = lax.bitcast_convert_type(_bf(lo).astype(jnp.float32), jnp.uint32)
    return (hb & jnp.uint32(0xFFFF0000)) | (lb >> 16)


def _unpack_bf16_pair(w):
    hi = lax.bitcast_convert_type(w & jnp.uint32(0xFFFF0000), jnp.float32)
    lo = lax.bitcast_convert_type(w << 16, jnp.float32)
    return _bf(hi), _bf(lo)


LANES = 128


def _row_tile_spec(rows, width, row_block):
    return pl.BlockSpec((rows * (width // LANES), LANES), lambda *a: (row_block(*a), 0))


def _store_row_tiles(ref, x):
    rows = x.shape[0]
    n_s = ref.shape[0] // rows
    for s in range(n_s):
        ref[pl.ds(s, rows, stride=n_s), :] = x[:, s * LANES:(s + 1) * LANES]


def _load_row_tiles(ref, rows):
    n_s = ref.shape[0] // rows
    return jnp.concatenate([ref[pl.ds(s, rows, stride=n_s), :] for s in range(n_s)], axis=1)


def _row(ref, r, n_s):
    return ref.at[pl.ds(pl.multiple_of(r * n_s, n_s), n_s), :]


def _sigmoid(x):
    return 1.0 / (1.0 + jnp.exp(-x))


def _silu(x):
    return x * _sigmoid(x)


def _rms(x, gain):
    return x * lax.rsqrt(jnp.mean(x * x, axis=-1, keepdims=True) + NORM_EPS) * gain


def _ada_kernel(c_ref, w_ref, b_ref, o_ref):
    c = c_ref[...]
    o_ref[...] = _dot(_silu(c), w_ref[...], HI) + b_ref[...]


def ada_mods(c, w_ada, b_ada):
    bsz, dm = c.shape
    n = w_ada.shape[1]
    tn = dm
    return pl.pallas_call(
        _ada_kernel,
        grid=(n // tn,),
        in_specs=[pl.BlockSpec((bsz, dm), lambda j: (0, 0)),
                  pl.BlockSpec((dm, tn), lambda j: (0, j)),
                  pl.BlockSpec((1, tn), lambda j: (0, j))],
        out_specs=pl.BlockSpec((bsz, tn), lambda j: (0, j)),
        out_shape=jax.ShapeDtypeStruct((bsz, n), jnp.float32),
        compiler_params=_cparams("arbitrary"),
        name="ada",
    )(c, w_ada, b_ada.reshape(1, n))


def _inproj_kernel(x_ref, mods_ref, g_ref, w_ref, p_ref, u_ref):
    x = x_ref[0]
    m = mods_ref[0]
    h = _rms(x, g_ref[...]) * (1.0 + m[1:2, :]) + m[0:1, :]
    proj = _dot(_bf(h), w_ref[...])
    n_p = p_ref.shape[-1]
    p_ref[0] = proj[:, :n_p]
    u_ref[0] = proj[:, n_p:]


def in_proj(x, mods3, g_pre, w_in_bf, n_rwkv, tm):
    bsz, seq, dm = x.shape
    n = w_in_bf.shape[1]
    n_s5 = n - n_rwkv
    return pl.pallas_call(
        _inproj_kernel,
        grid=(bsz, seq // tm),
        in_specs=[pl.BlockSpec((1, tm, dm), lambda b, i: (b, i, 0)),
                  pl.BlockSpec((1, N_MODS, dm), lambda b, i: (b, 0, 0)),
                  pl.BlockSpec((1, dm), lambda b, i: (0, 0)),
                  pl.BlockSpec((dm, n), lambda b, i: (0, 0))],
        out_specs=[pl.BlockSpec((1, tm, n_rwkv), lambda b, i: (b, i, 0)),
                   pl.BlockSpec((1, tm, n_s5), lambda b, i: (b, i, 0))],
        out_shape=[jax.ShapeDtypeStruct((bsz, seq, n_rwkv), jnp.float32),
                   jax.ShapeDtypeStruct((bsz, seq, n_s5), jnp.float32)],
        compiler_params=_cparams("arbitrary", "arbitrary"),
        name="inproj",
    )(x, mods3, g_pre.reshape(1, dm), w_in_bf)


RWKV_TILE = 256


def _split_dot(x, m01, terms):
    acc = None
    rem = x
    for _ in range(terms):
        piece = _bf(rem)
        part = _dot(piece, m01)
        acc = part if acc is None else acc + part
        rem = rem - piece.astype(jnp.float32)
    return acc


def _rwkv_kernel(p_ref, mu_ref, w0_ref, w2_ref, a0_ref, a2_ref, g2_ref, kk_ref, ka_ref,
                  rk_ref, lnw_ref, lnb_ref, bd_ref, o_ref,
                  s_ref, carry_ref, rt_ref, at_ref, bt_ref, kt_ref, v_ref, wl_ref, y_ref,
                  *, width, chunk):
    hd = RWKV_HEAD_DIM
    pw = 2 * hd
    pairs = width // pw
    i = pl.program_id(1)

    @pl.when(i == 0)
    def _():
        s_ref[...] = jnp.zeros_like(s_ref)
        carry_ref[...] = jnp.zeros_like(carry_ref)

    p = p_ref[0]
    n_t = p.shape[0]
    n_chunks = n_t // chunk
    row = lax.broadcasted_iota(jnp.int32, p.shape, 0)
    prev = jnp.where(row == 0, carry_ref[0:1, :], pltpu.roll(p, 1, axis=0))
    carry_ref[0:1, :] = p[n_t - 1:n_t, :]
    pm = p + (prev - p) * mu_ref[...]

    r = pm[:, 0:width]
    k = pm[:, width:2 * width]
    v = pm[:, 2 * width:3 * width]
    c0 = 3 * width
    w_lo = pm[:, c0:c0 + DECAY_LORA]
    a_lo = pm[:, c0 + DECAY_LORA:c0 + DECAY_LORA + ICLR_LORA]
    g_lo = pm[:, c0 + DECAY_LORA + ICLR_LORA:]

    z = w0_ref[...] + _dot(_bf(jnp.tanh(w_lo)), w2_ref[...])
    softplus_neg = jnp.maximum(-z, 0.0) + jnp.log(1.0 + jnp.exp(-jnp.abs(z)))
    logd = -jnp.exp(-softplus_neg - 0.5)
    iclr = _sigmoid(a0_ref[...] + _dot(_bf(a_lo), a2_ref[...]))
    gate = _dot(_bf(_sigmoid(g_lo)), g2_ref[...])

    bd = bd_ref[...]
    kk = k * kk_ref[...]
    kk = kk / jnp.maximum(jnp.sqrt(_split_dot(kk * kk, bd, 2)), L2_EPS)
    k2 = k * (1.0 + (iclr - 1.0) * ka_ref[...])
    bonus = _split_dot(r * k2 * rk_ref[...], bd, 2) * v

    ti = lax.broadcasted_iota(jnp.int32, (n_t, n_t), 0)
    si = lax.broadcasted_iota(jnp.int32, (n_t, n_t), 1)
    tri = jnp.where((ti >= si) & (ti // chunk == si // chunk), 1.0, 0.0).astype(jnp.bfloat16)
    cum = _split_dot_lhs(tri, logd, 3)
    e_pos = jnp.exp(cum)
    e_neg = jnp.exp(-cum)
    rt_ref[...] = _bf(r * e_pos)
    at_ref[...] = _bf(-kk * jnp.exp(cum - logd))
    bt_ref[...] = _bf(kk * iclr * e_neg)
    kt_ref[...] = _bf(k2 * e_neg)
    v_ref[...] = _bf(v)
    for c in range(n_chunks):
        wl_ref[c:c + 1, :] = e_pos[(c + 1) * chunk - 1:(c + 1) * chunk, :]

    two_l = 2 * chunk
    lane = lax.broadcasted_iota(jnp.int32, (chunk, pw), 1)
    lane0 = lane < hd
    bi = lax.broadcasted_iota(jnp.int32, (two_l, two_l), 0)
    bj = lax.broadcasted_iota(jnp.int32, (two_l, two_l), 1)
    same = (bi // chunk) == (bj // chunk)
    low_strict = same & (bi > bj)
    low_incl = same & (bi >= bj)
    eye_t = jnp.where(bi == bj, 1.0, 0.0)
    pi_ = lax.broadcasted_iota(jnp.int32, (pw, pw), 0)
    pj_ = lax.broadcasted_iota(jnp.int32, (pw, pw), 1)
    eye_p = jnp.where(pi_ == pj_, 1.0, 0.0)
    n_sq = max(1, int(math.ceil(math.log2(chunk))) - 1)
    zero = jnp.zeros((), jnp.bfloat16)

    def stack(x):
        return jnp.concatenate([jnp.where(lane0, x, zero), jnp.where(lane0, zero, x)], axis=0)

    def gather(ref):
        return jnp.stack([stack(ref[c * chunk:(c + 1) * chunk, hp * pw:(hp + 1) * pw])
                          for c in range(n_chunks) for hp in range(pairs)], axis=0)

    a_s, b_s, k_s, r_s, v_s = (gather(ref) for ref in (at_ref, bt_ref, kt_ref, rt_ref, v_ref))
    wl = jnp.stack([wl_ref[c:c + 1, hp * pw:(hp + 1) * pw]
                    for c in range(n_chunks) for hp in range(pairs)], axis=0)
    gram = _bdot_nt(jnp.concatenate([a_s, r_s], axis=1), jnp.concatenate([b_s, k_s], axis=1))
    m_ab = jnp.where(low_strict, gram[:, :two_l, :two_l], 0.0)
    m_ak = jnp.where(low_strict, gram[:, :two_l, two_l:], 0.0)
    n_rb = jnp.where(low_incl, gram[:, two_l:, :two_l], 0.0)
    n_rk = jnp.where(low_incl, gram[:, two_l:, two_l:], 0.0)
    t_inv = eye_t + m_ab
    m_pow = _bf(m_ab)
    for _ in range(n_sq):
        m_pow = _bf(_bdot(m_pow, m_pow))
        t_inv = t_inv + _bdot(_bf(t_inv), m_pow)
    makv = _bdot(_bf(m_ak), v_s)
    tx_bf = _bf(_bdot(_bf(t_inv), jnp.concatenate([a_s, _bf(makv)], axis=2)))
    nx = _bdot(_bf(n_rb), tx_bf)
    rbar = _bf(r_s.astype(jnp.float32) + nx[:, :, :pw])
    y0 = nx[:, :, pw:] + _bdot(_bf(n_rk), v_s)
    tb = _bdot(jnp.swapaxes(tx_bf, 1, 2), b_s)
    pmat = _bf((eye_p + tb[:, :pw, :]) * wl)
    dmat = (tb[:, pw:, :] + _bdot(jnp.swapaxes(v_s, 1, 2), k_s)) * wl

    s = s_ref[...]
    for c in range(n_chunks):
        sel = slice(c * pairs, (c + 1) * pairs)
        s_bf = _bf(s)
        ys = _bdot_nt(rbar[sel], s_bf) + y0[sel]
        yc_ = ys[:, :chunk, :] + ys[:, chunk:, :]
        for hp in range(pairs):
            y_ref[c * chunk:(c + 1) * chunk, hp * pw:(hp + 1) * pw] = yc_[hp]
        s = _bdot(s_bf, pmat[sel]) + dmat[sel]
    s_ref[...] = s

    y = y_ref[...]
    inv_hd = 1.0 / hd
    mean = _split_dot(y, bd, 2) * inv_hd
    yc = y - mean
    var = _split_dot(yc * yc, bd, 2) * inv_hd
    yn = yc * lax.rsqrt(var + LNX_EPS) * lnw_ref[...] + lnb_ref[...]
    o_ref[0] = (yn + bonus) * gate


def _split_dot_lhs(m01, x, terms):
    acc = None
    rem = x
    for _ in range(terms):
        piece = _bf(rem)
        part = _dot(m01, piece)
        acc = part if acc is None else acc + part
        rem = rem - piece.astype(jnp.float32)
    return acc


def rwkv_mixer(p, mu, w0, w2, a0, a2, g2, k_k, k_a, r_k, ln_w, ln_b):
    bsz, seq, n_p = p.shape
    width = w0.shape[-1]
    pairs = width // (2 * RWKV_HEAD_DIM)
    chunk = min(RWKV_CHUNK, seq)
    tile = min(RWKV_TILE, seq)
    hid = jnp.arange(width, dtype=jnp.int32) // RWKV_HEAD_DIM
    bd = (hid[:, None] == hid[None, :]).astype(jnp.bfloat16)
    row = lambda t: t.reshape(1, -1)
    full = lambda a: pl.BlockSpec(a.shape, lambda b, i: (0,) * a.ndim)
    consts = [row(mu), row(w0), _bf(w2), row(a0), _bf(a2), _bf(g2), row(k_k), row(k_a), row(r_k),
              row(ln_w), row(ln_b), bd]
    act = lambda: pltpu.VMEM((tile, width), jnp.bfloat16)
    return pl.pallas_call(
        functools.partial(_rwkv_kernel, width=width, chunk=chunk),
        grid=(bsz, seq // tile),
        in_specs=[pl.BlockSpec((1, tile, n_p), lambda b, i: (b, i, 0))] + [full(a) for a in consts],
        out_specs=pl.BlockSpec((1, tile, width), lambda b, i: (b, i, 0)),
        out_shape=jax.ShapeDtypeStruct((bsz, seq, width), jnp.float32),
        scratch_shapes=[pltpu.VMEM((pairs, 2 * RWKV_HEAD_DIM, 2 * RWKV_HEAD_DIM), jnp.float32),
                        pltpu.VMEM((8, n_p), jnp.float32),
                        act(), act(), act(), act(), act(),
                        pltpu.VMEM((max(8, tile // chunk), width), jnp.float32),
                        pltpu.VMEM((tile, width), jnp.float32)],
        compiler_params=_cparams("arbitrary", "arbitrary"),
        name="rwkv",
    )(p, *consts)


def _s5_discretise(a_re, a_im, dt):
    lam_re = jnp.minimum(a_re, S5_MAX_REAL)
    lam_im = a_im
    log_mag = lam_re * dt
    ang = lam_im * dt
    mag = jnp.exp(log_mag)
    ab_re, ab_im = mag * jnp.cos(ang), mag * jnp.sin(ang)
    den = lam_re * lam_re + lam_im * lam_im
    n_re, n_im = ab_re - 1.0, ab_im
    q_re = (n_re * lam_re + n_im * lam_im) / den
    q_im = (n_im * lam_re - n_re * lam_im) / den
    return log_mag, ang, q_re, q_im


def _s5_power(log_mag, ang, t):
    mag = jnp.exp(t * log_mag)
    return mag * jnp.cos(t * ang), mag * jnp.sin(t * ang)


def _s5ops_kernel(ldt_ref, ar_row, ai_row, ar_col, ai_col, bt_re, bt_im, ct_re, ct_im, til_ref,
                  toep_ref, pm_ref, q_ref, lvl_ref, *, lc):
    n_c, n_p = bt_re.shape[1], bt_re.shape[2]
    lw = lc * n_c
    dt = jnp.exp(ldt_ref[0])

    lm_r, an_r, q_re, q_im = _s5_discretise(ar_row[0], ai_row[0], dt)
    bbt_re = q_re * bt_re[0] - q_im * bt_im[0]
    bbt_im = q_re * bt_im[0] + q_im * bt_re[0]
    s_row = lax.broadcasted_iota(jnp.int32, (lw, n_p), 0) // n_c
    pw_re, pw_im = _s5_power(lm_r, an_r, (lc - 1 - s_row).astype(jnp.float32))
    tb_re = jnp.concatenate([bbt_re] * lc, axis=0)
    tb_im = jnp.concatenate([bbt_im] * lc, axis=0)
    pm_ref[0] = _bf(jnp.concatenate([pw_re * tb_re - pw_im * tb_im,
                                     pw_re * tb_im + pw_im * tb_re], axis=1))

    lm_c, an_c, _, _ = _s5_discretise(ar_col[0], ai_col[0], dt)
    til = til_ref[...]
    c_re = _dot(ct_re[0], til, HI)
    c_im = _dot(ct_im[0], til, HI)
    t_lane = (lax.broadcasted_iota(jnp.int32, (n_p, lw), 1) // n_c).astype(jnp.float32)
    p0_re, p0_im = _s5_power(lm_c, an_c, t_lane)
    p1_re, p1_im = _s5_power(lm_c, an_c, t_lane + 1.0)
    ca0_re, ca0_im = p0_re * c_re - p0_im * c_im, p0_re * c_im + p0_im * c_re
    q_ref[0] = _bf(jnp.concatenate([p1_re * c_re - p1_im * c_im,
                                    -(p1_re * c_im + p1_im * c_re)], axis=0))
    r0 = _dot(bbt_re, ca0_re, HI) - _dot(bbt_im, ca0_im, HI)
    lane = lax.broadcasted_iota(jnp.int32, (n_c, lw), 1)
    for s in range(lc):
        blk = r0 if s == 0 else jnp.where(lane >= s * n_c, pltpu.roll(r0, s * n_c, axis=1), 0.0)
        toep_ref[0, s * n_c:(s + 1) * n_c, :] = _bf(blk)

    cr, ci = _s5_power(lm_r, an_r, float(lc))
    for j in range(lvl_ref.shape[1]):
        lvl_ref[0, j] = jnp.concatenate([jnp.concatenate([cr, cr], axis=1),
                                         jnp.concatenate([-ci, ci], axis=1)], axis=0)
        cr, ci = cr * cr - ci * ci, 2.0 * cr * ci


def s5_operators(log_dt, a_re, a_im, b_re, b_im, c_re, c_im, n_chunks, lc):
    n_g, n_p = a_re.shape
    n_c = b_re.shape[-1]
    lw = lc * n_c
    n_lvl = max(1, int(math.ceil(math.log2(n_chunks))))
    til = (jnp.arange(lw)[None, :] % n_c == jnp.arange(n_c)[:, None]).astype(jnp.float32)
    t3 = lambda a: jnp.swapaxes(a, 1, 2)
    args = [log_dt.reshape(n_g, 1, 1), a_re.reshape(n_g, 1, n_p), a_im.reshape(n_g, 1, n_p),
            a_re.reshape(n_g, n_p, 1), a_im.reshape(n_g, n_p, 1),
            t3(b_re), t3(b_im), t3(c_re), t3(c_im)]
    per_g = lambda a: pl.BlockSpec((1,) + a.shape[1:], lambda g: (g,) + (0,) * (a.ndim - 1))
    return pl.pallas_call(
        functools.partial(_s5ops_kernel, lc=lc),
        grid=(n_g,),
        in_specs=[per_g(a) for a in args] + [pl.BlockSpec(til.shape, lambda g: (0, 0))],
        out_specs=[pl.BlockSpec((1, lw, lw), lambda g: (g, 0, 0)),
                   pl.BlockSpec((1, lw, 2 * n_p), lambda g: (g, 0, 0)),
                   pl.BlockSpec((1, 2 * n_p, lw), lambda g: (g, 0, 0)),
                   pl.BlockSpec((1, n_lvl, 2, 2 * n_p), lambda g: (g, 0, 0, 0))],
        out_shape=[jax.ShapeDtypeStruct((n_g, lw, lw), jnp.bfloat16),
                   jax.ShapeDtypeStruct((n_g, lw, 2 * n_p), jnp.bfloat16),
                   jax.ShapeDtypeStruct((n_g, 2 * n_p, lw), jnp.bfloat16),
                   jax.ShapeDtypeStruct((n_g, n_lvl, 2, 2 * n_p), jnp.float32)],
        compiler_params=_cparams("arbitrary"),
        name="s5ops",
    )(*args, til)


def _s5_kernel(u_ref, toep_ref, pm_ref, q_ref, lvl_ref, y_ref, *, n_chunks):
    u = _bf(u_ref[0])
    e = _dot(u, pm_ref[0])
    rows, two_p = e.shape
    half = two_p // 2
    cidx = lax.broadcasted_iota(jnp.int32, e.shape, 0) % n_chunks
    x = e
    n_lvl = lvl_ref.shape[1]
    for j in range(n_lvl):
        sh = 1 << j
        if sh >= n_chunks:
            break
        xs = jnp.where(cidx >= sh, pltpu.roll(x, sh, axis=0), 0.0)
        cf = lvl_ref[0, j]
        x = x + xs * cf[0:1, :] + pltpu.roll(xs, half, axis=1) * cf[1:2, :]
    x_in = jnp.where(cidx >= 1, pltpu.roll(x, 1, axis=0), 0.0)
    x_hi = _bf(x_in)
    x_lo = _bf(x_in - x_hi.astype(jnp.float32))
    q = q_ref[0]
    y_ref[0] = _dot(u, toep_ref[0]) + _dot(x_hi, q) + _dot(x_lo, q)


def s5_core(u, log_dt, a_re, a_im, b_re, b_im, c_re, c_im):
    bsz, seq, width = u.shape
    n_g, n_p = a_re.shape
    n_c = width // n_g
    lc = min(S5_CHUNK, seq)
    n_chunks = seq // lc
    toep, pm, q, lvl = s5_operators(log_dt, a_re, a_im, b_re, b_im, c_re, c_im, n_chunks, lc)
    ug = u.reshape(bsz, n_chunks, lc, n_g, n_c)
    ug = jnp.transpose(ug, (3, 0, 1, 2, 4)).reshape(n_g, bsz * n_chunks, lc * n_c)
    rows, lw = bsz * n_chunks, lc * n_c
    yg = pl.pallas_call(
        functools.partial(_s5_kernel, n_chunks=n_chunks),
        grid=(n_g,),
        in_specs=[pl.BlockSpec((1, rows, lw), lambda g: (g, 0, 0)),
                  pl.BlockSpec((1, lw, lw), lambda g: (g, 0, 0)),
                  pl.BlockSpec((1, lw, 2 * n_p), lambda g: (g, 0, 0)),
                  pl.BlockSpec((1, 2 * n_p, lw), lambda g: (g, 0, 0)),
                  pl.BlockSpec((1,) + lvl.shape[1:], lambda g: (g, 0, 0, 0))],
        out_specs=pl.BlockSpec((1, rows, lw), lambda g: (g, 0, 0)),
        out_shape=jax.ShapeDtypeStruct((n_g, rows, lw), jnp.float32),
        compiler_params=_cparams("arbitrary"),
        name="s5",
    )(ug, toep, pm, q, lvl)
    yg = yg.reshape(n_g, bsz, n_chunks, lc, n_c)
    return jnp.transpose(yg, (1, 2, 3, 0, 4)).reshape(bsz, seq, width)


def _gelu_tanh(y):
    return 0.5 * y * (1.0 + jnp.tanh(math.sqrt(2.0 / math.pi) * (y + 0.044715 * (y * y * y))))


def _outproj_kernel(yr_ref, ys_ref, u_ref, x_ref, mods_ref, d_ref, wglu_ref, bglu_ref, wout_ref,
                    gpost_ref, gpre_ref, wrt_ref, x1_ref, h2_ref, lg_ref):
    m = mods_ref[0]
    yr = yr_ref[...]
    y5 = _gelu_tanh(ys_ref[...] + d_ref[...] * u_ref[...])
    y5 = y5 * _sigmoid(_dot(_bf(y5), wglu_ref[...]) + bglu_ref[...])
    wr = yr.shape[-1]
    mixed = _dot(_bf(yr), wout_ref[0:wr, :]) + _dot(_bf(y5), wout_ref[wr:, :])
    x1 = x_ref[...] + m[2:3, :] * _rms(mixed, gpost_ref[...])
    x1_ref[...] = x1
    h2 = _rms(x1, gpre_ref[...]) * (1.0 + m[4:5, :]) + m[3:4, :]
    half = h2.shape[-1] // 2
    _store_row_tiles(h2_ref, _pack_bf16_pair(h2[:, :half], h2[:, half:]))
    w_hi, w_lo = wrt_ref[0], wrt_ref[1]
    h_hi = _bf(h2)
    h_lo = _bf(h2 - h_hi.astype(jnp.float32))
    lg_ref[...] = _dot_nt(w_hi, h_hi) + (_dot_nt(w_hi, h_lo) + _dot_nt(w_lo, h_hi))


def out_proj(y_rwkv, y_s5, u, x, mods3, s5_d, w_glu_bf, b_glu, w_out_bf, g_post, g_pre, w_router_t,
             seq, tm):
    n_tok, dm = x.shape
    wr, ws = y_rwkv.shape[-1], y_s5.shape[-1]
    n_e = w_router_t.shape[0]
    tpb = seq // tm
    tok = lambda w: pl.BlockSpec((tm, w), lambda i: (i, 0))
    full = lambda a: pl.BlockSpec(a.shape, lambda i: (0,) * a.ndim)
    row = lambda t: t.reshape(1, -1)
    wr_hi = _bf(w_router_t)
    wr_split = jnp.stack([wr_hi, _bf(w_router_t - wr_hi.astype(jnp.float32))])
    consts = [row(s5_d), w_glu_bf, row(b_glu), w_out_bf, row(g_post), row(g_pre), wr_split]
    return pl.pallas_call(
        _outproj_kernel,
        grid=(n_tok // tm,),
        in_specs=[tok(wr), tok(ws), tok(ws), tok(dm),
                  pl.BlockSpec((1, N_MODS, dm), lambda i: (i // tpb, 0, 0))] + [full(a) for a in consts],
        out_specs=[tok(dm), _row_tile_spec(tm, dm // 2, lambda i: i),
                   pl.BlockSpec((n_e, tm), lambda i: (0, i))],
        out_shape=[jax.ShapeDtypeStruct((n_tok, dm), jnp.float32),
                   jax.ShapeDtypeStruct((n_tok * (dm // 2 // LANES), LANES), jnp.uint32),
                   jax.ShapeDtypeStruct((n_e, n_tok), jnp.float32)],
        compiler_params=_cparams("arbitrary"),
        name="outproj",
    )(y_rwkv, y_s5, u, x, mods3, *consts)


def _route_kernel(lg_ref, bias_ref, tri_ref, e_ref, w_ref, r_ref, cnt_ref, carry_ref):
    i = pl.program_id(0)

    @pl.when(i == 0)
    def _():
        carry_ref[...] = jnp.zeros_like(carry_ref)

    neg = -jnp.inf
    scores = _sigmoid(lg_ref[...])
    n_e, tm = scores.shape
    choice = scores + bias_ref[...]
    gsz = n_e // ROUTE_GROUPS
    c3 = choice.reshape(ROUTE_GROUPS, gsz, tm)
    io = lax.broadcasted_iota(jnp.int32, c3.shape, 1)
    m1 = jnp.max(c3, axis=1, keepdims=True)
    first = jnp.min(jnp.where(c3 == m1, io, gsz), axis=1, keepdims=True)
    m2 = jnp.max(jnp.where(io == first, neg, c3), axis=1, keepdims=True)
    gs = m1 + m2
    gi = lax.broadcasted_iota(jnp.int32, gs.shape, 0)
    rank = jnp.zeros(gs.shape, jnp.int32)
    for j in range(ROUTE_GROUPS):
        gj = gs[j:j + 1]
        beats = (gj > gs) | ((gj == gs) & (gi > j))
        rank = rank + beats.astype(jnp.int32)
    masked = jnp.where(rank < ROUTE_TOPK_GROUPS, c3, neg).reshape(n_e, tm)

    eio = lax.broadcasted_iota(jnp.int32, (n_e, tm), 0)
    ids, ws = [], []
    mhot = jnp.zeros((n_e, tm), jnp.float32)
    for _ in range(TOP_K):
        m = jnp.max(masked, axis=0, keepdims=True)
        idx = jnp.min(jnp.where(masked == m, eio, n_e), axis=0, keepdims=True)
        sel = eio == idx
        ws.append(jnp.sum(jnp.where(sel, scores, 0.0), axis=0, keepdims=True))
        ids.append(idx)
        masked = jnp.where(sel, neg, masked)
        mhot = jnp.where(sel, 1.0, mhot)
    wsum = ws[0]
    for t in ws[1:]:
        wsum = wsum + t
    before = _dot(_bf(mhot), tri_ref[...]) + carry_ref[...]
    ranks = [jnp.sum(jnp.where(eio == idx, before, 0.0), axis=0, keepdims=True) for idx in ids]
    e_ref[...] = jnp.concatenate(ids, axis=0)
    w_ref[...] = jnp.concatenate(ws, axis=0) / wsum * ROUTE_SCALE
    r_ref[...] = jnp.concatenate(ranks, axis=0).astype(jnp.int32)
    carry_ref[...] = carry_ref[...] + jnp.sum(mhot, axis=1, keepdims=True)
    cnt_ref[...] = carry_ref[...]


def route(logits_t, router_bias, tm):
    n_e, n_tok = logits_t.shape
    tri = (jnp.arange(tm)[:, None] < jnp.arange(tm)[None, :]).astype(jnp.bfloat16)
    kt = lambda: pl.BlockSpec((TOP_K, tm), lambda i: (0, i))
    return pl.pallas_call(
        _route_kernel,
        grid=(n_tok // tm,),
        in_specs=[pl.BlockSpec((n_e, tm), lambda i: (0, i)),
                  pl.BlockSpec((n_e, 1), lambda i: (0, 0)),
                  pl.BlockSpec((tm, tm), lambda i: (0, 0))],
        out_specs=[kt(), kt(), kt(), pl.BlockSpec((n_e, 1), lambda i: (0, 0))],
        out_shape=[jax.ShapeDtypeStruct((TOP_K, n_tok), jnp.int32),
                   jax.ShapeDtypeStruct((TOP_K, n_tok), jnp.float32),
                   jax.ShapeDtypeStruct((TOP_K, n_tok), jnp.int32),
                   jax.ShapeDtypeStruct((n_e, 1), jnp.float32)],
        scratch_shapes=[pltpu.VMEM((n_e, 1), jnp.float32)],
        compiler_params=_cparams("arbitrary"),
        name="route",
    )(logits_t, router_bias.reshape(n_e, 1), tri)


def _dest_kernel(e_ref, r_ref, ps_ref, d_ref):
    n_e = ps_ref.shape[0]
    tm = e_ref.shape[1]
    eio = lax.broadcasted_iota(jnp.int32, (n_e, tm), 0)
    ps = ps_ref[...]
    rows = [jnp.sum(jnp.where(eio == e_ref[k:k + 1, :], ps, 0.0), axis=0, keepdims=True)
            for k in range(TOP_K)]
    d_ref[...] = jnp.concatenate(rows, axis=0).astype(jnp.int32) + r_ref[...]


def dest_rows(e_idx, rank, pad_start, tm):
    n_tok = e_idx.shape[1]
    n_e = pad_start.shape[0]
    kt = lambda: pl.BlockSpec((TOP_K, tm), lambda i: (0, i))
    return pl.pallas_call(
        _dest_kernel,
        grid=(n_tok // tm,),
        in_specs=[kt(), kt(), pl.BlockSpec((n_e, 1), lambda i: (0, 0))],
        out_specs=kt(),
        out_shape=jax.ShapeDtypeStruct((TOP_K, n_tok), jnp.int32),
        compiler_params=_cparams("arbitrary"),
        name="dest",
    )(e_idx, rank, pad_start.astype(jnp.float32).reshape(n_e, 1))


def _dispatch_kernel(fill_start_ref, fill_len_ref, nu_ref, dest_ref, h_ref, xs_ref, zeros, sem, zsem,
                     *, n_s, bm, experts_per_step, tiles_per_step, n_tiles):
    i = pl.program_id(0)
    td = dest_ref.shape[1]

    @pl.when(i == 0)
    def _():
        zeros[...] = jnp.zeros_like(zeros)

    def body(t, carry):
        for k in range(TOP_K):
            pltpu.make_async_copy(_row(h_ref, t, n_s), _row(xs_ref, dest_ref[k, t], n_s),
                                  sem).start(priority=k % 2)
        return carry

    lax.fori_loop(0, td, body, 0)

    for j in range(experts_per_step):
        e = i * experts_per_step + j
        start, length = fill_start_ref[e], fill_len_ref[e]
        piece = bm // 2
        while piece >= 1:
            @pl.when((length & piece) != 0)
            def _(piece=piece):
                off = start + (length & ~(2 * piece - 1))
                cp = pltpu.make_async_copy(zeros.at[pl.ds(0, piece * n_s), :],
                                           xs_ref.at[pl.ds(off * n_s, piece * n_s), :], zsem)
                cp.start()
                cp.wait()
            piece //= 2

    for j in range(tiles_per_step):
        tile = i * tiles_per_step + j

        @pl.when((tile >= nu_ref[0]) & (tile < n_tiles))
        def _(tile=tile):
            cp = pltpu.make_async_copy(zeros, xs_ref.at[pl.ds(tile * (bm * n_s), bm * n_s), :], zsem)
            cp.start()
            cp.wait()

    for _ in range(TOP_K):
        pltpu.make_async_copy(h_ref, xs_ref.at[pl.ds(0, td * n_s), :], sem).wait()


def dispatch(dest, h2p, fill_start, fill_len, n_used, n_tok, n_tiles, td, bm):
    n_s = h2p.shape[0] // n_tok
    n_steps = n_tok // td
    n_e = fill_start.shape[0]
    experts_per_step = -(-n_e // n_steps)
    tiles_per_step = -(-n_tiles // n_steps)
    pad = n_steps * experts_per_step - n_e
    fill_start = jnp.pad(fill_start, (0, pad))
    fill_len = jnp.pad(fill_len, (0, pad))
    return pl.pallas_call(
        functools.partial(_dispatch_kernel, n_s=n_s, bm=bm, experts_per_step=experts_per_step,
                          tiles_per_step=tiles_per_step, n_tiles=n_tiles),
        grid_spec=pltpu.PrefetchScalarGridSpec(
            num_scalar_prefetch=3,
            grid=(n_steps,),
            in_specs=[pl.BlockSpec((TOP_K, td), lambda i, *_: (0, i), memory_space=pltpu.SMEM),
                      _row_tile_spec(td, n_s * LANES, lambda i, *_: i)],
            out_specs=pl.BlockSpec(memory_space=pl.ANY),
            scratch_shapes=[pltpu.VMEM((bm * n_s, LANES), h2p.dtype),
                            pltpu.SemaphoreType.DMA(()), pltpu.SemaphoreType.DMA(())]),
        out_shape=jax.ShapeDtypeStruct((n_tiles * bm * n_s, LANES), h2p.dtype),
        compiler_params=_cparams("arbitrary"),
        name="dispatch",
    )(fill_start, fill_len, n_used, dest, h2p)


def _expert_kernel(te_ref, nu_ref, nt_ref, x_ref, wg_hbm, wu_hbm, wd_hbm, o_ref,
                   wg_st, wu_st, wd_st, wgu_bf, wd_bf, grp, sems, *, bm):
    i = pl.program_id(0)
    ff = wd_bf.shape[0]
    n_used = nu_ref[0]

    def fetch(e, slot):
        return (pltpu.make_async_copy(wg_hbm.at[e], wg_st.at[slot], sems.at[slot]),
                pltpu.make_async_copy(wu_hbm.at[e], wu_st.at[slot], sems.at[slot]),
                pltpu.make_async_copy(wd_hbm.at[e], wd_st.at[slot], sems.at[slot]))

    @pl.when(i == 0)
    def _():
        grp[0] = 0
        for cp in fetch(te_ref[0], 0):
            cp.start()

    @pl.when(i < n_used)
    def _():
        e = te_ref[i]

        @pl.when((i == 0) | (e != te_ref[jnp.maximum(i - 1, 0)]))
        def _():
            slot = grp[0] & 1
            grp[0] = grp[0] + 1
            for cp in fetch(e, slot):
                cp.wait()
            nxt = i + nt_ref[e]

            @pl.when(nxt < n_used)
            def _():
                for cp in fetch(te_ref[jnp.minimum(nxt, n_used - 1)], 1 - slot):
                    cp.start()

            wgu_bf[:, :ff] = _bf(wg_st[slot])
            wgu_bf[:, ff:] = _bf(wu_st[slot])
            wd_bf[...] = _bf(wd_st[slot])

        hi, lo = _unpack_bf16_pair(_load_row_tiles(x_ref, bm))
        x = jnp.concatenate([hi, lo], axis=1)
        gu = _dot(x, wgu_bf[...])
        hid = _silu(gu[:, :ff]) * gu[:, ff:]
        _store_row_tiles(o_ref, _dot(_bf(hid), wd_bf[...]))

    @pl.when(i >= nu_ref[0])
    def _():
        o_ref[...] = jnp.zeros_like(o_ref)


def expert_ffn(tile_e, n_used, tiles_per_expert, xs, n_rows, w_gate, w_up, w_down, bm):
    dm = w_gate.shape[1]
    half = dm // 2
    ff = w_gate.shape[-1]
    rows_in = _row_tile_spec(bm, half, lambda i, te, nu, nt: jnp.minimum(i, nu[0] - 1))
    rows_out = _row_tile_spec(bm, dm, lambda i, te, nu, nt: i)
    hbm = pl.BlockSpec(memory_space=pl.ANY)
    return pl.pallas_call(
        functools.partial(_expert_kernel, bm=bm),
        grid_spec=pltpu.PrefetchScalarGridSpec(
            num_scalar_prefetch=3,
            grid=(n_rows // bm,),
            in_specs=[rows_in, hbm, hbm, hbm],
            out_specs=rows_out,
            scratch_shapes=[pltpu.VMEM((2, dm, ff), jnp.float32),
                            pltpu.VMEM((2, dm, ff), jnp.float32),
                            pltpu.VMEM((2, ff, dm), jnp.float32),
                            pltpu.VMEM((dm, 2 * ff), jnp.bfloat16),
                            pltpu.VMEM((ff, dm), jnp.bfloat16),
                            pltpu.SMEM((1,), jnp.int32),
                            pltpu.SemaphoreType.DMA((2,))]),
        out_shape=jax.ShapeDtypeStruct((n_rows * (dm // LANES), LANES), jnp.float32),
        compiler_params=_cparams("arbitrary"),
        name="expert",
    )(tile_e, n_used, tiles_per_expert, xs, w_gate, w_up, w_down)


def _combine_kernel(dest_ref, dnext_ref, w_ref, x1_ref, h2_ref, mods_ref, g_ref, wsg_ref, wsu_ref,
                    wsd_ref, ys_ref, o_ref, gbuf, rbuf, wrep, sems, *, n_s):
    i = pl.program_id(0)
    n_steps = pl.num_programs(0)
    tc = x1_ref.shape[0]
    slot = i % 2

    def gather(d_ref, slot_, t):
        for k in range(TOP_K):
            pltpu.make_async_copy(_row(ys_ref, d_ref[k, t], n_s), _row(gbuf.at[slot_, k], t, n_s),
                                  sems.at[slot_]).start(priority=k % 2)

    def weighted(t):
        acc = None
        for k in range(TOP_K):
            wk = jnp.broadcast_to(wrep[k, pl.ds(t, 1), :], (n_s, LANES))
            term = wk * _row(gbuf.at[slot, k], t, n_s)[...]
            acc = term if acc is None else acc + term
        _row(rbuf, t, n_s)[...] = acc

    @pl.when(i == 0)
    def _():
        def body(t, carry):
            gather(dest_ref, 0, t)
            return carry
        lax.fori_loop(0, tc, body, 0)

    w = w_ref[...]
    for k in range(TOP_K):
        wrep[k] = jnp.broadcast_to(w[:, k:k + 1], (tc, LANES))

    for k in range(TOP_K):
        pltpu.make_async_copy(ys_ref.at[pl.ds(0, tc * n_s), :], gbuf.at[slot, k], sems.at[slot]).wait()

    @pl.when(i + 1 < n_steps)
    def _():
        def body(t, carry):
            gather(dnext_ref, 1 - slot, t)
            weighted(t)
            return carry
        lax.fori_loop(0, tc, body, 0)

    @pl.when(i + 1 == n_steps)
    def _():
        def body(t, carry):
            weighted(t)
            return carry
        lax.fori_loop(0, tc, body, 0)

    hi, lo = _unpack_bf16_pair(_load_row_tiles(h2_ref, tc))
    h = jnp.concatenate([hi, lo], axis=1)
    hid = _silu(_dot(h, wsg_ref[...])) * _dot(h, wsu_ref[...])
    y = _dot(_bf(hid), wsd_ref[...]) + _load_row_tiles(rbuf, tc)
    m = mods_ref[0]
    o_ref[...] = x1_ref[...] + m[5:6, :] * _rms(y, g_ref[...])


def combine(dest, w_tk, x1, h2p, mods3, g_post, ws_gate_bf, ws_up_bf, ws_down_bf, ys, seq, tc):
    n_tok, dm = x1.shape
    n_s = dm // LANES
    n_steps = n_tok // tc
    tpb = seq // tc
    full = lambda a: pl.BlockSpec(a.shape, lambda i: (0,) * a.ndim)
    tok = lambda w: pl.BlockSpec((tc, w), lambda i: (i, 0))
    g2 = g_post.reshape(1, dm)
    return pl.pallas_call(
        functools.partial(_combine_kernel, n_s=n_s),
        grid=(n_steps,),
        in_specs=[pl.BlockSpec((TOP_K, tc), lambda i: (0, i), memory_space=pltpu.SMEM),
                  pl.BlockSpec((TOP_K, tc), lambda i: (0, jnp.minimum(i + 1, n_steps - 1)),
                               memory_space=pltpu.SMEM),
                  tok(TOP_K),
                  tok(dm), _row_tile_spec(tc, dm // 2, lambda i: i),
                  pl.BlockSpec((1, N_MODS, dm), lambda i: (i // tpb, 0, 0)),
                  full(g2), full(ws_gate_bf), full(ws_up_bf), full(ws_down_bf),
                  pl.BlockSpec(memory_space=pl.ANY)],
        out_specs=tok(dm),
        out_shape=jax.ShapeDtypeStruct((n_tok, dm), jnp.float32),
        scratch_shapes=[pltpu.VMEM((2, TOP_K, tc * n_s, LANES), jnp.float32),
                        pltpu.VMEM((tc * n_s, LANES), jnp.float32),
                        pltpu.VMEM((TOP_K, tc, LANES), jnp.float32),
                        pltpu.SemaphoreType.DMA((2,))],
        compiler_params=_cparams("arbitrary"),
        name="combine",
    )(dest, dest, w_tk, x1, h2p, mods3, g2, ws_gate_bf, ws_up_bf, ws_down_bf, ys)


def moe_ffn(h2, x1, logits_t, mods3, router_bias, w_gate, w_up, w_down, ws_gate, ws_up, ws_down,
            g_post, seq):
    n_tok, dm = x1.shape
    n_e = w_gate.shape[0]
    bm = EXPERT_ROW_TILE
    e_idx, w_kt, rank, cnt = route(logits_t, router_bias, min(512, n_tok))
    counts = cnt[:, 0].astype(jnp.int32)
    padded = (counts + bm - 1) // bm * bm
    pad_end = jnp.cumsum(padded)
    pad_start = (pad_end - padded).astype(jnp.int32)
    n_tiles = -(-(n_tok * TOP_K) // bm) + n_e
    n_used = (pad_end[-1] // bm).astype(jnp.int32)
    tile_start = jnp.arange(n_tiles, dtype=jnp.int32) * bm
    tile_e = jnp.minimum(jnp.sum((pad_end[None, :] <= tile_start[:, None]).astype(jnp.int32), axis=1),
                         n_e - 1).astype(jnp.int32)
    tile_e = jnp.where(jnp.arange(n_tiles) < n_used, tile_e, tile_e[n_used - 1])
    dest = dest_rows(e_idx, rank, pad_start, min(512, n_tok))
    n_used1 = n_used.reshape(1)
    xs = dispatch(dest, h2, pad_start + counts, padded - counts, n_used1, n_tok, n_tiles,
                  min(256, n_tok), bm)
    ys = expert_ffn(tile_e, n_used1, (padded // bm).astype(jnp.int32), xs, n_tiles * bm,
                    w_gate, w_up, w_down, bm)
    bf = lambda a: a.astype(jnp.bfloat16)
    return combine(dest, w_kt.T, x1, h2, mods3, g_post, bf(ws_gate), bf(ws_up), bf(ws_down), ys,
                   seq, min(256, n_tok))


def kernel(x, c, w_ada, b_ada, g_pre_mix, g_post_mix, g_pre_ffn, g_post_ffn, w_in, mu_shift, rwkv_w0, rwkv_w2, rwkv_a0, rwkv_a2, rwkv_g2, rwkv_k_k, rwkv_k_a, rwkv_r_k, rwkv_ln_w, rwkv_ln_b, s5_log_dt, s5_a_re, s5_a_im, s5_b_re, s5_b_im, s5_c_re, s5_c_im, s5_d, s5_w_glu, s5_b_glu, w_out, w_router, router_bias, w_gate, w_up, w_down, ws_gate, ws_up, ws_down):
    bsz, seq, dm = x.shape
    depth = w_ada.shape[0]
    bf = lambda a: a.astype(jnp.bfloat16)
    tm = min(512, seq)
    for l in range(depth):
        mods3 = ada_mods(c, w_ada[l], b_ada[l]).reshape(bsz, N_MODS, dm)
        n_rwkv = mu_shift.shape[-1]
        p, u = in_proj(x, mods3, g_pre_mix[l], bf(w_in[l]), n_rwkv, tm)
        y_rwkv = rwkv_mixer(p, mu_shift[l], rwkv_w0[l], rwkv_w2[l], rwkv_a0[l], rwkv_a2[l],
                            rwkv_g2[l], rwkv_k_k[l], rwkv_k_a[l], rwkv_r_k[l].reshape(-1),
                            rwkv_ln_w[l], rwkv_ln_b[l])
        y_s5 = s5_core(u, s5_log_dt[l], s5_a_re[l], s5_a_im[l], s5_b_re[l], s5_b_im[l],
                       s5_c_re[l], s5_c_im[l])
        flat = lambda a: a.reshape(bsz * seq, a.shape[-1])
        x1, h2, logits_t = out_proj(flat(y_rwkv), flat(y_s5), flat(u), flat(x), mods3, s5_d[l],
                                    bf(s5_w_glu[l]), s5_b_glu[l], bf(w_out[l]), g_post_mix[l],
                                    g_pre_ffn[l], w_router[l].T, seq, tm)
        out = moe_ffn(h2, x1, logits_t, mods3, router_bias[l], w_gate[l], w_up[l], w_down[l],
                      ws_gate[l], ws_up[l], ws_down[l], g_post_ffn[l], seq)
        x = out.reshape(bsz, seq, dm)
    return x
```

```python
import functools
import math

import jax
import jax.numpy as jnp
from jax import lax
from jax.experimental import pallas as pl
from jax.experimental.pallas import tpu as pltpu

NORM_EPS = 1e-6
LNX_EPS = 64e-5
L2_EPS = 1e-12
S5_MAX_REAL = -1e-4
ROUTE_SCALE = 2.5
N_MODS = 6

RWKV_HEAD_DIM = 64
DECAY_LORA = 64
ICLR_LORA = 64
GATE_LORA = 128
S5_GROUP = 16
S5_STATE = 64
TOP_K = 8
ROUTE_GROUPS = 8
ROUTE_TOPK_GROUPS = 4

RWKV_CHUNK = 64
S5_CHUNK = 32
EXPERT_ROW_TILE = 256

VMEM_LIMIT = 56 * 1024 * 1024

HI = lax.Precision.HIGHEST


def _cparams(*sem):
    return pltpu.CompilerParams(dimension_semantics=sem, vmem_limit_bytes=VMEM_LIMIT)


def _dot(a, b, precision=None):
    return jnp.dot(a, b, preferred_element_type=jnp.float32, precision=precision)


def _dot_nt(a, b, precision=None):
    return lax.dot_general(a, b, (((1,), (1,)), ((), ())),
                           preferred_element_type=jnp.float32, precision=precision)


def _dot_tn(a, b, precision=None):
    return lax.dot_general(a, b, (((0,), (0,)), ((), ())),
                           preferred_element_type=jnp.float32, precision=precision)


def _bdot(a, b):
    return lax.dot_general(a, b, (((2,), (1,)), ((0,), (0,))), preferred_element_type=jnp.float32)


def _bdot_nt(a, b):
    return lax.dot_general(a, b, (((2,), (2,)), ((0,), (0,))), preferred_element_type=jnp.float32)


def _bf(x):
    return x.astype(jnp.bfloat16)


def _pack_bf16_pair(hi, lo):
    hb = lax.bitcast_convert_type(_bf(hi).astype(jnp.float32), jnp.uint32)
    lb = lax.bitcast_convert_type(_bf(lo).astype(jnp.float32), jnp.uint32)
    return (hb & jnp.uint32(0xFFFF0000)) | (lb >> 16)


def _unpack_pair_f32(w):
    hi = lax.bitcast_convert_type(w & jnp.uint32(0xFFFF0000), jnp.float32)
    lo = lax.bitcast_convert_type(w << 16, jnp.float32)
    return hi, lo


def _unpack_bf16_pair(w):
    hi, lo = _unpack_pair_f32(w)
    return _bf(hi), _bf(lo)


LANES = 128


def _row_tile_spec(rows, width, row_block):
    return pl.BlockSpec((rows * (width // LANES), LANES), lambda *a: (row_block(*a), 0))


def _store_row_tiles(ref, x):
    rows = x.shape[0]
    n_s = ref.shape[0] // rows
    for s in range(n_s):
        ref[pl.ds(s, rows, stride=n_s), :] = x[:, s * LANES:(s + 1) * LANES]


def _load_row_tiles(ref, rows):
    n_s = ref.shape[0] // rows
    return jnp.concatenate([ref[pl.ds(s, rows, stride=n_s), :] for s in range(n_s)], axis=1)


def _row(ref, r, n_s):
    return ref.at[pl.ds(pl.multiple_of(r * n_s, n_s), n_s), :]


def _sigmoid(x):
    return 1.0 / (1.0 + jnp.exp(-x))


def _silu(x):
    return x * _sigmoid(x)


def _rms(x, gain):
    return x * lax.rsqrt(jnp.mean(x * x, axis=-1, keepdims=True) + NORM_EPS) * gain


def _ada_kernel(c_ref, w_ref, b_ref, o_ref):
    c = c_ref[...]
    o_ref[...] = _dot(_silu(c), w_ref[...], HI) + b_ref[...]


def ada_mods(c, w_ada, b_ada):
    bsz, dm = c.shape
    n = w_ada.shape[1]
    tn = dm
    return pl.pallas_call(
        _ada_kernel,
        grid=(n // tn,),
        in_specs=[pl.BlockSpec((bsz, dm), lambda j: (0, 0)),
                  pl.BlockSpec((dm, tn), lambda j: (0, j)),
                  pl.BlockSpec((1, tn), lambda j: (0, j))],
        out_specs=pl.BlockSpec((bsz, tn), lambda j: (0, j)),
        out_shape=jax.ShapeDtypeStruct((bsz, n), jnp.float32),
        compiler_params=_cparams("arbitrary"),
        name="ada",
    )(c, w_ada, b_ada.reshape(1, n))


def _inproj_kernel(x_ref, mods_ref, g_ref, w_ref, p_ref, u_ref):
    x = x_ref[0]
    m = mods_ref[0]
    h = _rms(x, g_ref[...]) * (1.0 + m[1:2, :]) + m[0:1, :]
    proj = _dot(_bf(h), w_ref[...])
    n_p = p_ref.shape[-1]
    p_ref[0] = proj[:, :n_p]
    u_ref[0] = proj[:, n_p:]


def in_proj(x, mods3, g_pre, w_in_bf, n_rwkv, tm):
    bsz, seq, dm = x.shape
    n = w_in_bf.shape[1]
    n_s5 = n - n_rwkv
    return pl.pallas_call(
        _inproj_kernel,
        grid=(bsz, seq // tm),
        in_specs=[pl.BlockSpec((1, tm, dm), lambda b, i: (b, i, 0)),
                  pl.BlockSpec((1, N_MODS, dm), lambda b, i: (b, 0, 0)),
                  pl.BlockSpec((1, dm), lambda b, i: (0, 0)),
                  pl.BlockSpec((dm, n), lambda b, i: (0, 0))],
        out_specs=[pl.BlockSpec((1, tm, n_rwkv), lambda b, i: (b, i, 0)),
                   pl.BlockSpec((1, tm, n_s5), lambda b, i: (b, i, 0))],
        out_shape=[jax.ShapeDtypeStruct((bsz, seq, n_rwkv), jnp.float32),
                   jax.ShapeDtypeStruct((bsz, seq, n_s5), jnp.float32)],
        compiler_params=_cparams("arbitrary", "arbitrary"),
        name="inproj",
    )(x, mods3, g_pre.reshape(1, dm), w_in_bf)


RWKV_TILE = 256


def _split_dot(x, m01, terms):
    acc = None
    rem = x
    for _ in range(terms):
        piece = _bf(rem)
        part = _dot(piece, m01)
        acc = part if acc is None else acc + part
        rem = rem - piece.astype(jnp.float32)
    return acc


def _rwkv_kernel(p_ref, mu_ref, w0_ref, w2_ref, a0_ref, a2_ref, g2_ref, kk_ref, ka_ref,
                  rk_ref, lnw_ref, lnb_ref, bd_ref, o_ref,
                  s_ref, carry_ref, rt_ref, at_ref, bt_ref, kt_ref, v_ref, wl_ref, y_ref,
                  *, width, chunk):
    hd = RWKV_HEAD_DIM
    pw = 2 * hd
    pairs = width // pw
    i = pl.program_id(1)

    @pl.when(i == 0)
    def _():
        s_ref[...] = jnp.zeros_like(s_ref)
        carry_ref[...] = jnp.zeros_like(carry_ref)

    p = p_ref[0]
    n_t = p.shape[0]
    n_chunks = n_t // chunk
    row = lax.broadcasted_iota(jnp.int32, p.shape, 0)
    prev = jnp.where(row == 0, carry_ref[0:1, :], pltpu.roll(p, 1, axis=0))
    carry_ref[0:1, :] = p[n_t - 1:n_t, :]
    pm = p + (prev - p) * mu_ref[...]

    r = pm[:, 0:width]
    k = pm[:, width:2 * width]
    v = pm[:, 2 * width:3 * width]
    c0 = 3 * width
    w_lo = pm[:, c0:c0 + DECAY_LORA]
    a_lo = pm[:, c0 + DECAY_LORA:c0 + DECAY_LORA + ICLR_LORA]
    g_lo = pm[:, c0 + DECAY_LORA + ICLR_LORA:]

    z = w0_ref[...] + _dot(_bf(jnp.tanh(w_lo)), w2_ref[...])
    softplus_neg = jnp.maximum(-z, 0.0) + jnp.log(1.0 + jnp.exp(-jnp.abs(z)))
    logd = -jnp.exp(-softplus_neg - 0.5)
    iclr = _sigmoid(a0_ref[...] + _dot(_bf(a_lo), a2_ref[...]))
    gate = _dot(_bf(_sigmoid(g_lo)), g2_ref[...])

    bd = bd_ref[...]
    kk = k * kk_ref[...]
    kk = kk / jnp.maximum(jnp.sqrt(_split_dot(kk * kk, bd, 2)), L2_EPS)
    k2 = k * (1.0 + (iclr - 1.0) * ka_ref[...])
    bonus = _split_dot(r * k2 * rk_ref[...], bd, 2) * v

    ti = lax.broadcasted_iota(jnp.int32, (n_t, n_t), 0)
    si = lax.broadcasted_iota(jnp.int32, (n_t, n_t), 1)
    tri = jnp.where((ti >= si) & (ti // chunk == si // chunk), 1.0, 0.0).astype(jnp.bfloat16)
    cum = _split_dot_lhs(tri, logd, 3)
    e_pos = jnp.exp(cum)
    e_neg = jnp.exp(-cum)
    rt_ref[...] = _bf(r * e_pos)
    at_ref[...] = _bf(-kk * jnp.exp(cum - logd))
    bt_ref[...] = _bf(kk * iclr * e_neg)
    kt_ref[...] = _bf(k2 * e_neg)
    v_ref[...] = _bf(v)
    for c in range(n_chunks):
        wl_ref[c:c + 1, :] = e_pos[(c + 1) * chunk - 1:(c + 1) * chunk, :]

    two_l = 2 * chunk
    lane = lax.broadcasted_iota(jnp.int32, (chunk, pw), 1)
    lane0 = lane < hd
    bi = lax.broadcasted_iota(jnp.int32, (two_l, two_l), 0)
    bj = lax.broadcasted_iota(jnp.int32, (two_l, two_l), 1)
    same = (bi // chunk) == (bj // chunk)
    low_strict = same & (bi > bj)
    low_incl = same & (bi >= bj)
    eye_t = jnp.where(bi == bj, 1.0, 0.0)
    pi_ = lax.broadcasted_iota(jnp.int32, (pw, pw), 0)
    pj_ = lax.broadcasted_iota(jnp.int32, (pw, pw), 1)
    eye_p = jnp.where(pi_ == pj_, 1.0, 0.0)
    n_sq = max(1, int(math.ceil(math.log2(chunk))) - 1)
    zero = jnp.zeros((), jnp.bfloat16)

    def stack(x):
        return jnp.concatenate([jnp.where(lane0, x, zero), jnp.where(lane0, zero, x)], axis=0)

    def gather(ref):
        return jnp.stack([stack(ref[c * chunk:(c + 1) * chunk, hp * pw:(hp + 1) * pw])
                          for c in range(n_chunks) for hp in range(pairs)], axis=0)

    a_s, b_s, k_s, r_s, v_s = (gather(ref) for ref in (at_ref, bt_ref, kt_ref, rt_ref, v_ref))
    wl = jnp.stack([wl_ref[c:c + 1, hp * pw:(hp + 1) * pw]
                    for c in range(n_chunks) for hp in range(pairs)], axis=0)
    gram = _bdot_nt(jnp.concatenate([a_s, r_s], axis=1), jnp.concatenate([b_s, k_s], axis=1))
    m_ab = jnp.where(low_strict, gram[:, :two_l, :two_l], 0.0)
    m_ak = jnp.where(low_strict, gram[:, :two_l, two_l:], 0.0)
    n_rb = jnp.where(low_incl, gram[:, two_l:, :two_l], 0.0)
    n_rk = jnp.where(low_incl, gram[:, two_l:, two_l:], 0.0)
    t_inv = eye_t + m_ab
    m_pow = _bf(m_ab)
    for _ in range(n_sq):
        m_pow = _bf(_bdot(m_pow, m_pow))
        t_inv = t_inv + _bdot(_bf(t_inv), m_pow)
    makv = _bdot(_bf(m_ak), v_s)
    tx_bf = _bf(_bdot(_bf(t_inv), jnp.concatenate([a_s, _bf(makv)], axis=2)))
    nx = _bdot(_bf(n_rb), tx_bf)
    rbar = _bf(r_s.astype(jnp.float32) + nx[:, :, :pw])
    y0 = nx[:, :, pw:] + _bdot(_bf(n_rk), v_s)
    tb = _bdot(jnp.swapaxes(tx_bf, 1, 2), b_s)
    pmat = _bf((eye_p + tb[:, :pw, :]) * wl)
    dmat = (tb[:, pw:, :] + _bdot(jnp.swapaxes(v_s, 1, 2), k_s)) * wl

    s = s_ref[...]
    for c in range(n_chunks):
        sel = slice(c * pairs, (c + 1) * pairs)
        s_bf = _bf(s)
        ys = _bdot_nt(rbar[sel], s_bf) + y0[sel]
        yc_ = ys[:, :chunk, :] + ys[:, chunk:, :]
        for hp in range(pairs):
            y_ref[c * chunk:(c + 1) * chunk, hp * pw:(hp + 1) * pw] = yc_[hp]
        s = _bdot(s_bf, pmat[sel]) + dmat[sel]
    s_ref[...] = s

    y = y_ref[...]
    inv_hd = 1.0 / hd
    mean = _split_dot(y, bd, 2) * inv_hd
    yc = y - mean
    var = _split_dot(yc * yc, bd, 2) * inv_hd
    yn = yc * lax.rsqrt(var + LNX_EPS) * lnw_ref[...] + lnb_ref[...]
    o_ref[0] = (yn + bonus) * gate


def _split_dot_lhs(m01, x, terms):
    acc = None
    rem = x
    for _ in range(terms):
        piece = _bf(rem)
        part = _dot(m01, piece)
        acc = part if acc is None else acc + part
        rem = rem - piece.astype(jnp.float32)
    return acc


def rwkv_mixer(p, mu, w0, w2, a0, a2, g2, k_k, k_a, r_k, ln_w, ln_b):
    bsz, seq, n_p = p.shape
    width = w0.shape[-1]
    pairs = width // (2 * RWKV_HEAD_DIM)
    chunk = min(RWKV_CHUNK, seq)
    tile = min(RWKV_TILE, seq)
    hid = jnp.arange(width, dtype=jnp.int32) // RWKV_HEAD_DIM
    bd = (hid[:, None] == hid[None, :]).astype(jnp.bfloat16)
    row = lambda t: t.reshape(1, -1)
    full = lambda a: pl.BlockSpec(a.shape, lambda b, i: (0,) * a.ndim)
    consts = [row(mu), row(w0), _bf(w2), row(a0), _bf(a2), _bf(g2), row(k_k), row(k_a), row(r_k),
              row(ln_w), row(ln_b), bd]
    act = lambda: pltpu.VMEM((tile, width), jnp.bfloat16)
    return pl.pallas_call(
        functools.partial(_rwkv_kernel, width=width, chunk=chunk),
        grid=(bsz, seq // tile),
        in_specs=[pl.BlockSpec((1, tile, n_p), lambda b, i: (b, i, 0))] + [full(a) for a in consts],
        out_specs=pl.BlockSpec((1, tile, width), lambda b, i: (b, i, 0)),
        out_shape=jax.ShapeDtypeStruct((bsz, seq, width), jnp.float32),
        scratch_shapes=[pltpu.VMEM((pairs, 2 * RWKV_HEAD_DIM, 2 * RWKV_HEAD_DIM), jnp.float32),
                        pltpu.VMEM((8, n_p), jnp.float32),
                        act(), act(), act(), act(), act(),
                        pltpu.VMEM((max(8, tile // chunk), width), jnp.float32),
                        pltpu.VMEM((tile, width), jnp.float32)],
        compiler_params=_cparams("arbitrary", "arbitrary"),
        name="rwkv",
    )(p, *consts)


def _s5_discretise(a_re, a_im, dt):
    lam_re = jnp.minimum(a_re, S5_MAX_REAL)
    lam_im = a_im
    log_mag = lam_re * dt
    ang = lam_im * dt
    mag = jnp.exp(log_mag)
    ab_re, ab_im = mag * jnp.cos(ang), mag * jnp.sin(ang)
    den = lam_re * lam_re + lam_im * lam_im
    n_re, n_im = ab_re - 1.0, ab_im
    q_re = (n_re * lam_re + n_im * lam_im) / den
    q_im = (n_im * lam_re - n_re * lam_im) / den
    return log_mag, ang, q_re, q_im


def _s5_power(log_mag, ang, t):
    mag = jnp.exp(t * log_mag)
    return mag * jnp.cos(t * ang), mag * jnp.sin(t * ang)


def _s5ops_kernel(ldt_ref, ar_row, ai_row, ar_col, ai_col, bt_re, bt_im, ct_re, ct_im, til_ref,
                  toep_ref, pm_ref, q_ref, lvl_ref, *, lc):
    n_c, n_p = bt_re.shape[1], bt_re.shape[2]
    lw = lc * n_c
    dt = jnp.exp(ldt_ref[0])

    lm_r, an_r, q_re, q_im = _s5_discretise(ar_row[0], ai_row[0], dt)
    bbt_re = q_re * bt_re[0] - q_im * bt_im[0]
    bbt_im = q_re * bt_im[0] + q_im * bt_re[0]
    s_row = lax.broadcasted_iota(jnp.int32, (lw, n_p), 0) // n_c
    pw_re, pw_im = _s5_power(lm_r, an_r, (lc - 1 - s_row).astype(jnp.float32))
    tb_re = jnp.concatenate([bbt_re] * lc, axis=0)
    tb_im = jnp.concatenate([bbt_im] * lc, axis=0)
    pm_ref[0] = _bf(jnp.concatenate([pw_re * tb_re - pw_im * tb_im,
                                     pw_re * tb_im + pw_im * tb_re], axis=1))

    lm_c, an_c, _, _ = _s5_discretise(ar_col[0], ai_col[0], dt)
    til = til_ref[...]
    c_re = _dot(ct_re[0], til, HI)
    c_im = _dot(ct_im[0], til, HI)
    t_lane = (lax.broadcasted_iota(jnp.int32, (n_p, lw), 1) // n_c).astype(jnp.float32)
    p0_re, p0_im = _s5_power(lm_c, an_c, t_lane)
    p1_re, p1_im = _s5_power(lm_c, an_c, t_lane + 1.0)
    ca0_re, ca0_im = p0_re * c_re - p0_im * c_im, p0_re * c_im + p0_im * c_re
    q_ref[0] = _bf(jnp.concatenate([p1_re * c_re - p1_im * c_im,
                                    -(p1_re * c_im + p1_im * c_re)], axis=0))
    r0 = _dot(bbt_re, ca0_re, HI) - _dot(bbt_im, ca0_im, HI)
    lane = lax.broadcasted_iota(jnp.int32, (n_c, lw), 1)
    for s in range(lc):
        blk = r0 if s == 0 else jnp.where(lane >= s * n_c, pltpu.roll(r0, s * n_c, axis=1), 0.0)
        toep_ref[0, s * n_c:(s + 1) * n_c, :] = _bf(blk)

    cr, ci = _s5_power(lm_r, an_r, float(lc))
    for j in range(lvl_ref.shape[1]):
        lvl_ref[0, j] = jnp.concatenate([jnp.concatenate([cr, cr], axis=1),
                                         jnp.concatenate([-ci, ci], axis=1)], axis=0)
        cr, ci = cr * cr - ci * ci, 2.0 * cr * ci


def s5_operators(log_dt, a_re, a_im, b_re, b_im, c_re, c_im, n_chunks, lc):
    n_g, n_p = a_re.shape
    n_c = b_re.shape[-1]
    lw = lc * n_c
    n_lvl = max(1, int(math.ceil(math.log2(n_chunks))))
    til = (jnp.arange(lw)[None, :] % n_c == jnp.arange(n_c)[:, None]).astype(jnp.float32)
    t3 = lambda a: jnp.swapaxes(a, 1, 2)
    args = [log_dt.reshape(n_g, 1, 1), a_re.reshape(n_g, 1, n_p), a_im.reshape(n_g, 1, n_p),
            a_re.reshape(n_g, n_p, 1), a_im.reshape(n_g, n_p, 1),
            t3(b_re), t3(b_im), t3(c_re), t3(c_im)]
    per_g = lambda a: pl.BlockSpec((1,) + a.shape[1:], lambda g: (g,) + (0,) * (a.ndim - 1))
    return pl.pallas_call(
        functools.partial(_s5ops_kernel, lc=lc),
        grid=(n_g,),
        in_specs=[per_g(a) for a in args] + [pl.BlockSpec(til.shape, lambda g: (0, 0))],
        out_specs=[pl.BlockSpec((1, lw, lw), lambda g: (g, 0, 0)),
                   pl.BlockSpec((1, lw, 2 * n_p), lambda g: (g, 0, 0)),
                   pl.BlockSpec((1, 2 * n_p, lw), lambda g: (g, 0, 0)),
                   pl.BlockSpec((1, n_lvl, 2, 2 * n_p), lambda g: (g, 0, 0, 0))],
        out_shape=[jax.ShapeDtypeStruct((n_g, lw, lw), jnp.bfloat16),
                   jax.ShapeDtypeStruct((n_g, lw, 2 * n_p), jnp.bfloat16),
                   jax.ShapeDtypeStruct((n_g, 2 * n_p, lw), jnp.bfloat16),
                   jax.ShapeDtypeStruct((n_g, n_lvl, 2, 2 * n_p), jnp.float32)],
        compiler_params=_cparams("arbitrary"),
        name="s5ops",
    )(*args, til)


def _s5_kernel(u_ref, toep_ref, pm_ref, q_ref, lvl_ref, y_ref, *, n_chunks):
    u = _bf(u_ref[0])
    e = _dot(u, pm_ref[0])
    rows, two_p = e.shape
    half = two_p // 2
    cidx = lax.broadcasted_iota(jnp.int32, e.shape, 0) % n_chunks
    x = e
    n_lvl = lvl_ref.shape[1]
    for j in range(n_lvl):
        sh = 1 << j
        if sh >= n_chunks:
            break
        xs = jnp.where(cidx >= sh, pltpu.roll(x, sh, axis=0), 0.0)
        cf = lvl_ref[0, j]
        x = x + xs * cf[0:1, :] + pltpu.roll(xs, half, axis=1) * cf[1:2, :]
    x_in = jnp.where(cidx >= 1, pltpu.roll(x, 1, axis=0), 0.0)
    x_hi = _bf(x_in)
    x_lo = _bf(x_in - x_hi.astype(jnp.float32))
    q = q_ref[0]
    y_ref[0] = _dot(u, toep_ref[0]) + _dot(x_hi, q) + _dot(x_lo, q)


def _to_groups_kernel(u_ref, o_ref, *, lc, n_c):
    per, nb, lw = o_ref.shape
    slot = lax.broadcasted_iota(jnp.int32, (nb, LANES), 1) // n_c
    a = [u_ref[pl.ds(s, nb, stride=lc), :] for s in range(lc)]
    for gi in range(per):
        for j in range(lw // LANES):
            acc = None
            for ai in range(per):
                shift = ((ai - gi) % per) * n_c
                src = a[j * per + ai]
                piece = src if shift == 0 else pltpu.roll(src, shift, axis=1)
                acc = piece if acc is None else jnp.where(slot == ai, piece, acc)
            o_ref[gi, :, j * LANES:(j + 1) * LANES] = acc


def _from_groups_kernel(y_ref, o_ref, *, lc, n_c):
    per, nb, lw = y_ref.shape
    slot = lax.broadcasted_iota(jnp.int32, (nb, LANES), 1) // n_c
    for s in range(lc):
        j, ai = divmod(s, per)
        acc = None
        for gi in range(per):
            shift = ((gi - ai) % per) * n_c
            src = y_ref[gi, :, j * LANES:(j + 1) * LANES]
            piece = src if shift == 0 else pltpu.roll(src, shift, axis=1)
            acc = piece if acc is None else jnp.where(slot == gi, piece, acc)
        o_ref[pl.ds(s, nb, stride=lc), :] = acc


def _group_relayout(x, n_g, lc, to_groups, tile):
    if to_groups:
        n_tok, width = x.shape
    else:
        n_tok, width = x.shape[1] * lc, x.shape[2] // lc * n_g
    n_c = width // n_g
    per = LANES // n_c
    nb = tile // lc
    tok_spec = pl.BlockSpec((tile, LANES), lambda i, vb: (i, vb))
    grp_spec = pl.BlockSpec((per, nb, lc * n_c), lambda i, vb: (vb, i, 0))
    kern = _to_groups_kernel if to_groups else _from_groups_kernel
    out_shape = (n_g, n_tok // lc, lc * n_c) if to_groups else (n_tok, width)
    return pl.pallas_call(
        functools.partial(kern, lc=lc, n_c=n_c),
        grid=(n_tok // tile, n_g // per),
        in_specs=[tok_spec if to_groups else grp_spec],
        out_specs=grp_spec if to_groups else tok_spec,
        out_shape=jax.ShapeDtypeStruct(out_shape, x.dtype),
        compiler_params=_cparams("arbitrary", "arbitrary"),
        name="to_groups" if to_groups else "from_groups",
    )(x)


def s5_core(u, log_dt, a_re, a_im, b_re, b_im, c_re, c_im):
    bsz, seq, width = u.shape
    n_g, n_p = a_re.shape
    n_c = width // n_g
    lc = min(S5_CHUNK, seq)
    n_chunks = seq // lc
    toep, pm, q, lvl = s5_operators(log_dt, a_re, a_im, b_re, b_im, c_re, c_im, n_chunks, lc)
    tile = min(2048, bsz * seq)
    ug = _group_relayout(u.reshape(bsz * seq, width), n_g, lc, True, tile)
    rows, lw = bsz * n_chunks, lc * n_c
    yg = pl.pallas_call(
        functools.partial(_s5_kernel, n_chunks=n_chunks),
        grid=(n_g,),
        in_specs=[pl.BlockSpec((1, rows, lw), lambda g: (g, 0, 0)),
                  pl.BlockSpec((1, lw, lw), lambda g: (g, 0, 0)),
                  pl.BlockSpec((1, lw, 2 * n_p), lambda g: (g, 0, 0)),
                  pl.BlockSpec((1, 2 * n_p, lw), lambda g: (g, 0, 0)),
                  pl.BlockSpec((1,) + lvl.shape[1:], lambda g: (g, 0, 0, 0))],
        out_specs=pl.BlockSpec((1, rows, lw), lambda g: (g, 0, 0)),
        out_shape=jax.ShapeDtypeStruct((n_g, rows, lw), jnp.float32),
        compiler_params=_cparams("arbitrary"),
        name="s5",
    )(ug, toep, pm, q, lvl)
    return _group_relayout(yg, n_g, lc, False, tile).reshape(bsz, seq, width)


def _gelu_tanh(y):
    return 0.5 * y * (1.0 + jnp.tanh(math.sqrt(2.0 / math.pi) * (y + 0.044715 * (y * y * y))))


def _outproj_kernel(yr_ref, ys_ref, u_ref, x_ref, mods_ref, d_ref, wglu_ref, bglu_ref, wout_ref,
                    gpost_ref, gpre_ref, wrt_ref, x1_ref, h2_ref, lg_ref):
    m = mods_ref[0]
    yr = yr_ref[...]
    y5 = _gelu_tanh(ys_ref[...] + d_ref[...] * u_ref[...])
    y5 = y5 * _sigmoid(_dot(_bf(y5), wglu_ref[...]) + bglu_ref[...])
    wr = yr.shape[-1]
    mixed = _dot(_bf(yr), wout_ref[0:wr, :]) + _dot(_bf(y5), wout_ref[wr:, :])
    x1 = x_ref[...] + m[2:3, :] * _rms(mixed, gpost_ref[...])
    x1_ref[...] = x1
    h2 = _rms(x1, gpre_ref[...]) * (1.0 + m[4:5, :]) + m[3:4, :]
    half = h2.shape[-1] // 2
    _store_row_tiles(h2_ref, _pack_bf16_pair(h2[:, :half], h2[:, half:]))
    w_hi, w_lo = wrt_ref[0], wrt_ref[1]
    h_hi = _bf(h2)
    h_lo = _bf(h2 - h_hi.astype(jnp.float32))
    lg_ref[...] = _dot_nt(w_hi, h_hi) + (_dot_nt(w_hi, h_lo) + _dot_nt(w_lo, h_hi))


def out_proj(y_rwkv, y_s5, u, x, mods3, s5_d, w_glu_bf, b_glu, w_out_bf, g_post, g_pre, w_router_t,
             seq, tm):
    n_tok, dm = x.shape
    wr, ws = y_rwkv.shape[-1], y_s5.shape[-1]
    n_e = w_router_t.shape[0]
    tpb = seq // tm
    tok = lambda w: pl.BlockSpec((tm, w), lambda i: (i, 0))
    full = lambda a: pl.BlockSpec(a.shape, lambda i: (0,) * a.ndim)
    row = lambda t: t.reshape(1, -1)
    wr_hi = _bf(w_router_t)
    wr_split = jnp.stack([wr_hi, _bf(w_router_t - wr_hi.astype(jnp.float32))])
    consts = [row(s5_d), w_glu_bf, row(b_glu), w_out_bf, row(g_post), row(g_pre), wr_split]
    return pl.pallas_call(
        _outproj_kernel,
        grid=(n_tok // tm,),
        in_specs=[tok(wr), tok(ws), tok(ws), tok(dm),
                  pl.BlockSpec((1, N_MODS, dm), lambda i: (i // tpb, 0, 0))] + [full(a) for a in consts],
        out_specs=[tok(dm), _row_tile_spec(tm, dm // 2, lambda i: i),
                   pl.BlockSpec((n_e, tm), lambda i: (0, i))],
        out_shape=[jax.ShapeDtypeStruct((n_tok, dm), jnp.float32),
                   jax.ShapeDtypeStruct((n_tok * (dm // 2 // LANES), LANES), jnp.uint32),
                   jax.ShapeDtypeStruct((n_e, n_tok), jnp.float32)],
        compiler_params=_cparams("arbitrary"),
        name="outproj",
    )(y_rwkv, y_s5, u, x, mods3, *consts)


def _route_kernel(lg_ref, bias_ref, tri_ref, e_ref, w_ref, r_ref, cnt_ref, carry_ref):
    i = pl.program_id(0)

    @pl.when(i == 0)
    def _():
        carry_ref[...] = jnp.zeros_like(carry_ref)

    neg = -jnp.inf
    scores = _sigmoid(lg_ref[...])
    n_e, tm = scores.shape
    choice = scores + bias_ref[...]
    gsz = n_e // ROUTE_GROUPS
    c3 = choice.reshape(ROUTE_GROUPS, gsz, tm)
    io = lax.broadcasted_iota(jnp.int32, c3.shape, 1)
    m1 = jnp.max(c3, axis=1, keepdims=True)
    first = jnp.min(jnp.where(c3 == m1, io, gsz), axis=1, keepdims=True)
    m2 = jnp.max(jnp.where(io == first, neg, c3), axis=1, keepdims=True)
    gs = m1 + m2
    gi = lax.broadcasted_iota(jnp.int32, gs.shape, 0)
    rank = jnp.zeros(gs.shape, jnp.int32)
    for j in range(ROUTE_GROUPS):
        gj = gs[j:j + 1]
        beats = (gj > gs) | ((gj == gs) & (gi > j))
        rank = rank + beats.astype(jnp.int32)
    masked = jnp.where(rank < ROUTE_TOPK_GROUPS, c3, neg).reshape(n_e, tm)

    eio = lax.broadcasted_iota(jnp.int32, (n_e, tm), 0)
    ids, ws = [], []
    mhot = jnp.zeros((n_e, tm), jnp.float32)
    for _ in range(TOP_K):
        m = jnp.max(masked, axis=0, keepdims=True)
        idx = jnp.min(jnp.where(masked == m, eio, n_e), axis=0, keepdims=True)
        sel = eio == idx
        ws.append(jnp.sum(jnp.where(sel, scores, 0.0), axis=0, keepdims=True))
        ids.append(idx)
        masked = jnp.where(sel, neg, masked)
        mhot = jnp.where(sel, 1.0, mhot)
    wsum = ws[0]
    for t in ws[1:]:
        wsum = wsum + t
    before = _dot(_bf(mhot), tri_ref[...]) + carry_ref[...]
    ranks = [jnp.sum(jnp.where(eio == idx, before, 0.0), axis=0, keepdims=True) for idx in ids]
    e_ref[...] = jnp.concatenate(ids, axis=0)
    w_ref[...] = jnp.concatenate(ws, axis=0) / wsum * ROUTE_SCALE
    r_ref[...] = jnp.concatenate(ranks, axis=0).astype(jnp.int32)
    carry_ref[...] = carry_ref[...] + jnp.sum(mhot, axis=1, keepdims=True)
    cnt_ref[...] = carry_ref[...]


def route(logits_t, router_bias, tm):
    n_e, n_tok = logits_t.shape
    tri = (jnp.arange(tm)[:, None] < jnp.arange(tm)[None, :]).astype(jnp.bfloat16)
    kt = lambda: pl.BlockSpec((TOP_K, tm), lambda i: (0, i))
    return pl.pallas_call(
        _route_kernel,
        grid=(n_tok // tm,),
        in_specs=[pl.BlockSpec((n_e, tm), lambda i: (0, i)),
                  pl.BlockSpec((n_e, 1), lambda i: (0, 0)),
                  pl.BlockSpec((tm, tm), lambda i: (0, 0))],
        out_specs=[kt(), kt(), kt(), pl.BlockSpec((n_e, 1), lambda i: (0, 0))],
        out_shape=[jax.ShapeDtypeStruct((TOP_K, n_tok), jnp.int32),
                   jax.ShapeDtypeStruct((TOP_K, n_tok), jnp.float32),
                   jax.ShapeDtypeStruct((TOP_K, n_tok), jnp.int32),
                   jax.ShapeDtypeStruct((n_e, 1), jnp.float32)],
        scratch_shapes=[pltpu.VMEM((n_e, 1), jnp.float32)],
        compiler_params=_cparams("arbitrary"),
        name="route",
    )(logits_t, router_bias.reshape(n_e, 1), tri)


def _dest_kernel(e_ref, r_ref, ps_ref, d_ref):
    n_e = ps_ref.shape[0]
    tm = e_ref.shape[1]
    eio = lax.broadcasted_iota(jnp.int32, (n_e, tm), 0)
    ps = ps_ref[...]
    rows = [jnp.sum(jnp.where(eio == e_ref[k:k + 1, :], ps, 0.0), axis=0, keepdims=True)
            for k in range(TOP_K)]
    d_ref[...] = jnp.concatenate(rows, axis=0).astype(jnp.int32) + r_ref[...]


def dest_rows(e_idx, rank, pad_start, tm):
    n_tok = e_idx.shape[1]
    n_e = pad_start.shape[0]
    kt = lambda: pl.BlockSpec((TOP_K, tm), lambda i: (0, i))
    return pl.pallas_call(
        _dest_kernel,
        grid=(n_tok // tm,),
        in_specs=[kt(), kt(), pl.BlockSpec((n_e, 1), lambda i: (0, 0))],
        out_specs=kt(),
        out_shape=jax.ShapeDtypeStruct((TOP_K, n_tok), jnp.int32),
        compiler_params=_cparams("arbitrary"),
        name="dest",
    )(e_idx, rank, pad_start.astype(jnp.float32).reshape(n_e, 1))


def _dispatch_kernel(fill_start_ref, fill_len_ref, nu_ref, dest_ref, h_ref, xs_ref, zeros, sem, zsem,
                     *, n_s, bm, experts_per_step, tiles_per_step, n_tiles):
    i = pl.program_id(0)
    td = dest_ref.shape[1]

    @pl.when(i == 0)
    def _():
        zeros[...] = jnp.zeros_like(zeros)

    def body(t, carry):
        for k in range(TOP_K):
            pltpu.make_async_copy(_row(h_ref, t, n_s), _row(xs_ref, dest_ref[k, t], n_s),
                                  sem).start(priority=k % 2)
        return carry

    lax.fori_loop(0, td, body, 0)

    for j in range(experts_per_step):
        e = i * experts_per_step + j
        start, length = fill_start_ref[e], fill_len_ref[e]
        piece = bm // 2
        while piece >= 1:
            @pl.when((length & piece) != 0)
            def _(piece=piece):
                off = start + (length & ~(2 * piece - 1))
                cp = pltpu.make_async_copy(zeros.at[pl.ds(0, piece * n_s), :],
                                           xs_ref.at[pl.ds(off * n_s, piece * n_s), :], zsem)
                cp.start()
                cp.wait()
            piece //= 2

    for j in range(tiles_per_step):
        tile = i * tiles_per_step + j

        @pl.when((tile >= nu_ref[0]) & (tile < n_tiles))
        def _(tile=tile):
            cp = pltpu.make_async_copy(zeros, xs_ref.at[pl.ds(tile * (bm * n_s), bm * n_s), :], zsem)
            cp.start()
            cp.wait()

    for _ in range(TOP_K):
        pltpu.make_async_copy(h_ref, xs_ref.at[pl.ds(0, td * n_s), :], sem).wait()


def dispatch(dest, h2p, fill_start, fill_len, n_used, n_tok, n_tiles, td, bm):
    n_s = h2p.shape[0] // n_tok
    n_steps = n_tok // td
    n_e = fill_start.shape[0]
    experts_per_step = -(-n_e // n_steps)
    tiles_per_step = -(-n_tiles // n_steps)
    pad = n_steps * experts_per_step - n_e
    fill_start = jnp.pad(fill_start, (0, pad))
    fill_len = jnp.pad(fill_len, (0, pad))
    return pl.pallas_call(
        functools.partial(_dispatch_kernel, n_s=n_s, bm=bm, experts_per_step=experts_per_step,
                          tiles_per_step=tiles_per_step, n_tiles=n_tiles),
        grid_spec=pltpu.PrefetchScalarGridSpec(
            num_scalar_prefetch=3,
            grid=(n_steps,),
            in_specs=[pl.BlockSpec((TOP_K, td), lambda i, *_: (0, i), memory_space=pltpu.SMEM),
                      _row_tile_spec(td, n_s * LANES, lambda i, *_: i)],
            out_specs=pl.BlockSpec(memory_space=pl.ANY),
            scratch_shapes=[pltpu.VMEM((bm * n_s, LANES), h2p.dtype),
                            pltpu.SemaphoreType.DMA(()), pltpu.SemaphoreType.DMA(())]),
        out_shape=jax.ShapeDtypeStruct((n_tiles * bm * n_s, LANES), h2p.dtype),
        compiler_params=_cparams("arbitrary"),
        name="dispatch",
    )(fill_start, fill_len, n_used, dest, h2p)


def _expert_kernel(te_ref, nu_ref, nt_ref, x_ref, wg_hbm, wu_hbm, wd_hbm, o_ref,
                   wg_st, wu_st, wd_st, wgu_bf, wd_bf, grp, sems, *, bm):
    i = pl.program_id(0)
    ff = wd_bf.shape[0]
    n_used = nu_ref[0]

    def fetch(e, slot):
        return (pltpu.make_async_copy(wg_hbm.at[e], wg_st.at[slot], sems.at[slot]),
                pltpu.make_async_copy(wu_hbm.at[e], wu_st.at[slot], sems.at[slot]),
                pltpu.make_async_copy(wd_hbm.at[e], wd_st.at[slot], sems.at[slot]))

    @pl.when(i == 0)
    def _():
        grp[0] = 0
        for cp in fetch(te_ref[0], 0):
            cp.start()

    @pl.when(i < n_used)
    def _():
        e = te_ref[i]

        @pl.when((i == 0) | (e != te_ref[jnp.maximum(i - 1, 0)]))
        def _():
            slot = grp[0] & 1
            grp[0] = grp[0] + 1
            for cp in fetch(e, slot):
                cp.wait()
            nxt = i + nt_ref[e]

            @pl.when(nxt < n_used)
            def _():
                for cp in fetch(te_ref[jnp.minimum(nxt, n_used - 1)], 1 - slot):
                    cp.start()

            wgu_bf[:, :ff] = _bf(wg_st[slot])
            wgu_bf[:, ff:] = _bf(wu_st[slot])
            wd_bf[...] = _bf(wd_st[slot])

        hi, lo = _unpack_bf16_pair(_load_row_tiles(x_ref, bm))
        x = jnp.concatenate([hi, lo], axis=1)
        gu = _dot(x, wgu_bf[...])
        hid = _silu(gu[:, :ff]) * gu[:, ff:]
        y = _dot(_bf(hid), wd_bf[...])
        half = y.shape[1] // 2
        _store_row_tiles(o_ref, _pack_bf16_pair(y[:, :half], y[:, half:]))

    @pl.when(i >= nu_ref[0])
    def _():
        o_ref[...] = jnp.zeros_like(o_ref)


def expert_ffn(tile_e, n_used, tiles_per_expert, xs, n_rows, w_gate, w_up, w_down, bm):
    dm = w_gate.shape[1]
    half = dm // 2
    ff = w_gate.shape[-1]
    rows_in = _row_tile_spec(bm, half, lambda i, te, nu, nt: jnp.minimum(i, nu[0] - 1))
    rows_out = _row_tile_spec(bm, half, lambda i, te, nu, nt: i)
    hbm = pl.BlockSpec(memory_space=pl.ANY)
    return pl.pallas_call(
        functools.partial(_expert_kernel, bm=bm),
        grid_spec=pltpu.PrefetchScalarGridSpec(
            num_scalar_prefetch=3,
            grid=(n_rows // bm,),
            in_specs=[rows_in, hbm, hbm, hbm],
            out_specs=rows_out,
            scratch_shapes=[pltpu.VMEM((2, dm, ff), jnp.float32),
                            pltpu.VMEM((2, dm, ff), jnp.float32),
                            pltpu.VMEM((2, ff, dm), jnp.float32),
                            pltpu.VMEM((dm, 2 * ff), jnp.bfloat16),
                            pltpu.VMEM((ff, dm), jnp.bfloat16),
                            pltpu.SMEM((1,), jnp.int32),
                            pltpu.SemaphoreType.DMA((2,))]),
        out_shape=jax.ShapeDtypeStruct((n_rows * (half // LANES), LANES), jnp.uint32),
        compiler_params=_cparams("arbitrary"),
        name="expert",
    )(tile_e, n_used, tiles_per_expert, xs, w_gate, w_up, w_down)


def _combine_kernel(dest_ref, dnext_ref, w_ref, x1_ref, h2_ref, mods_ref, g_ref, wsg_ref, wsu_ref,
                    wsd_ref, ys_ref, o_ref, gbuf, rbuf, wrep, sems, *, n_s):
    i = pl.program_id(0)
    n_steps = pl.num_programs(0)
    tc = x1_ref.shape[0]
    slot = i % 2

    def gather(d_ref, slot_, t):
        for k in range(TOP_K):
            pltpu.make_async_copy(_row(ys_ref, d_ref[k, t], n_s), _row(gbuf.at[slot_, k], t, n_s),
                                  sems.at[slot_]).start(priority=k % 2)

    def weighted(t):
        acc_hi = acc_lo = None
        for k in range(TOP_K):
            wk = jnp.broadcast_to(wrep[k, pl.ds(t, 1), :], (n_s, LANES))
            hi, lo = _unpack_pair_f32(_row(gbuf.at[slot, k], t, n_s)[...])
            t_hi, t_lo = wk * hi, wk * lo
            acc_hi = t_hi if acc_hi is None else acc_hi + t_hi
            acc_lo = t_lo if acc_lo is None else acc_lo + t_lo
        _row(rbuf.at[0], t, n_s)[...] = acc_hi
        _row(rbuf.at[1], t, n_s)[...] = acc_lo

    @pl.when(i == 0)
    def _():
        def body(t, carry):
            gather(dest_ref, 0, t)
            return carry
        lax.fori_loop(0, tc, body, 0)

    w = w_ref[...]
    for k in range(TOP_K):
        wrep[k] = jnp.broadcast_to(w[:, k:k + 1], (tc, LANES))

    for k in range(TOP_K):
        pltpu.make_async_copy(ys_ref.at[pl.ds(0, tc * n_s), :], gbuf.at[slot, k], sems.at[slot]).wait()

    @pl.when(i + 1 < n_steps)
    def _():
        def body(t, carry):
            gather(dnext_ref, 1 - slot, t)
            weighted(t)
            return carry
        lax.fori_loop(0, tc, body, 0)

    @pl.when(i + 1 == n_steps)
    def _():
        def body(t, carry):
            weighted(t)
            return carry
        lax.fori_loop(0, tc, body, 0)

    hi, lo = _unpack_bf16_pair(_load_row_tiles(h2_ref, tc))
    h = jnp.concatenate([hi, lo], axis=1)
    hid = _silu(_dot(h, wsg_ref[...])) * _dot(h, wsu_ref[...])
    routed = jnp.concatenate([_load_row_tiles(rbuf.at[0], tc), _load_row_tiles(rbuf.at[1], tc)], axis=1)
    y = _dot(_bf(hid), wsd_ref[...]) + routed
    m = mods_ref[0]
    o_ref[...] = x1_ref[...] + m[5:6, :] * _rms(y, g_ref[...])


def combine(dest, w_tk, x1, h2p, mods3, g_post, ws_gate_bf, ws_up_bf, ws_down_bf, ys, seq, tc):
    n_tok, dm = x1.shape
    n_s = dm // 2 // LANES
    n_steps = n_tok // tc
    tpb = seq // tc
    full = lambda a: pl.BlockSpec(a.shape, lambda i: (0,) * a.ndim)
    tok = lambda w: pl.BlockSpec((tc, w), lambda i: (i, 0))
    g2 = g_post.reshape(1, dm)
    return pl.pallas_call(
        functools.partial(_combine_kernel, n_s=n_s),
        grid=(n_steps,),
        in_specs=[pl.BlockSpec((TOP_K, tc), lambda i: (0, i), memory_space=pltpu.SMEM),
                  pl.BlockSpec((TOP_K, tc), lambda i: (0, jnp.minimum(i + 1, n_steps - 1)),
                               memory_space=pltpu.SMEM),
                  tok(TOP_K),
                  tok(dm), _row_tile_spec(tc, dm // 2, lambda i: i),
                  pl.BlockSpec((1, N_MODS, dm), lambda i: (i // tpb, 0, 0)),
                  full(g2), full(ws_gate_bf), full(ws_up_bf), full(ws_down_bf),
                  pl.BlockSpec(memory_space=pl.ANY)],
        out_specs=tok(dm),
        out_shape=jax.ShapeDtypeStruct((n_tok, dm), jnp.float32),
        scratch_shapes=[pltpu.VMEM((2, TOP_K, tc * n_s, LANES), jnp.uint32),
                        pltpu.VMEM((2, tc * n_s, LANES), jnp.float32),
                        pltpu.VMEM((TOP_K, tc, LANES), jnp.float32),
                        pltpu.SemaphoreType.DMA((2,))],
        compiler_params=_cparams("arbitrary"),
        name="combine",
    )(dest, dest, w_tk, x1, h2p, mods3, g2, ws_gate_bf, ws_up_bf, ws_down_bf, ys)


def moe_ffn(h2, x1, logits_t, mods3, router_bias, w_gate, w_up, w_down, ws_gate, ws_up, ws_down,
            g_post, seq):
    n_tok, dm = x1.shape
    n_e = w_gate.shape[0]
    bm = EXPERT_ROW_TILE
    e_idx, w_kt, rank, cnt = route(logits_t, router_bias, min(512, n_tok))
    counts = cnt[:, 0].astype(jnp.int32)
    padded = (counts + bm - 1) // bm * bm
    pad_end = jnp.cumsum(padded)
    pad_start = (pad_end - padded).astype(jnp.int32)
    n_tiles = -(-(n_tok * TOP_K) // bm) + n_e
    n_used = (pad_end[-1] // bm).astype(jnp.int32)
    tile_start = jnp.arange(n_tiles, dtype=jnp.int32) * bm
    tile_e = jnp.minimum(jnp.sum((pad_end[None, :] <= tile_start[:, None]).astype(jnp.int32), axis=1),
                         n_e - 1).astype(jnp.int32)
    tile_e = jnp.where(jnp.arange(n_tiles) < n_used, tile_e, tile_e[n_used - 1])
    dest = dest_rows(e_idx, rank, pad_start, min(512, n_tok))
    n_used1 = n_used.reshape(1)
    xs = dispatch(dest, h2, pad_start + counts, padded - counts, n_used1, n_tok, n_tiles,
                  min(256, n_tok), bm)
    ys = expert_ffn(tile_e, n_used1, (padded // bm).astype(jnp.int32), xs, n_tiles * bm,
                    w_gate, w_up, w_down, bm)
    bf = lambda a: a.astype(jnp.bfloat16)
    return combine(dest, w_kt.T, x1, h2, mods3, g_post, bf(ws_gate), bf(ws_up), bf(ws_down), ys,
                   seq, min(256, n_tok))


def kernel(x, c, w_ada, b_ada, g_pre_mix, g_post_mix, g_pre_ffn, g_post_ffn, w_in, mu_shift, rwkv_w0, rwkv_w2, rwkv_a0, rwkv_a2, rwkv_g2, rwkv_k_k, rwkv_k_a, rwkv_r_k, rwkv_ln_w, rwkv_ln_b, s5_log_dt, s5_a_re, s5_a_im, s5_b_re, s5_b_im, s5_c_re, s5_c_im, s5_d, s5_w_glu, s5_b_glu, w_out, w_router, router_bias, w_gate, w_up, w_down, ws_gate, ws_up, ws_down):
    bsz, seq, dm = x.shape
    depth = w_ada.shape[0]
    bf = lambda a: a.astype(jnp.bfloat16)
    tm = min(512, seq)
    for l in range(depth):
        mods3 = ada_mods(c, w_ada[l], b_ada[l]).reshape(bsz, N_MODS, dm)
        n_rwkv = mu_shift.shape[-1]
        p, u = in_proj(x, mods3, g_pre_mix[l], bf(w_in[l]), n_rwkv, tm)
        y_rwkv = rwkv_mixer(p, mu_shift[l], rwkv_w0[l], rwkv_w2[l], rwkv_a0[l], rwkv_a2[l],
                            rwkv_g2[l], rwkv_k_k[l], rwkv_k_a[l], rwkv_r_k[l].reshape(-1),
                            rwkv_ln_w[l], rwkv_ln_b[l])
        y_s5 = s5_core(u, s5_log_dt[l], s5_a_re[l], s5_a_im[l], s5_b_re[l], s5_b_im[l],
                       s5_c_re[l], s5_c_im[l])
        flat = lambda a: a.reshape(bsz * seq, a.shape[-1])
        x1, h2, logits_t = out_proj(flat(y_rwkv), flat(y_s5), flat(u), flat(x), mods3, s5_d[l],
                                    bf(s5_w_glu[l]), s5_b_glu[l], bf(w_out[l]), g_post_mix[l],
                                    g_pre_ffn[l], w_router[l].T, seq, tm)
        out = moe_ffn(h2, x1, logits_t, mods3, router_bias[l], w_gate[l], w_up[l], w_down[l],
                      ws_gate[l], ws_up[l], ws_down[l], g_post_ffn[l], seq)
        x = out.reshape(bsz, seq, dm)
    return x
```

```python
import functools
import math

import jax
import jax.numpy as jnp
from jax import lax
from jax.experimental import pallas as pl
from jax.experimental.pallas import tpu as pltpu

NORM_EPS = 1e-6
LNX_EPS = 64e-5
L2_EPS = 1e-12
S5_MAX_REAL = -1e-4
ROUTE_SCALE = 2.5
N_MODS = 6

RWKV_HEAD_DIM = 64
DECAY_LORA = 64
ICLR_LORA = 64
GATE_LORA = 128
S5_GROUP = 16
S5_STATE = 64
TOP_K = 8
ROUTE_GROUPS = 8
ROUTE_TOPK_GROUPS = 4

RWKV_CHUNK = 64
S5_CHUNK = 32
EXPERT_ROW_TILE = 512

VMEM_LIMIT = 56 * 1024 * 1024

HI = lax.Precision.HIGHEST


def _cparams(*sem):
    return pltpu.CompilerParams(dimension_semantics=sem, vmem_limit_bytes=VMEM_LIMIT)


def _dot(a, b, precision=None):
    return jnp.dot(a, b, preferred_element_type=jnp.float32, precision=precision)


def _dot_nt(a, b, precision=None):
    return lax.dot_general(a, b, (((1,), (1,)), ((), ())),
                           preferred_element_type=jnp.float32, precision=precision)


def _dot_tn(a, b, precision=None):
    return lax.dot_general(a, b, (((0,), (0,)), ((), ())),
                           preferred_element_type=jnp.float32, precision=precision)


def _bdot(a, b):
    return lax.dot_general(a, b, (((2,), (1,)), ((0,), (0,))), preferred_element_type=jnp.float32)


def _bdot_nt(a, b):
    return lax.dot_general(a, b, (((2,), (2,)), ((0,), (0,))), preferred_element_type=jnp.float32)


def _bf(x):
    return x.astype(jnp.bfloat16)


def _pack_bf16_pair(hi, lo):
    hb = lax.bitcast_convert_type(_bf(hi).astype(jnp.float32), jnp.uint32)
    lb = lax.bitcast_convert_type(_bf(lo).astype(jnp.float32), jnp.uint32)
    return (hb & jnp.uint32(0xFFFF0000)) | (lb >> 16)


def _unpack_pair_f32(w):
    hi = lax.bitcast_convert_type(w & jnp.uint32(0xFFFF0000), jnp.float32)
    lo = lax.bitcast_convert_type(w << 16, jnp.float32)
    return hi, lo


def _unpack_bf16_pair(w):
    hi, lo = _unpack_pair_f32(w)
    return _bf(hi), _bf(lo)


LANES = 128


def _row_tile_spec(rows, width, row_block):
    return pl.BlockSpec((rows * (width // LANES), LANES), lambda *a: (row_block(*a), 0))


def _store_row_tiles(ref, x):
    rows = x.shape[0]
    n_s = ref.shape[0] // rows
    for s in range(n_s):
        ref[pl.ds(s, rows, stride=n_s), :] = x[:, s * LANES:(s + 1) * LANES]


def _load_row_tiles(ref, rows):
    n_s = ref.shape[0] // rows
    return jnp.concatenate([ref[pl.ds(s, rows, stride=n_s), :] for s in range(n_s)], axis=1)


def _row(ref, r, n_s):
    return ref.at[pl.ds(pl.multiple_of(r * n_s, n_s), n_s), :]


def _sigmoid(x):
    return 1.0 / (1.0 + jnp.exp(-x))


def _silu(x):
    return x * _sigmoid(x)


def _rms(x, gain):
    return x * lax.rsqrt(jnp.mean(x * x, axis=-1, keepdims=True) + NORM_EPS) * gain


def _ada_kernel(c_ref, w_ref, b_ref, o_ref):
    c = c_ref[...]
    o_ref[...] = _dot(_silu(c), w_ref[...], HI) + b_ref[...]


def ada_mods(c, w_ada, b_ada):
    bsz, dm = c.shape
    n = w_ada.shape[1]
    tn = dm
    return pl.pallas_call(
        _ada_kernel,
        grid=(n // tn,),
        in_specs=[pl.BlockSpec((bsz, dm), lambda j: (0, 0)),
                  pl.BlockSpec((dm, tn), lambda j: (0, j)),
                  pl.BlockSpec((1, tn), lambda j: (0, j))],
        out_specs=pl.BlockSpec((bsz, tn), lambda j: (0, j)),
        out_shape=jax.ShapeDtypeStruct((bsz, n), jnp.float32),
        compiler_params=_cparams("arbitrary"),
        name="ada",
    )(c, w_ada, b_ada.reshape(1, n))


def _inproj_kernel(x_ref, mods_ref, g_ref, w_ref, p_ref, u_ref):
    x = x_ref[0]
    m = mods_ref[0]
    h = _rms(x, g_ref[...]) * (1.0 + m[1:2, :]) + m[0:1, :]
    proj = _dot(_bf(h), w_ref[...])
    n_p = p_ref.shape[-1]
    p_ref[0] = proj[:, :n_p]
    u_ref[0] = proj[:, n_p:]


def in_proj(x, mods3, g_pre, w_in_bf, n_rwkv, tm):
    bsz, seq, dm = x.shape
    n = w_in_bf.shape[1]
    n_s5 = n - n_rwkv
    return pl.pallas_call(
        _inproj_kernel,
        grid=(bsz, seq // tm),
        in_specs=[pl.BlockSpec((1, tm, dm), lambda b, i: (b, i, 0)),
                  pl.BlockSpec((1, N_MODS, dm), lambda b, i: (b, 0, 0)),
                  pl.BlockSpec((1, dm), lambda b, i: (0, 0)),
                  pl.BlockSpec((dm, n), lambda b, i: (0, 0))],
        out_specs=[pl.BlockSpec((1, tm, n_rwkv), lambda b, i: (b, i, 0)),
                   pl.BlockSpec((1, tm, n_s5), lambda b, i: (b, i, 0))],
        out_shape=[jax.ShapeDtypeStruct((bsz, seq, n_rwkv), jnp.float32),
                   jax.ShapeDtypeStruct((bsz, seq, n_s5), jnp.float32)],
        compiler_params=_cparams("arbitrary", "arbitrary"),
        name="inproj",
    )(x, mods3, g_pre.reshape(1, dm), w_in_bf)


RWKV_TILE = 256


def _split_dot(x, m01, terms):
    acc = None
    rem = x
    for _ in range(terms):
        piece = _bf(rem)
        part = _dot(piece, m01)
        acc = part if acc is None else acc + part
        rem = rem - piece.astype(jnp.float32)
    return acc


def _rwkv_kernel(p_ref, mu_ref, w0_ref, w2_ref, a0_ref, a2_ref, g2_ref, kk_ref, ka_ref,
                  rk_ref, lnw_ref, lnb_ref, bd_ref, o_ref,
                  s_ref, carry_ref, rt_ref, at_ref, bt_ref, kt_ref, v_ref, wl_ref, y_ref,
                  *, width, chunk):
    hd = RWKV_HEAD_DIM
    pw = 2 * hd
    pairs = width // pw
    i = pl.program_id(1)

    @pl.when(i == 0)
    def _():
        s_ref[...] = jnp.zeros_like(s_ref)
        carry_ref[...] = jnp.zeros_like(carry_ref)

    p = p_ref[0]
    n_t = p.shape[0]
    n_chunks = n_t // chunk
    row = lax.broadcasted_iota(jnp.int32, p.shape, 0)
    prev = jnp.where(row == 0, carry_ref[0:1, :], pltpu.roll(p, 1, axis=0))
    carry_ref[0:1, :] = p[n_t - 1:n_t, :]
    pm = p + (prev - p) * mu_ref[...]

    r = pm[:, 0:width]
    k = pm[:, width:2 * width]
    v = pm[:, 2 * width:3 * width]
    c0 = 3 * width
    w_lo = pm[:, c0:c0 + DECAY_LORA]
    a_lo = pm[:, c0 + DECAY_LORA:c0 + DECAY_LORA + ICLR_LORA]
    g_lo = pm[:, c0 + DECAY_LORA + ICLR_LORA:]

    z = w0_ref[...] + _dot(_bf(jnp.tanh(w_lo)), w2_ref[...])
    softplus_neg = jnp.maximum(-z, 0.0) + jnp.log(1.0 + jnp.exp(-jnp.abs(z)))
    logd = -jnp.exp(-softplus_neg - 0.5)
    iclr = _sigmoid(a0_ref[...] + _dot(_bf(a_lo), a2_ref[...]))
    gate = _dot(_bf(_sigmoid(g_lo)), g2_ref[...])

    bd = bd_ref[...]
    kk = k * kk_ref[...]
    kk = kk / jnp.maximum(jnp.sqrt(_split_dot(kk * kk, bd, 2)), L2_EPS)
    k2 = k * (1.0 + (iclr - 1.0) * ka_ref[...])
    bonus = _split_dot(r * k2 * rk_ref[...], bd, 2) * v

    ti = lax.broadcasted_iota(jnp.int32, (n_t, n_t), 0)
    si = lax.broadcasted_iota(jnp.int32, (n_t, n_t), 1)
    tri = jnp.where((ti >= si) & (ti // chunk == si // chunk), 1.0, 0.0).astype(jnp.bfloat16)
    cum = _split_dot_lhs(tri, logd, 3)
    e_pos = jnp.exp(cum)
    e_neg = jnp.exp(-cum)
    rt_ref[...] = _bf(r * e_pos)
    at_ref[...] = _bf(-kk * jnp.exp(cum - logd))
    bt_ref[...] = _bf(kk * iclr * e_neg)
    kt_ref[...] = _bf(k2 * e_neg)
    v_ref[...] = _bf(v)
    for c in range(n_chunks):
        wl_ref[c:c + 1, :] = e_pos[(c + 1) * chunk - 1:(c + 1) * chunk, :]

    two_l = 2 * chunk
    lane = lax.broadcasted_iota(jnp.int32, (chunk, pw), 1)
    lane0 = lane < hd
    bi = lax.broadcasted_iota(jnp.int32, (two_l, two_l), 0)
    bj = lax.broadcasted_iota(jnp.int32, (two_l, two_l), 1)
    same = (bi // chunk) == (bj // chunk)
    low_strict = same & (bi > bj)
    low_incl = same & (bi >= bj)
    eye_t = jnp.where(bi == bj, 1.0, 0.0)
    pi_ = lax.broadcasted_iota(jnp.int32, (pw, pw), 0)
    pj_ = lax.broadcasted_iota(jnp.int32, (pw, pw), 1)
    eye_p = jnp.where(pi_ == pj_, 1.0, 0.0)
    n_sq = max(1, int(math.ceil(math.log2(chunk))) - 1)
    zero = jnp.zeros((), jnp.bfloat16)

    def stack(x):
        return jnp.concatenate([jnp.where(lane0, x, zero), jnp.where(lane0, zero, x)], axis=0)

    def gather(ref):
        return jnp.stack([stack(ref[c * chunk:(c + 1) * chunk, hp * pw:(hp + 1) * pw])
                          for c in range(n_chunks) for hp in range(pairs)], axis=0)

    a_s, b_s, k_s, r_s, v_s = (gather(ref) for ref in (at_ref, bt_ref, kt_ref, rt_ref, v_ref))
    wl = jnp.stack([wl_ref[c:c + 1, hp * pw:(hp + 1) * pw]
                    for c in range(n_chunks) for hp in range(pairs)], axis=0)
    gram = _bdot_nt(jnp.concatenate([a_s, r_s], axis=1), jnp.concatenate([b_s, k_s], axis=1))
    m_ab = jnp.where(low_strict, gram[:, :two_l, :two_l], 0.0)
    m_ak = jnp.where(low_strict, gram[:, :two_l, two_l:], 0.0)
    n_rb = jnp.where(low_incl, gram[:, two_l:, :two_l], 0.0)
    n_rk = jnp.where(low_incl, gram[:, two_l:, two_l:], 0.0)
    t_inv = eye_t + m_ab
    m_pow = _bf(m_ab)
    for _ in range(n_sq):
        m_pow = _bf(_bdot(m_pow, m_pow))
        t_inv = t_inv + _bdot(_bf(t_inv), m_pow)
    makv = _bdot(_bf(m_ak), v_s)
    tx_bf = _bf(_bdot(_bf(t_inv), jnp.concatenate([a_s, _bf(makv)], axis=2)))
    nx = _bdot(_bf(n_rb), tx_bf)
    rbar = _bf(r_s.astype(jnp.float32) + nx[:, :, :pw])
    y0 = nx[:, :, pw:] + _bdot(_bf(n_rk), v_s)
    tb = _bdot(jnp.swapaxes(tx_bf, 1, 2), b_s)
    pmat = _bf((eye_p + tb[:, :pw, :]) * wl)
    dmat = (tb[:, pw:, :] + _bdot(jnp.swapaxes(v_s, 1, 2), k_s)) * wl

    s = s_ref[...]
    for c in range(n_chunks):
        sel = slice(c * pairs, (c + 1) * pairs)
        s_bf = _bf(s)
        ys = _bdot_nt(rbar[sel], s_bf) + y0[sel]
        yc_ = ys[:, :chunk, :] + ys[:, chunk:, :]
        for hp in range(pairs):
            y_ref[c * chunk:(c + 1) * chunk, hp * pw:(hp + 1) * pw] = yc_[hp]
        s = _bdot(s_bf, pmat[sel]) + dmat[sel]
    s_ref[...] = s

    y = y_ref[...]
    inv_hd = 1.0 / hd
    mean = _split_dot(y, bd, 2) * inv_hd
    yc = y - mean
    var = _split_dot(yc * yc, bd, 2) * inv_hd
    yn = yc * lax.rsqrt(var + LNX_EPS) * lnw_ref[...] + lnb_ref[...]
    o_ref[0] = (yn + bonus) * gate


def _split_dot_lhs(m01, x, terms):
    acc = None
    rem = x
    for _ in range(terms):
        piece = _bf(rem)
        part = _dot(m01, piece)
        acc = part if acc is None else acc + part
        rem = rem - piece.astype(jnp.float32)
    return acc


def rwkv_mixer(p, mu, w0, w2, a0, a2, g2, k_k, k_a, r_k, ln_w, ln_b):
    bsz, seq, n_p = p.shape
    width = w0.shape[-1]
    pairs = width // (2 * RWKV_HEAD_DIM)
    chunk = min(RWKV_CHUNK, seq)
    tile = min(RWKV_TILE, seq)
    hid = jnp.arange(width, dtype=jnp.int32) // RWKV_HEAD_DIM
    bd = (hid[:, None] == hid[None, :]).astype(jnp.bfloat16)
    row = lambda t: t.reshape(1, -1)
    full = lambda a: pl.BlockSpec(a.shape, lambda b, i: (0,) * a.ndim)
    consts = [row(mu), row(w0), _bf(w2), row(a0), _bf(a2), _bf(g2), row(k_k), row(k_a), row(r_k),
              row(ln_w), row(ln_b), bd]
    act = lambda: pltpu.VMEM((tile, width), jnp.bfloat16)
    return pl.pallas_call(
        functools.partial(_rwkv_kernel, width=width, chunk=chunk),
        grid=(bsz, seq // tile),
        in_specs=[pl.BlockSpec((1, tile, n_p), lambda b, i: (b, i, 0))] + [full(a) for a in consts],
        out_specs=pl.BlockSpec((1, tile, width), lambda b, i: (b, i, 0)),
        out_shape=jax.ShapeDtypeStruct((bsz, seq, width), jnp.float32),
        scratch_shapes=[pltpu.VMEM((pairs, 2 * RWKV_HEAD_DIM, 2 * RWKV_HEAD_DIM), jnp.float32),
                        pltpu.VMEM((8, n_p), jnp.float32),
                        act(), act(), act(), act(), act(),
                        pltpu.VMEM((max(8, tile // chunk), width), jnp.float32),
                        pltpu.VMEM((tile, width), jnp.float32)],
        compiler_params=_cparams("arbitrary", "arbitrary"),
        name="rwkv",
    )(p, *consts)


def _s5_discretise(a_re, a_im, dt):
    lam_re = jnp.minimum(a_re, S5_MAX_REAL)
    lam_im = a_im
    log_mag = lam_re * dt
    ang = lam_im * dt
    mag = jnp.exp(log_mag)
    ab_re, ab_im = mag * jnp.cos(ang), mag * jnp.sin(ang)
    den = lam_re * lam_re + lam_im * lam_im
    n_re, n_im = ab_re - 1.0, ab_im
    q_re = (n_re * lam_re + n_im * lam_im) / den
    q_im = (n_im * lam_re - n_re * lam_im) / den
    return log_mag, ang, q_re, q_im


def _s5_power(log_mag, ang, t):
    mag = jnp.exp(t * log_mag)
    return mag * jnp.cos(t * ang), mag * jnp.sin(t * ang)


def _s5ops_kernel(ldt_ref, ar_row, ai_row, ar_col, ai_col, bt_re, bt_im, ct_re, ct_im, til_ref,
                  toep_ref, pm_ref, q_ref, lvl_ref, *, lc):
    n_c, n_p = bt_re.shape[1], bt_re.shape[2]
    lw = lc * n_c
    dt = jnp.exp(ldt_ref[0])

    lm_r, an_r, q_re, q_im = _s5_discretise(ar_row[0], ai_row[0], dt)
    bbt_re = q_re * bt_re[0] - q_im * bt_im[0]
    bbt_im = q_re * bt_im[0] + q_im * bt_re[0]
    s_row = lax.broadcasted_iota(jnp.int32, (lw, n_p), 0) // n_c
    pw_re, pw_im = _s5_power(lm_r, an_r, (lc - 1 - s_row).astype(jnp.float32))
    tb_re = jnp.concatenate([bbt_re] * lc, axis=0)
    tb_im = jnp.concatenate([bbt_im] * lc, axis=0)
    pm_ref[0] = _bf(jnp.concatenate([pw_re * tb_re - pw_im * tb_im,
                                     pw_re * tb_im + pw_im * tb_re], axis=1))

    lm_c, an_c, _, _ = _s5_discretise(ar_col[0], ai_col[0], dt)
    til = til_ref[...]
    c_re = _dot(ct_re[0], til, HI)
    c_im = _dot(ct_im[0], til, HI)
    t_lane = (lax.broadcasted_iota(jnp.int32, (n_p, lw), 1) // n_c).astype(jnp.float32)
    p0_re, p0_im = _s5_power(lm_c, an_c, t_lane)
    p1_re, p1_im = _s5_power(lm_c, an_c, t_lane + 1.0)
    ca0_re, ca0_im = p0_re * c_re - p0_im * c_im, p0_re * c_im + p0_im * c_re
    q_ref[0] = _bf(jnp.concatenate([p1_re * c_re - p1_im * c_im,
                                    -(p1_re * c_im + p1_im * c_re)], axis=0))
    r0 = _dot(bbt_re, ca0_re, HI) - _dot(bbt_im, ca0_im, HI)
    lane = lax.broadcasted_iota(jnp.int32, (n_c, lw), 1)
    for s in range(lc):
        blk = r0 if s == 0 else jnp.where(lane >= s * n_c, pltpu.roll(r0, s * n_c, axis=1), 0.0)
        toep_ref[0, s * n_c:(s + 1) * n_c, :] = _bf(blk)

    cr, ci = _s5_power(lm_r, an_r, float(lc))
    for j in range(lvl_ref.shape[1]):
        lvl_ref[0, j] = jnp.concatenate([jnp.concatenate([cr, cr], axis=1),
                                         jnp.concatenate([-ci, ci], axis=1)], axis=0)
        cr, ci = cr * cr - ci * ci, 2.0 * cr * ci


def s5_operators(log_dt, a_re, a_im, b_re, b_im, c_re, c_im, n_chunks, lc):
    n_g, n_p = a_re.shape
    n_c = b_re.shape[-1]
    lw = lc * n_c
    n_lvl = max(1, int(math.ceil(math.log2(n_chunks))))
    til = (jnp.arange(lw)[None, :] % n_c == jnp.arange(n_c)[:, None]).astype(jnp.float32)
    t3 = lambda a: jnp.swapaxes(a, 1, 2)
    args = [log_dt.reshape(n_g, 1, 1), a_re.reshape(n_g, 1, n_p), a_im.reshape(n_g, 1, n_p),
            a_re.reshape(n_g, n_p, 1), a_im.reshape(n_g, n_p, 1),
            t3(b_re), t3(b_im), t3(c_re), t3(c_im)]
    per_g = lambda a: pl.BlockSpec((1,) + a.shape[1:], lambda g: (g,) + (0,) * (a.ndim - 1))
    return pl.pallas_call(
        functools.partial(_s5ops_kernel, lc=lc),
        grid=(n_g,),
        in_specs=[per_g(a) for a in args] + [pl.BlockSpec(til.shape, lambda g: (0, 0))],
        out_specs=[pl.BlockSpec((1, lw, lw), lambda g: (g, 0, 0)),
                   pl.BlockSpec((1, lw, 2 * n_p), lambda g: (g, 0, 0)),
                   pl.BlockSpec((1, 2 * n_p, lw), lambda g: (g, 0, 0)),
                   pl.BlockSpec((1, n_lvl, 2, 2 * n_p), lambda g: (g, 0, 0, 0))],
        out_shape=[jax.ShapeDtypeStruct((n_g, lw, lw), jnp.bfloat16),
                   jax.ShapeDtypeStruct((n_g, lw, 2 * n_p), jnp.bfloat16),
                   jax.ShapeDtypeStruct((n_g, 2 * n_p, lw), jnp.bfloat16),
                   jax.ShapeDtypeStruct((n_g, n_lvl, 2, 2 * n_p), jnp.float32)],
        compiler_params=_cparams("arbitrary"),
        name="s5ops",
    )(*args, til)


def _s5_kernel(u_ref, toep_ref, pm_ref, q_ref, lvl_ref, y_ref, *, n_chunks):
    u = _bf(u_ref[0])
    e = _dot(u, pm_ref[0])
    rows, two_p = e.shape
    half = two_p // 2
    cidx = lax.broadcasted_iota(jnp.int32, e.shape, 0) % n_chunks
    x = e
    n_lvl = lvl_ref.shape[1]
    for j in range(n_lvl):
        sh = 1 << j
        if sh >= n_chunks:
            break
        xs = jnp.where(cidx >= sh, pltpu.roll(x, sh, axis=0), 0.0)
        cf = lvl_ref[0, j]
        x = x + xs * cf[0:1, :] + pltpu.roll(xs, half, axis=1) * cf[1:2, :]
    x_in = jnp.where(cidx >= 1, pltpu.roll(x, 1, axis=0), 0.0)
    x_hi = _bf(x_in)
    x_lo = _bf(x_in - x_hi.astype(jnp.float32))
    q = q_ref[0]
    y_ref[0] = _dot(u, toep_ref[0]) + _dot(x_hi, q) + _dot(x_lo, q)


def _to_groups_kernel(u_ref, o_ref, *, lc, n_c):
    per, nb, lw = o_ref.shape
    slot = lax.broadcasted_iota(jnp.int32, (nb, LANES), 1) // n_c
    a = [u_ref[pl.ds(s, nb, stride=lc), :] for s in range(lc)]
    for gi in range(per):
        for j in range(lw // LANES):
            acc = None
            for ai in range(per):
                shift = ((ai - gi) % per) * n_c
                src = a[j * per + ai]
                piece = src if shift == 0 else pltpu.roll(src, shift, axis=1)
                acc = piece if acc is None else jnp.where(slot == ai, piece, acc)
            o_ref[gi, :, j * LANES:(j + 1) * LANES] = acc


def _from_groups_kernel(y_ref, o_ref, *, lc, n_c):
    per, nb, lw = y_ref.shape
    slot = lax.broadcasted_iota(jnp.int32, (nb, LANES), 1) // n_c
    for s in range(lc):
        j, ai = divmod(s, per)
        acc = None
        for gi in range(per):
            shift = ((gi - ai) % per) * n_c
            src = y_ref[gi, :, j * LANES:(j + 1) * LANES]
            piece = src if shift == 0 else pltpu.roll(src, shift, axis=1)
            acc = piece if acc is None else jnp.where(slot == gi, piece, acc)
        o_ref[pl.ds(s, nb, stride=lc), :] = acc


def _group_relayout(x, n_g, lc, to_groups, tile):
    if to_groups:
        n_tok, width = x.shape
    else:
        n_tok, width = x.shape[1] * lc, x.shape[2] // lc * n_g
    n_c = width // n_g
    per = LANES // n_c
    nb = tile // lc
    tok_spec = pl.BlockSpec((tile, LANES), lambda i, vb: (i, vb))
    grp_spec = pl.BlockSpec((per, nb, lc * n_c), lambda i, vb: (vb, i, 0))
    kern = _to_groups_kernel if to_groups else _from_groups_kernel
    out_shape = (n_g, n_tok // lc, lc * n_c) if to_groups else (n_tok, width)
    return pl.pallas_call(
        functools.partial(kern, lc=lc, n_c=n_c),
        grid=(n_tok // tile, n_g // per),
        in_specs=[tok_spec if to_groups else grp_spec],
        out_specs=grp_spec if to_groups else tok_spec,
        out_shape=jax.ShapeDtypeStruct(out_shape, x.dtype),
        compiler_params=_cparams("arbitrary", "arbitrary"),
        name="to_groups" if to_groups else "from_groups",
    )(x)


def s5_core(u, log_dt, a_re, a_im, b_re, b_im, c_re, c_im):
    bsz, seq, width = u.shape
    n_g, n_p = a_re.shape
    n_c = width // n_g
    lc = min(S5_CHUNK, seq)
    n_chunks = seq // lc
    toep, pm, q, lvl = s5_operators(log_dt, a_re, a_im, b_re, b_im, c_re, c_im, n_chunks, lc)
    tile = min(2048, bsz * seq)
    ug = _group_relayout(u.reshape(bsz * seq, width), n_g, lc, True, tile)
    rows, lw = bsz * n_chunks, lc * n_c
    yg = pl.pallas_call(
        functools.partial(_s5_kernel, n_chunks=n_chunks),
        grid=(n_g,),
        in_specs=[pl.BlockSpec((1, rows, lw), lambda g: (g, 0, 0)),
                  pl.BlockSpec((1, lw, lw), lambda g: (g, 0, 0)),
                  pl.BlockSpec((1, lw, 2 * n_p), lambda g: (g, 0, 0)),
                  pl.BlockSpec((1, 2 * n_p, lw), lambda g: (g, 0, 0)),
                  pl.BlockSpec((1,) + lvl.shape[1:], lambda g: (g, 0, 0, 0))],
        out_specs=pl.BlockSpec((1, rows, lw), lambda g: (g, 0, 0)),
        out_shape=jax.ShapeDtypeStruct((n_g, rows, lw), jnp.float32),
        compiler_params=_cparams("arbitrary"),
        name="s5",
    )(ug, toep, pm, q, lvl)
    return _group_relayout(yg, n_g, lc, False, tile).reshape(bsz, seq, width)


def _gelu_tanh(y):
    return 0.5 * y * (1.0 + jnp.tanh(math.sqrt(2.0 / math.pi) * (y + 0.044715 * (y * y * y))))


def _outproj_kernel(yr_ref, ys_ref, u_ref, x_ref, mods_ref, d_ref, wglu_ref, bglu_ref, wout_ref,
                    gpost_ref, gpre_ref, wrt_ref, x1_ref, h2_ref, lg_ref):
    m = mods_ref[0]
    yr = yr_ref[...]
    y5 = _gelu_tanh(ys_ref[...] + d_ref[...] * u_ref[...])
    y5 = y5 * _sigmoid(_dot(_bf(y5), wglu_ref[...]) + bglu_ref[...])
    wr = yr.shape[-1]
    mixed = _dot(_bf(yr), wout_ref[0:wr, :]) + _dot(_bf(y5), wout_ref[wr:, :])
    x1 = x_ref[...] + m[2:3, :] * _rms(mixed, gpost_ref[...])
    x1_ref[...] = x1
    h2 = _rms(x1, gpre_ref[...]) * (1.0 + m[4:5, :]) + m[3:4, :]
    half = h2.shape[-1] // 2
    _store_row_tiles(h2_ref, _pack_bf16_pair(h2[:, :half], h2[:, half:]))
    w_hi, w_lo = wrt_ref[0], wrt_ref[1]
    h_hi = _bf(h2)
    h_lo = _bf(h2 - h_hi.astype(jnp.float32))
    lg_ref[...] = _dot_nt(w_hi, h_hi) + (_dot_nt(w_hi, h_lo) + _dot_nt(w_lo, h_hi))


def out_proj(y_rwkv, y_s5, u, x, mods3, s5_d, w_glu_bf, b_glu, w_out_bf, g_post, g_pre, w_router_t,
             seq, tm):
    n_tok, dm = x.shape
    wr, ws = y_rwkv.shape[-1], y_s5.shape[-1]
    n_e = w_router_t.shape[0]
    tpb = seq // tm
    tok = lambda w: pl.BlockSpec((tm, w), lambda i: (i, 0))
    full = lambda a: pl.BlockSpec(a.shape, lambda i: (0,) * a.ndim)
    row = lambda t: t.reshape(1, -1)
    wr_hi = _bf(w_router_t)
    wr_split = jnp.stack([wr_hi, _bf(w_router_t - wr_hi.astype(jnp.float32))])
    consts = [row(s5_d), w_glu_bf, row(b_glu), w_out_bf, row(g_post), row(g_pre), wr_split]
    return pl.pallas_call(
        _outproj_kernel,
        grid=(n_tok // tm,),
        in_specs=[tok(wr), tok(ws), tok(ws), tok(dm),
                  pl.BlockSpec((1, N_MODS, dm), lambda i: (i // tpb, 0, 0))] + [full(a) for a in consts],
        out_specs=[tok(dm), _row_tile_spec(tm, dm // 2, lambda i: i),
                   pl.BlockSpec((n_e, tm), lambda i: (0, i))],
        out_shape=[jax.ShapeDtypeStruct((n_tok, dm), jnp.float32),
                   jax.ShapeDtypeStruct((n_tok * (dm // 2 // LANES), LANES), jnp.uint32),
                   jax.ShapeDtypeStruct((n_e, n_tok), jnp.float32)],
        compiler_params=_cparams("arbitrary"),
        name="outproj",
    )(y_rwkv, y_s5, u, x, mods3, *consts)


def _route_kernel(lg_ref, bias_ref, tri_ref, e_ref, w_ref, r_ref, cnt_ref, carry_ref):
    i = pl.program_id(0)

    @pl.when(i == 0)
    def _():
        carry_ref[...] = jnp.zeros_like(carry_ref)

    neg = -jnp.inf
    scores = _sigmoid(lg_ref[...])
    n_e, tm = scores.shape
    choice = scores + bias_ref[...]
    gsz = n_e // ROUTE_GROUPS
    c3 = choice.reshape(ROUTE_GROUPS, gsz, tm)
    io = lax.broadcasted_iota(jnp.int32, c3.shape, 1)
    m1 = jnp.max(c3, axis=1, keepdims=True)
    first = jnp.min(jnp.where(c3 == m1, io, gsz), axis=1, keepdims=True)
    m2 = jnp.max(jnp.where(io == first, neg, c3), axis=1, keepdims=True)
    gs = m1 + m2
    gi = lax.broadcasted_iota(jnp.int32, gs.shape, 0)
    rank = jnp.zeros(gs.shape, jnp.int32)
    for j in range(ROUTE_GROUPS):
        gj = gs[j:j + 1]
        beats = (gj > gs) | ((gj == gs) & (gi > j))
        rank = rank + beats.astype(jnp.int32)
    masked = jnp.where(rank < ROUTE_TOPK_GROUPS, c3, neg).reshape(n_e, tm)

    eio = lax.broadcasted_iota(jnp.int32, (n_e, tm), 0)
    ids, ws = [], []
    mhot = jnp.zeros((n_e, tm), jnp.float32)
    for _ in range(TOP_K):
        m = jnp.max(masked, axis=0, keepdims=True)
        idx = jnp.min(jnp.where(masked == m, eio, n_e), axis=0, keepdims=True)
        sel = eio == idx
        ws.append(jnp.sum(jnp.where(sel, scores, 0.0), axis=0, keepdims=True))
        ids.append(idx)
        masked = jnp.where(sel, neg, masked)
        mhot = jnp.where(sel, 1.0, mhot)
    wsum = ws[0]
    for t in ws[1:]:
        wsum = wsum + t
    before = _dot(_bf(mhot), tri_ref[...]) + carry_ref[...]
    ranks = [jnp.sum(jnp.where(eio == idx, before, 0.0), axis=0, keepdims=True) for idx in ids]
    e_ref[...] = jnp.concatenate(ids, axis=0)
    w_ref[...] = jnp.concatenate(ws, axis=0) / wsum * ROUTE_SCALE
    r_ref[...] = jnp.concatenate(ranks, axis=0).astype(jnp.int32)
    carry_ref[...] = carry_ref[...] + jnp.sum(mhot, axis=1, keepdims=True)
    cnt_ref[...] = carry_ref[...]


def route(logits_t, router_bias, tm):
    n_e, n_tok = logits_t.shape
    tri = (jnp.arange(tm)[:, None] < jnp.arange(tm)[None, :]).astype(jnp.bfloat16)
    kt = lambda: pl.BlockSpec((TOP_K, tm), lambda i: (0, i))
    return pl.pallas_call(
        _route_kernel,
        grid=(n_tok // tm,),
        in_specs=[pl.BlockSpec((n_e, tm), lambda i: (0, i)),
                  pl.BlockSpec((n_e, 1), lambda i: (0, 0)),
                  pl.BlockSpec((tm, tm), lambda i: (0, 0))],
        out_specs=[kt(), kt(), kt(), pl.BlockSpec((n_e, 1), lambda i: (0, 0))],
        out_shape=[jax.ShapeDtypeStruct((TOP_K, n_tok), jnp.int32),
                   jax.ShapeDtypeStruct((TOP_K, n_tok), jnp.float32),
                   jax.ShapeDtypeStruct((TOP_K, n_tok), jnp.int32),
                   jax.ShapeDtypeStruct((n_e, 1), jnp.float32)],
        scratch_shapes=[pltpu.VMEM((n_e, 1), jnp.float32)],
        compiler_params=_cparams("arbitrary"),
        name="route",
    )(logits_t, router_bias.reshape(n_e, 1), tri)


def _dest_kernel(e_ref, r_ref, ps_ref, d_ref):
    n_e = ps_ref.shape[0]
    tm = e_ref.shape[1]
    eio = lax.broadcasted_iota(jnp.int32, (n_e, tm), 0)
    ps = ps_ref[...]
    rows = [jnp.sum(jnp.where(eio == e_ref[k:k + 1, :], ps, 0.0), axis=0, keepdims=True)
            for k in range(TOP_K)]
    d_ref[...] = jnp.concatenate(rows, axis=0).astype(jnp.int32) + r_ref[...]


def dest_rows(e_idx, rank, pad_start, tm):
    n_tok = e_idx.shape[1]
    n_e = pad_start.shape[0]
    kt = lambda: pl.BlockSpec((TOP_K, tm), lambda i: (0, i))
    return pl.pallas_call(
        _dest_kernel,
        grid=(n_tok // tm,),
        in_specs=[kt(), kt(), pl.BlockSpec((n_e, 1), lambda i: (0, 0))],
        out_specs=kt(),
        out_shape=jax.ShapeDtypeStruct((TOP_K, n_tok), jnp.int32),
        compiler_params=_cparams("arbitrary"),
        name="dest",
    )(e_idx, rank, pad_start.astype(jnp.float32).reshape(n_e, 1))


def _dispatch_kernel(fill_start_ref, fill_len_ref, nu_ref, dest_ref, h_ref, h_hbm, xs_ref, zeros, sem,
                     zsem, *, n_s, bm, experts_per_step, tiles_per_step, n_tiles):
    i = pl.program_id(0)
    td = dest_ref.shape[1]

    @pl.when(i == 0)
    def _():
        zeros[...] = jnp.zeros_like(zeros)

    def body(t, carry):
        for k in range(TOP_K):
            src = _row(h_ref, t, n_s) if k < TOP_K // 2 else _row(h_hbm, i * td + t, n_s)
            pltpu.make_async_copy(src, _row(xs_ref, dest_ref[k, t], n_s), sem).start(priority=k % 2)
        return carry

    lax.fori_loop(0, td, body, 0)

    for j in range(experts_per_step):
        e = i * experts_per_step + j
        start, length = fill_start_ref[e], fill_len_ref[e]
        piece = bm // 2
        while piece >= 1:
            @pl.when((length & piece) != 0)
            def _(piece=piece):
                off = start + (length & ~(2 * piece - 1))
                cp = pltpu.make_async_copy(zeros.at[pl.ds(0, piece * n_s), :],
                                           xs_ref.at[pl.ds(off * n_s, piece * n_s), :], zsem)
                cp.start()
                cp.wait()
            piece //= 2

    for j in range(tiles_per_step):
        tile = i * tiles_per_step + j

        @pl.when((tile >= nu_ref[0]) & (tile < n_tiles))
        def _(tile=tile):
            cp = pltpu.make_async_copy(zeros, xs_ref.at[pl.ds(tile * (bm * n_s), bm * n_s), :], zsem)
            cp.start()
            cp.wait()

    for _ in range(TOP_K):
        pltpu.make_async_copy(h_ref, xs_ref.at[pl.ds(0, td * n_s), :], sem).wait()


def dispatch(dest, h2p, fill_start, fill_len, n_used, n_tok, n_tiles, td, bm):
    n_s = h2p.shape[0] // n_tok
    n_steps = n_tok // td
    n_e = fill_start.shape[0]
    experts_per_step = -(-n_e // n_steps)
    tiles_per_step = -(-n_tiles // n_steps)
    pad = n_steps * experts_per_step - n_e
    fill_start = jnp.pad(fill_start, (0, pad))
    fill_len = jnp.pad(fill_len, (0, pad))
    return pl.pallas_call(
        functools.partial(_dispatch_kernel, n_s=n_s, bm=bm, experts_per_step=experts_per_step,
                          tiles_per_step=tiles_per_step, n_tiles=n_tiles),
        grid_spec=pltpu.PrefetchScalarGridSpec(
            num_scalar_prefetch=3,
            grid=(n_steps,),
            in_specs=[pl.BlockSpec((TOP_K, td), lambda i, *_: (0, i), memory_space=pltpu.SMEM),
                      _row_tile_spec(td, n_s * LANES, lambda i, *_: i),
                      pl.BlockSpec(memory_space=pl.ANY)],
            out_specs=pl.BlockSpec(memory_space=pl.ANY),
            scratch_shapes=[pltpu.VMEM((bm * n_s, LANES), h2p.dtype),
                            pltpu.SemaphoreType.DMA(()), pltpu.SemaphoreType.DMA(())]),
        out_shape=jax.ShapeDtypeStruct((n_tiles * bm * n_s, LANES), h2p.dtype),
        compiler_params=_cparams("arbitrary"),
        name="dispatch",
    )(fill_start, fill_len, n_used, dest, h2p, h2p)


def _expert_kernel(te_ref, nu_ref, nt_ref, x_ref, wg_hbm, wu_hbm, wd_hbm, o_ref,
                   wg_st, wu_st, wd_st, wgu_bf, wd_bf, grp, sems, *, bm):
    i = pl.program_id(0)
    ff = wd_bf.shape[0]
    n_used = nu_ref[0]

    def fetch(e, slot):
        return (pltpu.make_async_copy(wg_hbm.at[e], wg_st.at[slot], sems.at[slot]),
                pltpu.make_async_copy(wu_hbm.at[e], wu_st.at[slot], sems.at[slot]),
                pltpu.make_async_copy(wd_hbm.at[e], wd_st.at[slot], sems.at[slot]))

    @pl.when(i == 0)
    def _():
        grp[0] = 0
        for cp in fetch(te_ref[0], 0):
            cp.start()

    @pl.when(i < n_used)
    def _():
        e = te_ref[i]

        @pl.when((i == 0) | (e != te_ref[jnp.maximum(i - 1, 0)]))
        def _():
            slot = grp[0] & 1
            grp[0] = grp[0] + 1
            for cp in fetch(e, slot):
                cp.wait()
            nxt = i + nt_ref[e]

            @pl.when(nxt < n_used)
            def _():
                for cp in fetch(te_ref[jnp.minimum(nxt, n_used - 1)], 1 - slot):
                    cp.start()

            wgu_bf[:, :ff] = _bf(wg_st[slot])
            wgu_bf[:, ff:] = _bf(wu_st[slot])
            wd_bf[...] = _bf(wd_st[slot])

        hi, lo = _unpack_bf16_pair(_load_row_tiles(x_ref, bm))
        x = jnp.concatenate([hi, lo], axis=1)
        gu = _dot(x, wgu_bf[...])
        hid = _silu(gu[:, :ff]) * gu[:, ff:]
        y = _dot(_bf(hid), wd_bf[...])
        half = y.shape[1] // 2
        _store_row_tiles(o_ref, _pack_bf16_pair(y[:, :half], y[:, half:]))

    @pl.when(i >= nu_ref[0])
    def _():
        o_ref[...] = jnp.zeros_like(o_ref)


def expert_ffn(tile_e, n_used, tiles_per_expert, xs, n_rows, w_gate, w_up, w_down, bm):
    dm = w_gate.shape[1]
    half = dm // 2
    ff = w_gate.shape[-1]
    rows_in = _row_tile_spec(bm, half, lambda i, te, nu, nt: jnp.minimum(i, nu[0] - 1))
    rows_out = _row_tile_spec(bm, half, lambda i, te, nu, nt: i)
    hbm = pl.BlockSpec(memory_space=pl.ANY)
    return pl.pallas_call(
        functools.partial(_expert_kernel, bm=bm),
        grid_spec=pltpu.PrefetchScalarGridSpec(
            num_scalar_prefetch=3,
            grid=(n_rows // bm,),
            in_specs=[rows_in, hbm, hbm, hbm],
            out_specs=rows_out,
            scratch_shapes=[pltpu.VMEM((2, dm, ff), jnp.float32),
                            pltpu.VMEM((2, dm, ff), jnp.float32),
                            pltpu.VMEM((2, ff, dm), jnp.float32),
                            pltpu.VMEM((dm, 2 * ff), jnp.bfloat16),
                            pltpu.VMEM((ff, dm), jnp.bfloat16),
                            pltpu.SMEM((1,), jnp.int32),
                            pltpu.SemaphoreType.DMA((2,))]),
        out_shape=jax.ShapeDtypeStruct((n_rows * (half // LANES), LANES), jnp.uint32),
        compiler_params=_cparams("arbitrary"),
        name="expert",
    )(tile_e, n_used, tiles_per_expert, xs, w_gate, w_up, w_down)


def _combine_kernel(dest_ref, dnext_ref, w_ref, x1_ref, h2_ref, mods_ref, g_ref, wsg_ref, wsu_ref,
                    wsd_ref, ys_ref, o_ref, gbuf, rbuf, wrep, sems, *, n_s):
    i = pl.program_id(0)
    n_steps = pl.num_programs(0)
    tc = x1_ref.shape[0]
    slot = i % 2

    def gather(d_ref, slot_, t):
        for k in range(TOP_K):
            pltpu.make_async_copy(_row(ys_ref, d_ref[k, t], n_s), _row(gbuf.at[slot_, k], t, n_s),
                                  sems.at[slot_]).start(priority=k % 2)

    def weighted(t):
        acc_hi = acc_lo = None
        for k in range(TOP_K):
            wk = jnp.broadcast_to(wrep[k, pl.ds(t, 1), :], (n_s, LANES))
            hi, lo = _unpack_pair_f32(_row(gbuf.at[slot, k], t, n_s)[...])
            t_hi, t_lo = wk * hi, wk * lo
            acc_hi = t_hi if acc_hi is None else acc_hi + t_hi
            acc_lo = t_lo if acc_lo is None else acc_lo + t_lo
        _row(rbuf.at[0], t, n_s)[...] = acc_hi
        _row(rbuf.at[1], t, n_s)[...] = acc_lo

    @pl.when(i == 0)
    def _():
        def body(t, carry):
            gather(dest_ref, 0, t)
            return carry
        lax.fori_loop(0, tc, body, 0)

    w = w_ref[...]
    for k in range(TOP_K):
        wrep[k] = jnp.broadcast_to(w[:, k:k + 1], (tc, LANES))

    for k in range(TOP_K):
        pltpu.make_async_copy(ys_ref.at[pl.ds(0, tc * n_s), :], gbuf.at[slot, k], sems.at[slot]).wait()

    @pl.when(i + 1 < n_steps)
    def _():
        def body(t, carry):
            gather(dnext_ref, 1 - slot, t)
            weighted(t)
            return carry
        lax.fori_loop(0, tc, body, 0)

    @pl.when(i + 1 == n_steps)
    def _():
        def body(t, carry):
            weighted(t)
            return carry
        lax.fori_loop(0, tc, body, 0)

    hi, lo = _unpack_bf16_pair(_load_row_tiles(h2_ref, tc))
    h = jnp.concatenate([hi, lo], axis=1)
    hid = _silu(_dot(h, wsg_ref[...])) * _dot(h, wsu_ref[...])
    routed = jnp.concatenate([_load_row_tiles(rbuf.at[0], tc), _load_row_tiles(rbuf.at[1], tc)], axis=1)
    y = _dot(_bf(hid), wsd_ref[...]) + routed
    m = mods_ref[0]
    o_ref[...] = x1_ref[...] + m[5:6, :] * _rms(y, g_ref[...])


def combine(dest, w_tk, x1, h2p, mods3, g_post, ws_gate_bf, ws_up_bf, ws_down_bf, ys, seq, tc):
    n_tok, dm = x1.shape
    n_s = dm // 2 // LANES
    n_steps = n_tok // tc
    tpb = seq // tc
    full = lambda a: pl.BlockSpec(a.shape, lambda i: (0,) * a.ndim)
    tok = lambda w: pl.BlockSpec((tc, w), lambda i: (i, 0))
    g2 = g_post.reshape(1, dm)
    return pl.pallas_call(
        functools.partial(_combine_kernel, n_s=n_s),
        grid=(n_steps,),
        in_specs=[pl.BlockSpec((TOP_K, tc), lambda i: (0, i), memory_space=pltpu.SMEM),
                  pl.BlockSpec((TOP_K, tc), lambda i: (0, jnp.minimum(i + 1, n_steps - 1)),
                               memory_space=pltpu.SMEM),
                  tok(TOP_K),
                  tok(dm), _row_tile_spec(tc, dm // 2, lambda i: i),
                  pl.BlockSpec((1, N_MODS, dm), lambda i: (i // tpb, 0, 0)),
                  full(g2), full(ws_gate_bf), full(ws_up_bf), full(ws_down_bf),
                  pl.BlockSpec(memory_space=pl.ANY)],
        out_specs=tok(dm),
        out_shape=jax.ShapeDtypeStruct((n_tok, dm), jnp.float32),
        scratch_shapes=[pltpu.VMEM((2, TOP_K, tc * n_s, LANES), jnp.uint32),
                        pltpu.VMEM((2, tc * n_s, LANES), jnp.float32),
                        pltpu.VMEM((TOP_K, tc, LANES), jnp.float32),
                        pltpu.SemaphoreType.DMA((2,))],
        compiler_params=_cparams("arbitrary"),
        name="combine",
    )(dest, dest, w_tk, x1, h2p, mods3, g2, ws_gate_bf, ws_up_bf, ws_down_bf, ys)


def moe_ffn(h2, x1, logits_t, mods3, router_bias, w_gate, w_up, w_down, ws_gate, ws_up, ws_down,
            g_post, seq):
    n_tok, dm = x1.shape
    n_e = w_gate.shape[0]
    bm = EXPERT_ROW_TILE
    e_idx, w_kt, rank, cnt = route(logits_t, router_bias, min(512, n_tok))
    counts = cnt[:, 0].astype(jnp.int32)
    padded = (counts + bm - 1) // bm * bm
    pad_end = jnp.cumsum(padded)
    pad_start = (pad_end - padded).astype(jnp.int32)
    n_tiles = -(-(n_tok * TOP_K) // bm) + n_e
    n_used = (pad_end[-1] // bm).astype(jnp.int32)
    tile_start = jnp.arange(n_tiles, dtype=jnp.int32) * bm
    tile_e = jnp.minimum(jnp.sum((pad_end[None, :] <= tile_start[:, None]).astype(jnp.int32), axis=1),
                         n_e - 1).astype(jnp.int32)
    tile_e = jnp.where(jnp.arange(n_tiles) < n_used, tile_e, tile_e[n_used - 1])
    dest = dest_rows(e_idx, rank, pad_start, min(512, n_tok))
    n_used1 = n_used.reshape(1)
    xs = dispatch(dest, h2, pad_start + counts, padded - counts, n_used1, n_tok, n_tiles,
                  min(256, n_tok), bm)
    ys = expert_ffn(tile_e, n_used1, (padded // bm).astype(jnp.int32), xs, n_tiles * bm,
                    w_gate, w_up, w_down, bm)
    bf = lambda a: a.astype(jnp.bfloat16)
    return combine(dest, w_kt.T, x1, h2, mods3, g_post, bf(ws_gate), bf(ws_up), bf(ws_down), ys,
                   seq, min(256, n_tok))


def kernel(x, c, w_ada, b_ada, g_pre_mix, g_post_mix, g_pre_ffn, g_post_ffn, w_in, mu_shift, rwkv_w0, rwkv_w2, rwkv_a0, rwkv_a2, rwkv_g2, rwkv_k_k, rwkv_k_a, rwkv_r_k, rwkv_ln_w, rwkv_ln_b, s5_log_dt, s5_a_re, s5_a_im, s5_b_re, s5_b_im, s5_c_re, s5_c_im, s5_d, s5_w_glu, s5_b_glu, w_out, w_router, router_bias, w_gate, w_up, w_down, ws_gate, ws_up, ws_down):
    bsz, seq, dm = x.shape
    depth = w_ada.shape[0]
    bf = lambda a: a.astype(jnp.bfloat16)
    tm = min(512, seq)
    for l in range(depth):
        mods3 = ada_mods(c, w_ada[l], b_ada[l]).reshape(bsz, N_MODS, dm)
        n_rwkv = mu_shift.shape[-1]
        p, u = in_proj(x, mods3, g_pre_mix[l], bf(w_in[l]), n_rwkv, tm)
        y_rwkv = rwkv_mixer(p, mu_shift[l], rwkv_w0[l], rwkv_w2[l], rwkv_a0[l], rwkv_a2[l],
                            rwkv_g2[l], rwkv_k_k[l], rwkv_k_a[l], rwkv_r_k[l].reshape(-1),
                            rwkv_ln_w[l], rwkv_ln_b[l])
        y_s5 = s5_core(u, s5_log_dt[l], s5_a_re[l], s5_a_im[l], s5_b_re[l], s5_b_im[l],
                       s5_c_re[l], s5_c_im[l])
        flat = lambda a: a.reshape(bsz * seq, a.shape[-1])
        x1, h2, logits_t = out_proj(flat(y_rwkv), flat(y_s5), flat(u), flat(x), mods3, s5_d[l],
                                    bf(s5_w_glu[l]), s5_b_glu[l], bf(w_out[l]), g_post_mix[l],
                                    g_pre_ffn[l], w_router[l].T, seq, tm)
        out = moe_ffn(h2, x1, logits_t, mods3, router_bias[l], w_gate[l], w_up[l], w_down[l],
                      ws_gate[l], ws_up[l], ws_down[l], g_post_ffn[l], seq)
        x = out.reshape(bsz, seq, dm)
    return x
```

```python
import functools
import math

import jax
import jax.numpy as jnp
from jax import lax
from jax.experimental import pallas as pl
from jax.experimental.pallas import tpu as pltpu

NORM_EPS = 1e-6
LNX_EPS = 64e-5
L2_EPS = 1e-12
S5_MAX_REAL = -1e-4
ROUTE_SCALE = 2.5
N_MODS = 6

RWKV_HEAD_DIM = 64
DECAY_LORA = 64
ICLR_LORA = 64
GATE_LORA = 128
S5_GROUP = 16
S5_STATE = 64
TOP_K = 8
ROUTE_GROUPS = 8
ROUTE_TOPK_GROUPS = 4

RWKV_CHUNK = 64
S5_CHUNK = 32
EXPERT_ROW_TILE = 512

VMEM_LIMIT = 56 * 1024 * 1024

HI = lax.Precision.HIGHEST


def _cparams(*sem):
    return pltpu.CompilerParams(dimension_semantics=sem, vmem_limit_bytes=VMEM_LIMIT)


def _dot(a, b, precision=None):
    return jnp.dot(a, b, preferred_element_type=jnp.float32, precision=precision)


def _dot_nt(a, b, precision=None):
    return lax.dot_general(a, b, (((1,), (1,)), ((), ())),
                           preferred_element_type=jnp.float32, precision=precision)


def _dot_tn(a, b, precision=None):
    return lax.dot_general(a, b, (((0,), (0,)), ((), ())),
                           preferred_element_type=jnp.float32, precision=precision)


def _bdot(a, b):
    return lax.dot_general(a, b, (((2,), (1,)), ((0,), (0,))), preferred_element_type=jnp.float32)


def _bdot_nt(a, b):
    return lax.dot_general(a, b, (((2,), (2,)), ((0,), (0,))), preferred_element_type=jnp.float32)


def _bf(x):
    return x.astype(jnp.bfloat16)


def _pack_bf16_pair(hi, lo):
    hb = lax.bitcast_convert_type(_bf(hi).astype(jnp.float32), jnp.uint32)
    lb = lax.bitcast_convert_type(_bf(lo).astype(jnp.float32), jnp.uint32)
    return (hb & jnp.uint32(0xFFFF0000)) | (lb >> 16)


def _unpack_pair_f32(w):
    hi = lax.bitcast_convert_type(w & jnp.uint32(0xFFFF0000), jnp.float32)
    lo = lax.bitcast_convert_type(w << 16, jnp.float32)
    return hi, lo


def _unpack_bf16_pair(w):
    hi, lo = _unpack_pair_f32(w)
    return _bf(hi), _bf(lo)


LANES = 128


def _row_tile_spec(rows, width, row_block):
    return pl.BlockSpec((rows * (width // LANES), LANES), lambda *a: (row_block(*a), 0))


def _store_row_tiles(ref, x):
    rows = x.shape[0]
    n_s = ref.shape[0] // rows
    for s in range(n_s):
        ref[pl.ds(s, rows, stride=n_s), :] = x[:, s * LANES:(s + 1) * LANES]


def _load_row_tiles(ref, rows):
    n_s = ref.shape[0] // rows
    return jnp.concatenate([ref[pl.ds(s, rows, stride=n_s), :] for s in range(n_s)], axis=1)


def _row(ref, r, n_s):
    return ref.at[pl.ds(pl.multiple_of(r * n_s, n_s), n_s), :]


def _sigmoid(x):
    return 1.0 / (1.0 + jnp.exp(-x))


def _silu(x):
    return x * _sigmoid(x)


def _rms(x, gain):
    return x * lax.rsqrt(jnp.mean(x * x, axis=-1, keepdims=True) + NORM_EPS) * gain


def _ada_kernel(c_ref, w_ref, b_ref, o_ref):
    c = c_ref[...]
    o_ref[...] = _dot(_silu(c), w_ref[...], HI) + b_ref[...]


def ada_mods(c, w_ada, b_ada):
    bsz, dm = c.shape
    n = w_ada.shape[1]
    tn = dm
    return pl.pallas_call(
        _ada_kernel,
        grid=(n // tn,),
        in_specs=[pl.BlockSpec((bsz, dm), lambda j: (0, 0)),
                  pl.BlockSpec((dm, tn), lambda j: (0, j)),
                  pl.BlockSpec((1, tn), lambda j: (0, j))],
        out_specs=pl.BlockSpec((bsz, tn), lambda j: (0, j)),
        out_shape=jax.ShapeDtypeStruct((bsz, n), jnp.float32),
        compiler_params=_cparams("arbitrary"),
        name="ada",
    )(c, w_ada, b_ada.reshape(1, n))


def _inproj_kernel(x_ref, mods_ref, g_ref, w_ref, p_ref, u_ref):
    x = x_ref[0]
    m = mods_ref[0]
    h = _rms(x, g_ref[...]) * (1.0 + m[1:2, :]) + m[0:1, :]
    proj = _dot(_bf(h), w_ref[...])
    n_p = p_ref.shape[-1]
    p_ref[0] = proj[:, :n_p]
    u_ref[0] = proj[:, n_p:]


def in_proj(x, mods3, g_pre, w_in_bf, n_rwkv, tm):
    bsz, seq, dm = x.shape
    n = w_in_bf.shape[1]
    n_s5 = n - n_rwkv
    return pl.pallas_call(
        _inproj_kernel,
        grid=(bsz, seq // tm),
        in_specs=[pl.BlockSpec((1, tm, dm), lambda b, i: (b, i, 0)),
                  pl.BlockSpec((1, N_MODS, dm), lambda b, i: (b, 0, 0)),
                  pl.BlockSpec((1, dm), lambda b, i: (0, 0)),
                  pl.BlockSpec((dm, n), lambda b, i: (0, 0))],
        out_specs=[pl.BlockSpec((1, tm, n_rwkv), lambda b, i: (b, i, 0)),
                   pl.BlockSpec((1, tm, n_s5), lambda b, i: (b, i, 0))],
        out_shape=[jax.ShapeDtypeStruct((bsz, seq, n_rwkv), jnp.float32),
                   jax.ShapeDtypeStruct((bsz, seq, n_s5), jnp.float32)],
        compiler_params=_cparams("arbitrary", "arbitrary"),
        name="inproj",
    )(x, mods3, g_pre.reshape(1, dm), w_in_bf)


RWKV_TILE = 256


def _split_dot(x, m01, terms):
    blk = m01.shape[0]
    pieces = []
    rem = x
    for _ in range(terms):
        piece = _bf(rem)
        pieces.append(piece)
        rem = rem - piece.astype(jnp.float32)
    cols = []
    for c in range(x.shape[1] // blk):
        acc = None
        for piece in pieces:
            part = _dot(piece[:, c * blk:(c + 1) * blk], m01)
            acc = part if acc is None else acc + part
        cols.append(acc)
    return cols[0] if len(cols) == 1 else jnp.concatenate(cols, axis=1)


def _rwkv_kernel(p_ref, mu_ref, w0_ref, w2_ref, a0_ref, a2_ref, g2_ref, kk_ref, ka_ref,
                  rk_ref, lnw_ref, lnb_ref, bd_ref, o_ref,
                  s_ref, carry_ref, rt_ref, at_ref, bt_ref, kt_ref, v_ref, wl_ref, y_ref,
                  *, width, chunk):
    hd = RWKV_HEAD_DIM
    pw = 2 * hd
    pairs = width // pw
    i = pl.program_id(1)

    @pl.when(i == 0)
    def _():
        s_ref[...] = jnp.zeros_like(s_ref)
        carry_ref[...] = jnp.zeros_like(carry_ref)

    p = p_ref[0]
    n_t = p.shape[0]
    n_chunks = n_t // chunk
    row = lax.broadcasted_iota(jnp.int32, p.shape, 0)
    prev = jnp.where(row == 0, carry_ref[0:1, :], pltpu.roll(p, 1, axis=0))
    carry_ref[0:1, :] = p[n_t - 1:n_t, :]
    pm = p + (prev - p) * mu_ref[...]

    r = pm[:, 0:width]
    k = pm[:, width:2 * width]
    v = pm[:, 2 * width:3 * width]
    c0 = 3 * width
    w_lo = pm[:, c0:c0 + DECAY_LORA]
    a_lo = pm[:, c0 + DECAY_LORA:c0 + DECAY_LORA + ICLR_LORA]
    g_lo = pm[:, c0 + DECAY_LORA + ICLR_LORA:]

    z = w0_ref[...] + _dot(_bf(jnp.tanh(w_lo)), w2_ref[...])
    softplus_neg = jnp.maximum(-z, 0.0) + jnp.log(1.0 + jnp.exp(-jnp.abs(z)))
    logd = -jnp.exp(-softplus_neg - 0.5)
    iclr = _sigmoid(a0_ref[...] + _dot(_bf(a_lo), a2_ref[...]))
    gate = _dot(_bf(_sigmoid(g_lo)), g2_ref[...])

    bd = bd_ref[...]
    kk = k * kk_ref[...]
    kk = kk / jnp.maximum(jnp.sqrt(_split_dot(kk * kk, bd, 2)), L2_EPS)
    k2 = k * (1.0 + (iclr - 1.0) * ka_ref[...])
    bonus = _split_dot(r * k2 * rk_ref[...], bd, 2) * v

    ti = lax.broadcasted_iota(jnp.int32, (n_t, n_t), 0)
    si = lax.broadcasted_iota(jnp.int32, (n_t, n_t), 1)
    tri = jnp.where((ti >= si) & (ti // chunk == si // chunk), 1.0, 0.0).astype(jnp.bfloat16)
    cum = _split_dot_lhs(tri, logd, 3)
    e_pos = jnp.exp(cum)
    e_neg = jnp.exp(-cum)
    rt_ref[...] = _bf(r * e_pos)
    at_ref[...] = _bf(-kk * jnp.exp(cum - logd))
    bt_ref[...] = _bf(kk * iclr * e_neg)
    kt_ref[...] = _bf(k2 * e_neg)
    v_ref[...] = _bf(v)
    for c in range(n_chunks):
        wl_ref[c:c + 1, :] = e_pos[(c + 1) * chunk - 1:(c + 1) * chunk, :]

    two_l = 2 * chunk
    lane = lax.broadcasted_iota(jnp.int32, (chunk, pw), 1)
    lane0 = lane < hd
    bi = lax.broadcasted_iota(jnp.int32, (two_l, two_l), 0)
    bj = lax.broadcasted_iota(jnp.int32, (two_l, two_l), 1)
    same = (bi // chunk) == (bj // chunk)
    low_strict = same & (bi > bj)
    low_incl = same & (bi >= bj)
    eye_t = jnp.where(bi == bj, 1.0, 0.0)
    pi_ = lax.broadcasted_iota(jnp.int32, (pw, pw), 0)
    pj_ = lax.broadcasted_iota(jnp.int32, (pw, pw), 1)
    eye_p = jnp.where(pi_ == pj_, 1.0, 0.0)
    n_sq = max(1, int(math.ceil(math.log2(chunk))) - 1)
    zero = jnp.zeros((), jnp.bfloat16)

    def stack(x):
        return jnp.concatenate([jnp.where(lane0, x, zero), jnp.where(lane0, zero, x)], axis=0)

    def gather(ref):
        return jnp.stack([stack(ref[c * chunk:(c + 1) * chunk, hp * pw:(hp + 1) * pw])
                          for c in range(n_chunks) for hp in range(pairs)], axis=0)

    a_s, b_s, k_s, r_s, v_s = (gather(ref) for ref in (at_ref, bt_ref, kt_ref, rt_ref, v_ref))
    wl = jnp.stack([wl_ref[c:c + 1, hp * pw:(hp + 1) * pw]
                    for c in range(n_chunks) for hp in range(pairs)], axis=0)
    gram = _bdot_nt(jnp.concatenate([a_s, r_s], axis=1), jnp.concatenate([b_s, k_s], axis=1))
    m_ab = jnp.where(low_strict, gram[:, :two_l, :two_l], 0.0)
    m_ak = jnp.where(low_strict, gram[:, :two_l, two_l:], 0.0)
    n_rb = jnp.where(low_incl, gram[:, two_l:, :two_l], 0.0)
    n_rk = jnp.where(low_incl, gram[:, two_l:, two_l:], 0.0)
    t_inv = eye_t + m_ab
    m_pow = _bf(m_ab)
    for _ in range(n_sq):
        m_pow = _bf(_bdot(m_pow, m_pow))
        t_inv = t_inv + _bdot(_bf(t_inv), m_pow)
    makv = _bdot(_bf(m_ak), v_s)
    tx_bf = _bf(_bdot(_bf(t_inv), jnp.concatenate([a_s, _bf(makv)], axis=2)))
    nx = _bdot(_bf(n_rb), tx_bf)
    rbar = _bf(r_s.astype(jnp.float32) + nx[:, :, :pw])
    y0 = nx[:, :, pw:] + _bdot(_bf(n_rk), v_s)
    tb = _bdot(jnp.swapaxes(tx_bf, 1, 2), b_s)
    pmat = _bf((eye_p + tb[:, :pw, :]) * wl)
    dmat = (tb[:, pw:, :] + _bdot(jnp.swapaxes(v_s, 1, 2), k_s)) * wl

    s = s_ref[...]
    for c in range(n_chunks):
        sel = slice(c * pairs, (c + 1) * pairs)
        s_bf = _bf(s)
        ys = _bdot_nt(rbar[sel], s_bf) + y0[sel]
        yc_ = ys[:, :chunk, :] + ys[:, chunk:, :]
        for hp in range(pairs):
            y_ref[c * chunk:(c + 1) * chunk, hp * pw:(hp + 1) * pw] = yc_[hp]
        s = _bdot(s_bf, pmat[sel]) + dmat[sel]
    s_ref[...] = s

    y = y_ref[...]
    inv_hd = 1.0 / hd
    mean = _split_dot(y, bd, 2) * inv_hd
    yc = y - mean
    var = _split_dot(yc * yc, bd, 2) * inv_hd
    yn = yc * lax.rsqrt(var + LNX_EPS) * lnw_ref[...] + lnb_ref[...]
    o_ref[0] = (yn + bonus) * gate


def _split_dot_lhs(m01, x, terms):
    acc = None
    rem = x
    for _ in range(terms):
        piece = _bf(rem)
        part = _dot(m01, piece)
        acc = part if acc is None else acc + part
        rem = rem - piece.astype(jnp.float32)
    return acc


def rwkv_mixer(p, mu, w0, w2, a0, a2, g2, k_k, k_a, r_k, ln_w, ln_b):
    bsz, seq, n_p = p.shape
    width = w0.shape[-1]
    pairs = width // (2 * RWKV_HEAD_DIM)
    chunk = min(RWKV_CHUNK, seq)
    tile = min(RWKV_TILE, seq)
    hid = jnp.arange(2 * RWKV_HEAD_DIM, dtype=jnp.int32) // RWKV_HEAD_DIM
    bd = (hid[:, None] == hid[None, :]).astype(jnp.bfloat16)
    row = lambda t: t.reshape(1, -1)
    full = lambda a: pl.BlockSpec(a.shape, lambda b, i: (0,) * a.ndim)
    consts = [row(mu), row(w0), _bf(w2), row(a0), _bf(a2), _bf(g2), row(k_k), row(k_a), row(r_k),
              row(ln_w), row(ln_b), bd]
    act = lambda: pltpu.VMEM((tile, width), jnp.bfloat16)
    return pl.pallas_call(
        functools.partial(_rwkv_kernel, width=width, chunk=chunk),
        grid=(bsz, seq // tile),
        in_specs=[pl.BlockSpec((1, tile, n_p), lambda b, i: (b, i, 0))] + [full(a) for a in consts],
        out_specs=pl.BlockSpec((1, tile, width), lambda b, i: (b, i, 0)),
        out_shape=jax.ShapeDtypeStruct((bsz, seq, width), jnp.float32),
        scratch_shapes=[pltpu.VMEM((pairs, 2 * RWKV_HEAD_DIM, 2 * RWKV_HEAD_DIM), jnp.float32),
                        pltpu.VMEM((8, n_p), jnp.float32),
                        act(), act(), act(), act(), act(),
                        pltpu.VMEM((max(8, tile // chunk), width), jnp.float32),
                        pltpu.VMEM((tile, width), jnp.float32)],
        compiler_params=_cparams("arbitrary", "arbitrary"),
        name="rwkv",
    )(p, *consts)


def _s5_discretise(a_re, a_im, dt):
    lam_re = jnp.minimum(a_re, S5_MAX_REAL)
    lam_im = a_im
    log_mag = lam_re * dt
    ang = lam_im * dt
    mag = jnp.exp(log_mag)
    ab_re, ab_im = mag * jnp.cos(ang), mag * jnp.sin(ang)
    den = lam_re * lam_re + lam_im * lam_im
    n_re, n_im = ab_re - 1.0, ab_im
    q_re = (n_re * lam_re + n_im * lam_im) / den
    q_im = (n_im * lam_re - n_re * lam_im) / den
    return log_mag, ang, q_re, q_im


def _s5_power(log_mag, ang, t):
    mag = jnp.exp(t * log_mag)
    return mag * jnp.cos(t * ang), mag * jnp.sin(t * ang)


def _s5ops_kernel(ldt_ref, ar_row, ai_row, ar_col, ai_col, bt_re, bt_im, ct_re, ct_im, til_ref,
                  toep_ref, pm_ref, q_ref, lvl_ref, *, lc):
    n_c, n_p = bt_re.shape[1], bt_re.shape[2]
    lw = lc * n_c
    dt = jnp.exp(ldt_ref[0])

    lm_r, an_r, q_re, q_im = _s5_discretise(ar_row[0], ai_row[0], dt)
    bbt_re = q_re * bt_re[0] - q_im * bt_im[0]
    bbt_im = q_re * bt_im[0] + q_im * bt_re[0]
    s_row = lax.broadcasted_iota(jnp.int32, (lw, n_p), 0) // n_c
    pw_re, pw_im = _s5_power(lm_r, an_r, (lc - 1 - s_row).astype(jnp.float32))
    tb_re = jnp.concatenate([bbt_re] * lc, axis=0)
    tb_im = jnp.concatenate([bbt_im] * lc, axis=0)
    pm_ref[0] = _bf(jnp.concatenate([pw_re * tb_re - pw_im * tb_im,
                                     pw_re * tb_im + pw_im * tb_re], axis=1))

    lm_c, an_c, _, _ = _s5_discretise(ar_col[0], ai_col[0], dt)
    til = til_ref[...]
    c_re = _dot(ct_re[0], til, HI)
    c_im = _dot(ct_im[0], til, HI)
    t_lane = (lax.broadcasted_iota(jnp.int32, (n_p, lw), 1) // n_c).astype(jnp.float32)
    p0_re, p0_im = _s5_power(lm_c, an_c, t_lane)
    p1_re, p1_im = _s5_power(lm_c, an_c, t_lane + 1.0)
    ca0_re, ca0_im = p0_re * c_re - p0_im * c_im, p0_re * c_im + p0_im * c_re
    q_ref[0] = _bf(jnp.concatenate([p1_re * c_re - p1_im * c_im,
                                    -(p1_re * c_im + p1_im * c_re)], axis=0))
    r0 = _dot(bbt_re, ca0_re, HI) - _dot(bbt_im, ca0_im, HI)
    lane = lax.broadcasted_iota(jnp.int32, (n_c, lw), 1)
    for s in range(lc):
        blk = r0 if s == 0 else jnp.where(lane >= s * n_c, pltpu.roll(r0, s * n_c, axis=1), 0.0)
        toep_ref[0, s * n_c:(s + 1) * n_c, :] = _bf(blk)

    cr, ci = _s5_power(lm_r, an_r, float(lc))
    for j in range(lvl_ref.shape[1]):
        lvl_ref[0, j] = jnp.concatenate([jnp.concatenate([cr, cr], axis=1),
                                         jnp.concatenate([-ci, ci], axis=1)], axis=0)
        cr, ci = cr * cr - ci * ci, 2.0 * cr * ci


def s5_operators(log_dt, a_re, a_im, b_re, b_im, c_re, c_im, n_chunks, lc):
    n_g, n_p = a_re.shape
    n_c = b_re.shape[-1]
    lw = lc * n_c
    n_lvl = max(1, int(math.ceil(math.log2(n_chunks))))
    til = (jnp.arange(lw)[None, :] % n_c == jnp.arange(n_c)[:, None]).astype(jnp.float32)
    t3 = lambda a: jnp.swapaxes(a, 1, 2)
    args = [log_dt.reshape(n_g, 1, 1), a_re.reshape(n_g, 1, n_p), a_im.reshape(n_g, 1, n_p),
            a_re.reshape(n_g, n_p, 1), a_im.reshape(n_g, n_p, 1),
            t3(b_re), t3(b_im), t3(c_re), t3(c_im)]
    per_g = lambda a: pl.BlockSpec((1,) + a.shape[1:], lambda g: (g,) + (0,) * (a.ndim - 1))
    return pl.pallas_call(
        functools.partial(_s5ops_kernel, lc=lc),
        grid=(n_g,),
        in_specs=[per_g(a) for a in args] + [pl.BlockSpec(til.shape, lambda g: (0, 0))],
        out_specs=[pl.BlockSpec((1, lw, lw), lambda g: (g, 0, 0)),
                   pl.BlockSpec((1, lw, 2 * n_p), lambda g: (g, 0, 0)),
                   pl.BlockSpec((1, 2 * n_p, lw), lambda g: (g, 0, 0)),
                   pl.BlockSpec((1, n_lvl, 2, 2 * n_p), lambda g: (g, 0, 0, 0))],
        out_shape=[jax.ShapeDtypeStruct((n_g, lw, lw), jnp.bfloat16),
                   jax.ShapeDtypeStruct((n_g, lw, 2 * n_p), jnp.bfloat16),
                   jax.ShapeDtypeStruct((n_g, 2 * n_p, lw), jnp.bfloat16),
                   jax.ShapeDtypeStruct((n_g, n_lvl, 2, 2 * n_p), jnp.float32)],
        compiler_params=_cparams("arbitrary"),
        name="s5ops",
    )(*args, til)


def _s5_kernel(u_ref, toep_ref, pm_ref, q_ref, lvl_ref, y_ref, *, n_chunks):
    u = _bf(u_ref[0])
    e = _dot(u, pm_ref[0])
    rows, two_p = e.shape
    half = two_p // 2
    cidx = lax.broadcasted_iota(jnp.int32, e.shape, 0) % n_chunks
    x = e
    n_lvl = lvl_ref.shape[1]
    for j in range(n_lvl):
        sh = 1 << j
        if sh >= n_chunks:
            break
        xs = jnp.where(cidx >= sh, pltpu.roll(x, sh, axis=0), 0.0)
        cf = lvl_ref[0, j]
        x = x + xs * cf[0:1, :] + pltpu.roll(xs, half, axis=1) * cf[1:2, :]
    x_in = jnp.where(cidx >= 1, pltpu.roll(x, 1, axis=0), 0.0)
    x_hi = _bf(x_in)
    x_lo = _bf(x_in - x_hi.astype(jnp.float32))
    q = q_ref[0]
    y_ref[0] = _dot(u, toep_ref[0]) + _dot(x_hi, q) + _dot(x_lo, q)


def _to_groups_kernel(u_ref, o_ref, *, lc, n_c):
    per, nb, lw = o_ref.shape
    slot = lax.broadcasted_iota(jnp.int32, (nb, LANES), 1) // n_c
    a = [u_ref[pl.ds(s, nb, stride=lc), :] for s in range(lc)]
    for gi in range(per):
        for j in range(lw // LANES):
            acc = None
            for ai in range(per):
                shift = ((ai - gi) % per) * n_c
                src = a[j * per + ai]
                piece = src if shift == 0 else pltpu.roll(src, shift, axis=1)
                acc = piece if acc is None else jnp.where(slot == ai, piece, acc)
            o_ref[gi, :, j * LANES:(j + 1) * LANES] = acc


def _from_groups_kernel(y_ref, o_ref, *, lc, n_c):
    per, nb, lw = y_ref.shape
    slot = lax.broadcasted_iota(jnp.int32, (nb, LANES), 1) // n_c
    for s in range(lc):
        j, ai = divmod(s, per)
        acc = None
        for gi in range(per):
            shift = ((gi - ai) % per) * n_c
            src = y_ref[gi, :, j * LANES:(j + 1) * LANES]
            piece = src if shift == 0 else pltpu.roll(src, shift, axis=1)
            acc = piece if acc is None else jnp.where(slot == gi, piece, acc)
        o_ref[pl.ds(s, nb, stride=lc), :] = acc


def _group_relayout(x, n_g, lc, to_groups, tile):
    if to_groups:
        n_tok, width = x.shape
    else:
        n_tok, width = x.shape[1] * lc, x.shape[2] // lc * n_g
    n_c = width // n_g
    per = LANES // n_c
    nb = tile // lc
    tok_spec = pl.BlockSpec((tile, LANES), lambda i, vb: (i, vb))
    grp_spec = pl.BlockSpec((per, nb, lc * n_c), lambda i, vb: (vb, i, 0))
    kern = _to_groups_kernel if to_groups else _from_groups_kernel
    out_shape = (n_g, n_tok // lc, lc * n_c) if to_groups else (n_tok, width)
    return pl.pallas_call(
        functools.partial(kern, lc=lc, n_c=n_c),
        grid=(n_tok // tile, n_g // per),
        in_specs=[tok_spec if to_groups else grp_spec],
        out_specs=grp_spec if to_groups else tok_spec,
        out_shape=jax.ShapeDtypeStruct(out_shape, x.dtype),
        compiler_params=_cparams("arbitrary", "arbitrary"),
        name="to_groups" if to_groups else "from_groups",
    )(x)


def s5_core(u, log_dt, a_re, a_im, b_re, b_im, c_re, c_im):
    bsz, seq, width = u.shape
    n_g, n_p = a_re.shape
    n_c = width // n_g
    lc = min(S5_CHUNK, seq)
    n_chunks = seq // lc
    toep, pm, q, lvl = s5_operators(log_dt, a_re, a_im, b_re, b_im, c_re, c_im, n_chunks, lc)
    tile = min(2048, bsz * seq)
    ug = _group_relayout(u.reshape(bsz * seq, width), n_g, lc, True, tile)
    rows, lw = bsz * n_chunks, lc * n_c
    yg = pl.pallas_call(
        functools.partial(_s5_kernel, n_chunks=n_chunks),
        grid=(n_g,),
        in_specs=[pl.BlockSpec((1, rows, lw), lambda g: (g, 0, 0)),
                  pl.BlockSpec((1, lw, lw), lambda g: (g, 0, 0)),
                  pl.BlockSpec((1, lw, 2 * n_p), lambda g: (g, 0, 0)),
                  pl.BlockSpec((1, 2 * n_p, lw), lambda g: (g, 0, 0)),
                  pl.BlockSpec((1,) + lvl.shape[1:], lambda g: (g, 0, 0, 0))],
        out_specs=pl.BlockSpec((1, rows, lw), lambda g: (g, 0, 0)),
        out_shape=jax.ShapeDtypeStruct((n_g, rows, lw), jnp.float32),
        compiler_params=_cparams("arbitrary"),
        name="s5",
    )(ug, toep, pm, q, lvl)
    return _group_relayout(yg, n_g, lc, False, tile).reshape(bsz, seq, width)


def _gelu_tanh(y):
    return 0.5 * y * (1.0 + jnp.tanh(math.sqrt(2.0 / math.pi) * (y + 0.044715 * (y * y * y))))


def _outproj_kernel(yr_ref, ys_ref, u_ref, x_ref, mods_ref, d_ref, wglu_ref, bglu_ref, wout_ref,
                    gpost_ref, gpre_ref, wrt_ref, x1_ref, h2_ref, lg_ref):
    m = mods_ref[0]
    yr = yr_ref[...]
    y5 = _gelu_tanh(ys_ref[...] + d_ref[...] * u_ref[...])
    y5 = y5 * _sigmoid(_dot(_bf(y5), wglu_ref[...]) + bglu_ref[...])
    wr = yr.shape[-1]
    mixed = _dot(_bf(yr), wout_ref[0:wr, :]) + _dot(_bf(y5), wout_ref[wr:, :])
    x1 = x_ref[...] + m[2:3, :] * _rms(mixed, gpost_ref[...])
    x1_ref[...] = x1
    h2 = _rms(x1, gpre_ref[...]) * (1.0 + m[4:5, :]) + m[3:4, :]
    half = h2.shape[-1] // 2
    _store_row_tiles(h2_ref, _pack_bf16_pair(h2[:, :half], h2[:, half:]))
    w_hi, w_lo = wrt_ref[0], wrt_ref[1]
    h_hi = _bf(h2)
    h_lo = _bf(h2 - h_hi.astype(jnp.float32))
    lg_ref[...] = _dot_nt(w_hi, h_hi) + (_dot_nt(w_hi, h_lo) + _dot_nt(w_lo, h_hi))


def out_proj(y_rwkv, y_s5, u, x, mods3, s5_d, w_glu_bf, b_glu, w_out_bf, g_post, g_pre, w_router_t,
             seq, tm):
    n_tok, dm = x.shape
    wr, ws = y_rwkv.shape[-1], y_s5.shape[-1]
    n_e = w_router_t.shape[0]
    tpb = seq // tm
    tok = lambda w: pl.BlockSpec((tm, w), lambda i: (i, 0))
    full = lambda a: pl.BlockSpec(a.shape, lambda i: (0,) * a.ndim)
    row = lambda t: t.reshape(1, -1)
    wr_hi = _bf(w_router_t)
    wr_split = jnp.stack([wr_hi, _bf(w_router_t - wr_hi.astype(jnp.float32))])
    consts = [row(s5_d), w_glu_bf, row(b_glu), w_out_bf, row(g_post), row(g_pre), wr_split]
    return pl.pallas_call(
        _outproj_kernel,
        grid=(n_tok // tm,),
        in_specs=[tok(wr), tok(ws), tok(ws), tok(dm),
                  pl.BlockSpec((1, N_MODS, dm), lambda i: (i // tpb, 0, 0))] + [full(a) for a in consts],
        out_specs=[tok(dm), _row_tile_spec(tm, dm // 2, lambda i: i),
                   pl.BlockSpec((n_e, tm), lambda i: (0, i))],
        out_shape=[jax.ShapeDtypeStruct((n_tok, dm), jnp.float32),
                   jax.ShapeDtypeStruct((n_tok * (dm // 2 // LANES), LANES), jnp.uint32),
                   jax.ShapeDtypeStruct((n_e, n_tok), jnp.float32)],
        compiler_params=_cparams("arbitrary"),
        name="outproj",
    )(y_rwkv, y_s5, u, x, mods3, *consts)


def _route_kernel(lg_ref, bias_ref, tri_ref, e_ref, w_ref, r_ref, cnt_ref, carry_ref):
    i = pl.program_id(0)

    @pl.when(i == 0)
    def _():
        carry_ref[...] = jnp.zeros_like(carry_ref)

    neg = -jnp.inf
    scores = _sigmoid(lg_ref[...])
    n_e, tm = scores.shape
    choice = scores + bias_ref[...]
    gsz = n_e // ROUTE_GROUPS
    c3 = choice.reshape(ROUTE_GROUPS, gsz, tm)
    io = lax.broadcasted_iota(jnp.int32, c3.shape, 1)
    m1 = jnp.max(c3, axis=1, keepdims=True)
    first = jnp.min(jnp.where(c3 == m1, io, gsz), axis=1, keepdims=True)
    m2 = jnp.max(jnp.where(io == first, neg, c3), axis=1, keepdims=True)
    gs = m1 + m2
    gi = lax.broadcasted_iota(jnp.int32, gs.shape, 0)
    rank = jnp.zeros(gs.shape, jnp.int32)
    for j in range(ROUTE_GROUPS):
        gj = gs[j:j + 1]
        beats = (gj > gs) | ((gj == gs) & (gi > j))
        rank = rank + beats.astype(jnp.int32)
    masked = jnp.where(rank < ROUTE_TOPK_GROUPS, c3, neg).reshape(n_e, tm)

    eio = lax.broadcasted_iota(jnp.int32, (n_e, tm), 0)
    ids, ws = [], []
    mhot = jnp.zeros((n_e, tm), jnp.float32)
    for _ in range(TOP_K):
        m = jnp.max(masked, axis=0, keepdims=True)
        idx = jnp.min(jnp.where(masked == m, eio, n_e), axis=0, keepdims=True)
        sel = eio == idx
        ws.append(jnp.sum(jnp.where(sel, scores, 0.0), axis=0, keepdims=True))
        ids.append(idx)
        masked = jnp.where(sel, neg, masked)
        mhot = jnp.where(sel, 1.0, mhot)
    wsum = ws[0]
    for t in ws[1:]:
        wsum = wsum + t
    before = _dot(_bf(mhot), tri_ref[...]) + carry_ref[...]
    ranks = [jnp.sum(jnp.where(eio == idx, before, 0.0), axis=0, keepdims=True) for idx in ids]
    e_ref[...] = jnp.concatenate(ids, axis=0)
    w_ref[...] = jnp.concatenate(ws, axis=0) / wsum * ROUTE_SCALE
    r_ref[...] = jnp.concatenate(ranks, axis=0).astype(jnp.int32)
    carry_ref[...] = carry_ref[...] + jnp.sum(mhot, axis=1, keepdims=True)
    cnt_ref[...] = carry_ref[...]


def route(logits_t, router_bias, tm):
    n_e, n_tok = logits_t.shape
    tri = (jnp.arange(tm)[:, None] < jnp.arange(tm)[None, :]).astype(jnp.bfloat16)
    kt = lambda: pl.BlockSpec((TOP_K, tm), lambda i: (0, i))
    return pl.pallas_call(
        _route_kernel,
        grid=(n_tok // tm,),
        in_specs=[pl.BlockSpec((n_e, tm), lambda i: (0, i)),
                  pl.BlockSpec((n_e, 1), lambda i: (0, 0)),
                  pl.BlockSpec((tm, tm), lambda i: (0, 0))],
        out_specs=[kt(), kt(), kt(), pl.BlockSpec((n_e, 1), lambda i: (0, 0))],
        out_shape=[jax.ShapeDtypeStruct((TOP_K, n_tok), jnp.int32),
                   jax.ShapeDtypeStruct((TOP_K, n_tok), jnp.float32),
                   jax.ShapeDtypeStruct((TOP_K, n_tok), jnp.int32),
                   jax.ShapeDtypeStruct((n_e, 1), jnp.float32)],
        scratch_shapes=[pltpu.VMEM((n_e, 1), jnp.float32)],
        compiler_params=_cparams("arbitrary"),
        name="route",
    )(logits_t, router_bias.reshape(n_e, 1), tri)


def _dest_kernel(e_ref, r_ref, ps_ref, d_ref):
    n_e = ps_ref.shape[0]
    tm = e_ref.shape[1]
    eio = lax.broadcasted_iota(jnp.int32, (n_e, tm), 0)
    ps = ps_ref[...]
    rows = [jnp.sum(jnp.where(eio == e_ref[k:k + 1, :], ps, 0.0), axis=0, keepdims=True)
            for k in range(TOP_K)]
    d_ref[...] = jnp.concatenate(rows, axis=0).astype(jnp.int32) + r_ref[...]


def dest_rows(e_idx, rank, pad_start, tm):
    n_tok = e_idx.shape[1]
    n_e = pad_start.shape[0]
    kt = lambda: pl.BlockSpec((TOP_K, tm), lambda i: (0, i))
    return pl.pallas_call(
        _dest_kernel,
        grid=(n_tok // tm,),
        in_specs=[kt(), kt(), pl.BlockSpec((n_e, 1), lambda i: (0, 0))],
        out_specs=kt(),
        out_shape=jax.ShapeDtypeStruct((TOP_K, n_tok), jnp.int32),
        compiler_params=_cparams("arbitrary"),
        name="dest",
    )(e_idx, rank, pad_start.astype(jnp.float32).reshape(n_e, 1))


def _dispatch_kernel(fill_start_ref, fill_len_ref, nu_ref, dest_ref, h_ref, xs_ref, zeros, sem, zsem,
                     *, n_s, bm, experts_per_step, tiles_per_step, n_tiles):
    i = pl.program_id(0)
    td = dest_ref.shape[1]

    @pl.when(i == 0)
    def _():
        zeros[...] = jnp.zeros_like(zeros)

    def body(t, carry):
        for k in range(TOP_K):
            pltpu.make_async_copy(_row(h_ref, t, n_s), _row(xs_ref, dest_ref[k, t], n_s),
                                  sem).start(priority=k % 2)
        return carry

    lax.fori_loop(0, td, body, 0)

    for j in range(experts_per_step):
        e = i * experts_per_step + j
        start, length = fill_start_ref[e], fill_len_ref[e]
        piece = bm // 2
        while piece >= 1:
            @pl.when((length & piece) != 0)
            def _(piece=piece):
                off = start + (length & ~(2 * piece - 1))
                cp = pltpu.make_async_copy(zeros.at[pl.ds(0, piece * n_s), :],
                                           xs_ref.at[pl.ds(off * n_s, piece * n_s), :], zsem)
                cp.start()
                cp.wait()
            piece //= 2

    for j in range(tiles_per_step):
        tile = i * tiles_per_step + j

        @pl.when((tile >= nu_ref[0]) & (tile < n_tiles))
        def _(tile=tile):
            cp = pltpu.make_async_copy(zeros, xs_ref.at[pl.ds(tile * (bm * n_s), bm * n_s), :], zsem)
            cp.start()
            cp.wait()

    for _ in range(TOP_K):
        pltpu.make_async_copy(h_ref, xs_ref.at[pl.ds(0, td * n_s), :], sem).wait()


def dispatch(dest, h2p, fill_start, fill_len, n_used, n_tok, n_tiles, td, bm):
    n_s = h2p.shape[0] // n_tok
    n_steps = n_tok // td
    n_e = fill_start.shape[0]
    experts_per_step = -(-n_e // n_steps)
    tiles_per_step = -(-n_tiles // n_steps)
    pad = n_steps * experts_per_step - n_e
    fill_start = jnp.pad(fill_start, (0, pad))
    fill_len = jnp.pad(fill_len, (0, pad))
    return pl.pallas_call(
        functools.partial(_dispatch_kernel, n_s=n_s, bm=bm, experts_per_step=experts_per_step,
                          tiles_per_step=tiles_per_step, n_tiles=n_tiles),
        grid_spec=pltpu.PrefetchScalarGridSpec(
            num_scalar_prefetch=3,
            grid=(n_steps,),
            in_specs=[pl.BlockSpec((TOP_K, td), lambda i, *_: (0, i), memory_space=pltpu.SMEM),
                      _row_tile_spec(td, n_s * LANES, lambda i, *_: i)],
            out_specs=pl.BlockSpec(memory_space=pl.ANY),
            scratch_shapes=[pltpu.VMEM((bm * n_s, LANES), h2p.dtype),
                            pltpu.SemaphoreType.DMA(()), pltpu.SemaphoreType.DMA(())]),
        out_shape=jax.ShapeDtypeStruct((n_tiles * bm * n_s, LANES), h2p.dtype),
        compiler_params=_cparams("arbitrary"),
        name="dispatch",
    )(fill_start, fill_len, n_used, dest, h2p)


def _expert_kernel(te_ref, nu_ref, nt_ref, x_ref, wg_hbm, wu_hbm, wd_hbm, o_ref,
                   wg_st, wu_st, wd_st, wgu_bf, wd_bf, grp, sems, *, bm):
    i = pl.program_id(0)
    ff = wd_bf.shape[0]
    n_used = nu_ref[0]

    def fetch(e, slot):
        return (pltpu.make_async_copy(wg_hbm.at[e], wg_st.at[slot], sems.at[slot]),
                pltpu.make_async_copy(wu_hbm.at[e], wu_st.at[slot], sems.at[slot]),
                pltpu.make_async_copy(wd_hbm.at[e], wd_st.at[slot], sems.at[slot]))

    @pl.when(i == 0)
    def _():
        grp[0] = 0
        for cp in fetch(te_ref[0], 0):
            cp.start()

    @pl.when(i < n_used)
    def _():
        e = te_ref[i]

        @pl.when((i == 0) | (e != te_ref[jnp.maximum(i - 1, 0)]))
        def _():
            slot = grp[0] & 1
            grp[0] = grp[0] + 1
            for cp in fetch(e, slot):
                cp.wait()
            nxt = i + nt_ref[e]

            @pl.when(nxt < n_used)
            def _():
                for cp in fetch(te_ref[jnp.minimum(nxt, n_used - 1)], 1 - slot):
                    cp.start()

            wgu_bf[:, :ff] = _bf(wg_st[slot])
            wgu_bf[:, ff:] = _bf(wu_st[slot])
            wd_bf[...] = _bf(wd_st[slot])

        hi, lo = _unpack_bf16_pair(_load_row_tiles(x_ref, bm))
        x = jnp.concatenate([hi, lo], axis=1)
        gu = _dot(x, wgu_bf[...])
        hid = _silu(gu[:, :ff]) * gu[:, ff:]
        y = _dot(_bf(hid), wd_bf[...])
        half = y.shape[1] // 2
        _store_row_tiles(o_ref, _pack_bf16_pair(y[:, :half], y[:, half:]))

    @pl.when(i >= nu_ref[0])
    def _():
        o_ref[...] = jnp.zeros_like(o_ref)


def expert_ffn(tile_e, n_used, tiles_per_expert, xs, n_rows, w_gate, w_up, w_down, bm):
    dm = w_gate.shape[1]
    half = dm // 2
    ff = w_gate.shape[-1]
    rows_in = _row_tile_spec(bm, half, lambda i, te, nu, nt: jnp.minimum(i, nu[0] - 1))
    rows_out = _row_tile_spec(bm, half, lambda i, te, nu, nt: i)
    hbm = pl.BlockSpec(memory_space=pl.ANY)
    return pl.pallas_call(
        functools.partial(_expert_kernel, bm=bm),
        grid_spec=pltpu.PrefetchScalarGridSpec(
            num_scalar_prefetch=3,
            grid=(n_rows // bm,),
            in_specs=[rows_in, hbm, hbm, hbm],
            out_specs=rows_out,
            scratch_shapes=[pltpu.VMEM((2, dm, ff), jnp.float32),
                            pltpu.VMEM((2, dm, ff), jnp.float32),
                            pltpu.VMEM((2, ff, dm), jnp.float32),
                            pltpu.VMEM((dm, 2 * ff), jnp.bfloat16),
                            pltpu.VMEM((ff, dm), jnp.bfloat16),
                            pltpu.SMEM((1,), jnp.int32),
                            pltpu.SemaphoreType.DMA((2,))]),
        out_shape=jax.ShapeDtypeStruct((n_rows * (half // LANES), LANES), jnp.uint32),
        compiler_params=_cparams("arbitrary"),
        name="expert",
    )(tile_e, n_used, tiles_per_expert, xs, w_gate, w_up, w_down)


def _combine_kernel(dest_ref, dnext_ref, w_ref, x1_ref, h2_ref, mods_ref, g_ref, wsg_ref, wsu_ref,
                    wsd_ref, ys_ref, o_ref, gbuf, rbuf, wrep, sems, *, n_s):
    i = pl.program_id(0)
    n_steps = pl.num_programs(0)
    tc = x1_ref.shape[0]
    slot = i % 2

    def gather(d_ref, slot_, t):
        for k in range(TOP_K):
            pltpu.make_async_copy(_row(ys_ref, d_ref[k, t], n_s), _row(gbuf.at[slot_, k], t, n_s),
                                  sems.at[slot_]).start(priority=k % 2)

    def weighted(t):
        acc_hi = acc_lo = None
        for k in range(TOP_K):
            wk = jnp.broadcast_to(wrep[k, pl.ds(t, 1), :], (n_s, LANES))
            hi, lo = _unpack_pair_f32(_row(gbuf.at[slot, k], t, n_s)[...])
            t_hi, t_lo = wk * hi, wk * lo
            acc_hi = t_hi if acc_hi is None else acc_hi + t_hi
            acc_lo = t_lo if acc_lo is None else acc_lo + t_lo
        _row(rbuf.at[0], t, n_s)[...] = acc_hi
        _row(rbuf.at[1], t, n_s)[...] = acc_lo

    @pl.when(i == 0)
    def _():
        def body(t, carry):
            gather(dest_ref, 0, t)
            return carry
        lax.fori_loop(0, tc, body, 0)

    w = w_ref[...]
    for k in range(TOP_K):
        wrep[k] = jnp.broadcast_to(w[:, k:k + 1], (tc, LANES))

    for k in range(TOP_K):
        pltpu.make_async_copy(ys_ref.at[pl.ds(0, tc * n_s), :], gbuf.at[slot, k], sems.at[slot]).wait()

    @pl.when(i + 1 < n_steps)
    def _():
        def body(t, carry):
            gather(dnext_ref, 1 - slot, t)
            weighted(t)
            return carry
        lax.fori_loop(0, tc, body, 0)

    @pl.when(i + 1 == n_steps)
    def _():
        def body(t, carry):
            weighted(t)
            return carry
        lax.fori_loop(0, tc, body, 0)

    hi, lo = _unpack_bf16_pair(_load_row_tiles(h2_ref, tc))
    h = jnp.concatenate([hi, lo], axis=1)
    hid = _silu(_dot(h, wsg_ref[...])) * _dot(h, wsu_ref[...])
    routed = jnp.concatenate([_load_row_tiles(rbuf.at[0], tc), _load_row_tiles(rbuf.at[1], tc)], axis=1)
    y = _dot(_bf(hid), wsd_ref[...]) + routed
    m = mods_ref[0]
    o_ref[...] = x1_ref[...] + m[5:6, :] * _rms(y, g_ref[...])


def combine(dest, w_tk, x1, h2p, mods3, g_post, ws_gate_bf, ws_up_bf, ws_down_bf, ys, seq, tc):
    n_tok, dm = x1.shape
    n_s = dm // 2 // LANES
    n_steps = n_tok // tc
    tpb = seq // tc
    full = lambda a: pl.BlockSpec(a.shape, lambda i: (0,) * a.ndim)
    tok = lambda w: pl.BlockSpec((tc, w), lambda i: (i, 0))
    g2 = g_post.reshape(1, dm)
    return pl.pallas_call(
        functools.partial(_combine_kernel, n_s=n_s),
        grid=(n_steps,),
        in_specs=[pl.BlockSpec((TOP_K, tc), lambda i: (0, i), memory_space=pltpu.SMEM),
                  pl.BlockSpec((TOP_K, tc), lambda i: (0, jnp.minimum(i + 1, n_steps - 1)),
                               memory_space=pltpu.SMEM),
                  tok(TOP_K),
                  tok(dm), _row_tile_spec(tc, dm // 2, lambda i: i),
                  pl.BlockSpec((1, N_MODS, dm), lambda i: (i // tpb, 0, 0)),
                  full(g2), full(ws_gate_bf), full(ws_up_bf), full(ws_down_bf),
                  pl.BlockSpec(memory_space=pl.ANY)],
        out_specs=tok(dm),
        out_shape=jax.ShapeDtypeStruct((n_tok, dm), jnp.float32),
        scratch_shapes=[pltpu.VMEM((2, TOP_K, tc * n_s, LANES), jnp.uint32),
                        pltpu.VMEM((2, tc * n_s, LANES), jnp.float32),
                        pltpu.VMEM((TOP_K, tc, LANES), jnp.float32),
                        pltpu.SemaphoreType.DMA((2,))],
        compiler_params=_cparams("arbitrary"),
        name="combine",
    )(dest, dest, w_tk, x1, h2p, mods3, g2, ws_gate_bf, ws_up_bf, ws_down_bf, ys)


def moe_ffn(h2, x1, logits_t, mods3, router_bias, w_gate, w_up, w_down, ws_gate, ws_up, ws_down,
            g_post, seq):
    n_tok, dm = x1.shape
    n_e = w_gate.shape[0]
    bm = EXPERT_ROW_TILE
    e_idx, w_kt, rank, cnt = route(logits_t, router_bias, min(512, n_tok))
    counts = cnt[:, 0].astype(jnp.int32)
    padded = (counts + bm - 1) // bm * bm
    pad_end = jnp.cumsum(padded)
    pad_start = (pad_end - padded).astype(jnp.int32)
    n_tiles = -(-(n_tok * TOP_K) // bm) + n_e
    n_used = (pad_end[-1] // bm).astype(jnp.int32)
    tile_start = jnp.arange(n_tiles, dtype=jnp.int32) * bm
    tile_e = jnp.minimum(jnp.sum((pad_end[None, :] <= tile_start[:, None]).astype(jnp.int32), axis=1),
                         n_e - 1).astype(jnp.int32)
    tile_e = jnp.where(jnp.arange(n_tiles) < n_used, tile_e, tile_e[n_used - 1])
    dest = dest_rows(e_idx, rank, pad_start, min(512, n_tok))
    n_used1 = n_used.reshape(1)
    xs = dispatch(dest, h2, pad_start + counts, padded - counts, n_used1, n_tok, n_tiles,
                  min(256, n_tok), bm)
    ys = expert_ffn(tile_e, n_used1, (padded // bm).astype(jnp.int32), xs, n_tiles * bm,
                    w_gate, w_up, w_down, bm)
    bf = lambda a: a.astype(jnp.bfloat16)
    return combine(dest, w_kt.T, x1, h2, mods3, g_post, bf(ws_gate), bf(ws_up), bf(ws_down), ys,
                   seq, min(256, n_tok))


def kernel(x, c, w_ada, b_ada, g_pre_mix, g_post_mix, g_pre_ffn, g_post_ffn, w_in, mu_shift, rwkv_w0, rwkv_w2, rwkv_a0, rwkv_a2, rwkv_g2, rwkv_k_k, rwkv_k_a, rwkv_r_k, rwkv_ln_w, rwkv_ln_b, s5_log_dt, s5_a_re, s5_a_im, s5_b_re, s5_b_im, s5_c_re, s5_c_im, s5_d, s5_w_glu, s5_b_glu, w_out, w_router, router_bias, w_gate, w_up, w_down, ws_gate, ws_up, ws_down):
    bsz, seq, dm = x.shape
    depth = w_ada.shape[0]
    bf = lambda a: a.astype(jnp.bfloat16)
    tm = min(512, seq)
    for l in range(depth):
        mods3 = ada_mods(c, w_ada[l], b_ada[l]).reshape(bsz, N_MODS, dm)
        n_rwkv = mu_shift.shape[-1]
        p, u = in_proj(x, mods3, g_pre_mix[l], bf(w_in[l]), n_rwkv, tm)
        y_rwkv = rwkv_mixer(p, mu_shift[l], rwkv_w0[l], rwkv_w2[l], rwkv_a0[l], rwkv_a2[l],
                            rwkv_g2[l], rwkv_k_k[l], rwkv_k_a[l], rwkv_r_k[l].reshape(-1),
                            rwkv_ln_w[l], rwkv_ln_b[l])
        y_s5 = s5_core(u, s5_log_dt[l], s5_a_re[l], s5_a_im[l], s5_b_re[l], s5_b_im[l],
                       s5_c_re[l], s5_c_im[l])
        flat = lambda a: a.reshape(bsz * seq, a.shape[-1])
        x1, h2, logits_t = out_proj(flat(y_rwkv), flat(y_s5), flat(u), flat(x), mods3, s5_d[l],
                                    bf(s5_w_glu[l]), s5_b_glu[l], bf(w_out[l]), g_post_mix[l],
                                    g_pre_ffn[l], w_router[l].T, seq, tm)
        out = moe_ffn(h2, x1, logits_t, mods3, router_bias[l], w_gate[l], w_up[l], w_down[l],
                      ws_gate[l], ws_up[l], ws_down[l], g_post_ffn[l], seq)
        x = out.reshape(bsz, seq, dm)
    return x
```

```python
import functools
import math

import jax
import jax.numpy as jnp
from jax import lax
from jax.experimental import pallas as pl
from jax.experimental.pallas import tpu as pltpu

NORM_EPS = 1e-6
LNX_EPS = 64e-5
L2_EPS = 1e-12
S5_MAX_REAL = -1e-4
ROUTE_SCALE = 2.5
N_MODS = 6

RWKV_HEAD_DIM = 64
DECAY_LORA = 64
ICLR_LORA = 64
GATE_LORA = 128
S5_GROUP = 16
S5_STATE = 64
TOP_K = 8
ROUTE_GROUPS = 8
ROUTE_TOPK_GROUPS = 4

RWKV_CHUNK = 64
S5_CHUNK = 16
EXPERT_ROW_TILE = 512

VMEM_LIMIT = 56 * 1024 * 1024

HI = lax.Precision.HIGHEST


def _cparams(*sem):
    return pltpu.CompilerParams(dimension_semantics=sem, vmem_limit_bytes=VMEM_LIMIT)


def _dot(a, b, precision=None):
    return jnp.dot(a, b, preferred_element_type=jnp.float32, precision=precision)


def _dot_nt(a, b, precision=None):
    return lax.dot_general(a, b, (((1,), (1,)), ((), ())),
                           preferred_element_type=jnp.float32, precision=precision)


def _dot_tn(a, b, precision=None):
    return lax.dot_general(a, b, (((0,), (0,)), ((), ())),
                           preferred_element_type=jnp.float32, precision=precision)


def _bdot(a, b):
    return lax.dot_general(a, b, (((2,), (1,)), ((0,), (0,))), preferred_element_type=jnp.float32)


def _bdot_nt(a, b):
    return lax.dot_general(a, b, (((2,), (2,)), ((0,), (0,))), preferred_element_type=jnp.float32)


def _bf(x):
    return x.astype(jnp.bfloat16)


def _pack_bf16_pair(hi, lo):
    hb = lax.bitcast_convert_type(_bf(hi).astype(jnp.float32), jnp.uint32)
    lb = lax.bitcast_convert_type(_bf(lo).astype(jnp.float32), jnp.uint32)
    return (hb & jnp.uint32(0xFFFF0000)) | (lb >> 16)


def _unpack_pair_f32(w):
    hi = lax.bitcast_convert_type(w & jnp.uint32(0xFFFF0000), jnp.float32)
    lo = lax.bitcast_convert_type(w << 16, jnp.float32)
    return hi, lo


def _unpack_bf16_pair(w):
    hi, lo = _unpack_pair_f32(w)
    return _bf(hi), _bf(lo)


LANES = 128


def _row_tile_spec(rows, width, row_block):
    return pl.BlockSpec((rows * (width // LANES), LANES), lambda *a: (row_block(*a), 0))


def _store_row_tiles(ref, x):
    rows = x.shape[0]
    n_s = ref.shape[0] // rows
    for s in range(n_s):
        ref[pl.ds(s, rows, stride=n_s), :] = x[:, s * LANES:(s + 1) * LANES]


def _load_row_tiles(ref, rows):
    n_s = ref.shape[0] // rows
    return jnp.concatenate([ref[pl.ds(s, rows, stride=n_s), :] for s in range(n_s)], axis=1)


def _row(ref, r, n_s):
    return ref.at[pl.ds(pl.multiple_of(r * n_s, n_s), n_s), :]


def _sigmoid(x):
    return 1.0 / (1.0 + jnp.exp(-x))


def _silu(x):
    return x * _sigmoid(x)


def _rms(x, gain):
    return x * lax.rsqrt(jnp.mean(x * x, axis=-1, keepdims=True) + NORM_EPS) * gain


def _ada_kernel(c_ref, w_ref, b_ref, o_ref):
    c = c_ref[...]
    o_ref[...] = _dot(_silu(c), w_ref[...], HI) + b_ref[...]


def ada_mods(c, w_ada, b_ada):
    bsz, dm = c.shape
    n = w_ada.shape[1]
    tn = dm
    return pl.pallas_call(
        _ada_kernel,
        grid=(n // tn,),
        in_specs=[pl.BlockSpec((bsz, dm), lambda j: (0, 0)),
                  pl.BlockSpec((dm, tn), lambda j: (0, j)),
                  pl.BlockSpec((1, tn), lambda j: (0, j))],
        out_specs=pl.BlockSpec((bsz, tn), lambda j: (0, j)),
        out_shape=jax.ShapeDtypeStruct((bsz, n), jnp.float32),
        compiler_params=_cparams("arbitrary"),
        name="ada",
    )(c, w_ada, b_ada.reshape(1, n))


def _inproj_kernel(x_ref, mods_ref, g_ref, w_ref, p_ref, u_ref):
    x = x_ref[0]
    m = mods_ref[0]
    h = _rms(x, g_ref[...]) * (1.0 + m[1:2, :]) + m[0:1, :]
    proj = _dot(_bf(h), w_ref[...])
    n_p = p_ref.shape[-1]
    p_ref[0] = proj[:, :n_p]
    u_ref[0] = proj[:, n_p:]


def in_proj(x, mods3, g_pre, w_in_bf, n_rwkv, tm):
    bsz, seq, dm = x.shape
    n = w_in_bf.shape[1]
    n_s5 = n - n_rwkv
    return pl.pallas_call(
        _inproj_kernel,
        grid=(bsz, seq // tm),
        in_specs=[pl.BlockSpec((1, tm, dm), lambda b, i: (b, i, 0)),
                  pl.BlockSpec((1, N_MODS, dm), lambda b, i: (b, 0, 0)),
                  pl.BlockSpec((1, dm), lambda b, i: (0, 0)),
                  pl.BlockSpec((dm, n), lambda b, i: (0, 0))],
        out_specs=[pl.BlockSpec((1, tm, n_rwkv), lambda b, i: (b, i, 0)),
                   pl.BlockSpec((1, tm, n_s5), lambda b, i: (b, i, 0))],
        out_shape=[jax.ShapeDtypeStruct((bsz, seq, n_rwkv), jnp.float32),
                   jax.ShapeDtypeStruct((bsz, seq, n_s5), jnp.float32)],
        compiler_params=_cparams("arbitrary", "arbitrary"),
        name="inproj",
    )(x, mods3, g_pre.reshape(1, dm), w_in_bf)


RWKV_TILE = 256


def _split_dot(x, m01, terms):
    blk = m01.shape[0]
    pieces = []
    rem = x
    for _ in range(terms):
        piece = _bf(rem)
        pieces.append(piece)
        rem = rem - piece.astype(jnp.float32)
    cols = []
    for c in range(x.shape[1] // blk):
        acc = None
        for piece in pieces:
            part = _dot(piece[:, c * blk:(c + 1) * blk], m01)
            acc = part if acc is None else acc + part
        cols.append(acc)
    return cols[0] if len(cols) == 1 else jnp.concatenate(cols, axis=1)


def _rwkv_kernel(p_ref, mu_ref, w0_ref, w2_ref, a0_ref, a2_ref, g2_ref, kk_ref, ka_ref,
                  rk_ref, lnw_ref, lnb_ref, bd_ref, o_ref,
                  s_ref, carry_ref, rt_ref, at_ref, bt_ref, kt_ref, v_ref, wl_ref, y_ref,
                  *, width, chunk):
    hd = RWKV_HEAD_DIM
    pw = 2 * hd
    pairs = width // pw
    i = pl.program_id(1)

    @pl.when(i == 0)
    def _():
        s_ref[...] = jnp.zeros_like(s_ref)
        carry_ref[...] = jnp.zeros_like(carry_ref)

    p = p_ref[0]
    n_t = p.shape[0]
    n_chunks = n_t // chunk
    row = lax.broadcasted_iota(jnp.int32, p.shape, 0)
    prev = jnp.where(row == 0, carry_ref[0:1, :], pltpu.roll(p, 1, axis=0))
    carry_ref[0:1, :] = p[n_t - 1:n_t, :]
    pm = p + (prev - p) * mu_ref[...]

    r = pm[:, 0:width]
    k = pm[:, width:2 * width]
    v = pm[:, 2 * width:3 * width]
    c0 = 3 * width
    w_lo = pm[:, c0:c0 + DECAY_LORA]
    a_lo = pm[:, c0 + DECAY_LORA:c0 + DECAY_LORA + ICLR_LORA]
    g_lo = pm[:, c0 + DECAY_LORA + ICLR_LORA:]

    z = w0_ref[...] + _dot(_bf(jnp.tanh(w_lo)), w2_ref[...])
    softplus_neg = jnp.maximum(-z, 0.0) + jnp.log(1.0 + jnp.exp(-jnp.abs(z)))
    logd = -jnp.exp(-softplus_neg - 0.5)
    iclr = _sigmoid(a0_ref[...] + _dot(_bf(a_lo), a2_ref[...]))
    gate = _dot(_bf(_sigmoid(g_lo)), g2_ref[...])

    bd = bd_ref[...]
    kk = k * kk_ref[...]
    kk = kk / jnp.maximum(jnp.sqrt(_split_dot(kk * kk, bd, 2)), L2_EPS)
    k2 = k * (1.0 + (iclr - 1.0) * ka_ref[...])
    bonus = _split_dot(r * k2 * rk_ref[...], bd, 2) * v

    ti = lax.broadcasted_iota(jnp.int32, (n_t, n_t), 0)
    si = lax.broadcasted_iota(jnp.int32, (n_t, n_t), 1)
    tri = jnp.where((ti >= si) & (ti // chunk == si // chunk), 1.0, 0.0).astype(jnp.bfloat16)
    cum = _split_dot_lhs(tri, logd, 3)
    e_pos = jnp.exp(cum)
    e_neg = jnp.exp(-cum)
    rt_ref[...] = _bf(r * e_pos)
    at_ref[...] = _bf(-kk * jnp.exp(cum - logd))
    bt_ref[...] = _bf(kk * iclr * e_neg)
    kt_ref[...] = _bf(k2 * e_neg)
    v_ref[...] = _bf(v)
    for c in range(n_chunks):
        wl_ref[c:c + 1, :] = e_pos[(c + 1) * chunk - 1:(c + 1) * chunk, :]

    two_l = 2 * chunk
    lane = lax.broadcasted_iota(jnp.int32, (chunk, pw), 1)
    lane0 = lane < hd
    bi = lax.broadcasted_iota(jnp.int32, (two_l, two_l), 0)
    bj = lax.broadcasted_iota(jnp.int32, (two_l, two_l), 1)
    same = (bi // chunk) == (bj // chunk)
    low_strict = same & (bi > bj)
    low_incl = same & (bi >= bj)
    eye_t = jnp.where(bi == bj, 1.0, 0.0)
    pi_ = lax.broadcasted_iota(jnp.int32, (pw, pw), 0)
    pj_ = lax.broadcasted_iota(jnp.int32, (pw, pw), 1)
    eye_p = jnp.where(pi_ == pj_, 1.0, 0.0)
    n_sq = max(1, int(math.ceil(math.log2(chunk))) - 1)
    zero = jnp.zeros((), jnp.bfloat16)

    def stack(x):
        return jnp.concatenate([jnp.where(lane0, x, zero), jnp.where(lane0, zero, x)], axis=0)

    def gather(ref):
        return jnp.stack([stack(ref[c * chunk:(c + 1) * chunk, hp * pw:(hp + 1) * pw])
                          for c in range(n_chunks) for hp in range(pairs)], axis=0)

    a_s, b_s, k_s, r_s, v_s = (gather(ref) for ref in (at_ref, bt_ref, kt_ref, rt_ref, v_ref))
    wl = jnp.stack([wl_ref[c:c + 1, hp * pw:(hp + 1) * pw]
                    for c in range(n_chunks) for hp in range(pairs)], axis=0)
    gram = _bdot_nt(jnp.concatenate([a_s, r_s], axis=1), jnp.concatenate([b_s, k_s], axis=1))
    m_ab = jnp.where(low_strict, gram[:, :two_l, :two_l], 0.0)
    m_ak = jnp.where(low_strict, gram[:, :two_l, two_l:], 0.0)
    n_rb = jnp.where(low_incl, gram[:, two_l:, :two_l], 0.0)
    n_rk = jnp.where(low_incl, gram[:, two_l:, two_l:], 0.0)
    t_inv = eye_t + m_ab
    m_pow = _bf(m_ab)
    for _ in range(n_sq):
        m_pow = _bf(_bdot(m_pow, m_pow))
        t_inv = t_inv + _bdot(_bf(t_inv), m_pow)
    makv = _bdot(_bf(m_ak), v_s)
    tx_bf = _bf(_bdot(_bf(t_inv), jnp.concatenate([a_s, _bf(makv)], axis=2)))
    nx = _bdot(_bf(n_rb), tx_bf)
    rbar = _bf(r_s.astype(jnp.float32) + nx[:, :, :pw])
    y0 = nx[:, :, pw:] + _bdot(_bf(n_rk), v_s)
    tb = _bdot(jnp.swapaxes(tx_bf, 1, 2), b_s)
    pmat = _bf((eye_p + tb[:, :pw, :]) * wl)
    dmat = (tb[:, pw:, :] + _bdot(jnp.swapaxes(v_s, 1, 2), k_s)) * wl

    s = s_ref[...]
    for c in range(n_chunks):
        sel = slice(c * pairs, (c + 1) * pairs)
        s_bf = _bf(s)
        ys = _bdot_nt(rbar[sel], s_bf) + y0[sel]
        yc_ = ys[:, :chunk, :] + ys[:, chunk:, :]
        for hp in range(pairs):
            y_ref[c * chunk:(c + 1) * chunk, hp * pw:(hp + 1) * pw] = yc_[hp]
        s = _bdot(s_bf, pmat[sel]) + dmat[sel]
    s_ref[...] = s

    y = y_ref[...]
    inv_hd = 1.0 / hd
    mean = _split_dot(y, bd, 2) * inv_hd
    yc = y - mean
    var = _split_dot(yc * yc, bd, 2) * inv_hd
    yn = yc * lax.rsqrt(var + LNX_EPS) * lnw_ref[...] + lnb_ref[...]
    o_ref[0] = (yn + bonus) * gate


def _split_dot_lhs(m01, x, terms):
    acc = None
    rem = x
    for _ in range(terms):
        piece = _bf(rem)
        part = _dot(m01, piece)
        acc = part if acc is None else acc + part
        rem = rem - piece.astype(jnp.float32)
    return acc


def rwkv_mixer(p, mu, w0, w2, a0, a2, g2, k_k, k_a, r_k, ln_w, ln_b):
    bsz, seq, n_p = p.shape
    width = w0.shape[-1]
    pairs = width // (2 * RWKV_HEAD_DIM)
    chunk = min(RWKV_CHUNK, seq)
    tile = min(RWKV_TILE, seq)
    hid = jnp.arange(2 * RWKV_HEAD_DIM, dtype=jnp.int32) // RWKV_HEAD_DIM
    bd = (hid[:, None] == hid[None, :]).astype(jnp.bfloat16)
    row = lambda t: t.reshape(1, -1)
    full = lambda a: pl.BlockSpec(a.shape, lambda b, i: (0,) * a.ndim)
    consts = [row(mu), row(w0), _bf(w2), row(a0), _bf(a2), _bf(g2), row(k_k), row(k_a), row(r_k),
              row(ln_w), row(ln_b), bd]
    act = lambda: pltpu.VMEM((tile, width), jnp.bfloat16)
    return pl.pallas_call(
        functools.partial(_rwkv_kernel, width=width, chunk=chunk),
        grid=(bsz, seq // tile),
        in_specs=[pl.BlockSpec((1, tile, n_p), lambda b, i: (b, i, 0))] + [full(a) for a in consts],
        out_specs=pl.BlockSpec((1, tile, width), lambda b, i: (b, i, 0)),
        out_shape=jax.ShapeDtypeStruct((bsz, seq, width), jnp.float32),
        scratch_shapes=[pltpu.VMEM((pairs, 2 * RWKV_HEAD_DIM, 2 * RWKV_HEAD_DIM), jnp.float32),
                        pltpu.VMEM((8, n_p), jnp.float32),
                        act(), act(), act(), act(), act(),
                        pltpu.VMEM((max(8, tile // chunk), width), jnp.float32),
                        pltpu.VMEM((tile, width), jnp.float32)],
        compiler_params=_cparams("arbitrary", "arbitrary"),
        name="rwkv",
    )(p, *consts)


def _s5_discretise(a_re, a_im, dt):
    lam_re = jnp.minimum(a_re, S5_MAX_REAL)
    lam_im = a_im
    mag = jnp.exp(lam_re * dt)
    ang = lam_im * dt
    ab_re, ab_im = mag * jnp.cos(ang), mag * jnp.sin(ang)
    den = lam_re * lam_re + lam_im * lam_im
    n_re, n_im = ab_re - 1.0, ab_im
    q_re = (n_re * lam_re + n_im * lam_im) / den
    q_im = (n_im * lam_re - n_re * lam_im) / den
    return ab_re, ab_im, q_re, q_im


def _s5_power(ab_re, ab_im, t, t_max):
    pr = jnp.ones(t.shape, jnp.float32)
    pi = jnp.zeros(t.shape, jnp.float32)
    br, bi = ab_re, ab_im
    for j in range(max(1, int(t_max).bit_length())):
        bit = ((t >> j) & 1) == 1
        pr, pi = jnp.where(bit, pr * br - pi * bi, pr), jnp.where(bit, pr * bi + pi * br, pi)
        br, bi = br * br - bi * bi, 2.0 * br * bi
    return pr, pi


def _s5ops_kernel(ldt_ref, ar_row, ai_row, ar_col, ai_col, bt_re, bt_im, ct_re, ct_im, til_ref,
                  toep_ref, pm_ref, q_ref, lvl_ref, *, lc):
    n_c, n_p = bt_re.shape[1], bt_re.shape[2]
    lw = lc * n_c
    dt = jnp.exp(ldt_ref[0])

    abr_re, abr_im, q_re, q_im = _s5_discretise(ar_row[0], ai_row[0], dt)
    bbt_re = q_re * bt_re[0] - q_im * bt_im[0]
    bbt_im = q_re * bt_im[0] + q_im * bt_re[0]
    s_row = lax.broadcasted_iota(jnp.int32, (lw, n_p), 0) // n_c
    pw_re, pw_im = _s5_power(abr_re, abr_im, lc - 1 - s_row, lc)
    tb_re = jnp.concatenate([bbt_re] * lc, axis=0)
    tb_im = jnp.concatenate([bbt_im] * lc, axis=0)
    pm_ref[0] = _bf(jnp.concatenate([pw_re * tb_re - pw_im * tb_im,
                                     pw_re * tb_im + pw_im * tb_re], axis=1))

    abc_re, abc_im, _, _ = _s5_discretise(ar_col[0], ai_col[0], dt)
    til = til_ref[...]
    c_re = _dot(ct_re[0], til, HI)
    c_im = _dot(ct_im[0], til, HI)
    t_lane = lax.broadcasted_iota(jnp.int32, (n_p, lw), 1) // n_c
    p0_re, p0_im = _s5_power(abc_re, abc_im, t_lane, lc)
    p1_re, p1_im = p0_re * abc_re - p0_im * abc_im, p0_re * abc_im + p0_im * abc_re
    ca0_re, ca0_im = p0_re * c_re - p0_im * c_im, p0_re * c_im + p0_im * c_re
    q_ref[0] = _bf(jnp.concatenate([p1_re * c_re - p1_im * c_im,
                                    -(p1_re * c_im + p1_im * c_re)], axis=0))
    r0 = _dot(bbt_re, ca0_re, HI) - _dot(bbt_im, ca0_im, HI)
    lane = lax.broadcasted_iota(jnp.int32, (n_c, lw), 1)
    for s in range(lc):
        blk = r0 if s == 0 else jnp.where(lane >= s * n_c, pltpu.roll(r0, s * n_c, axis=1), 0.0)
        toep_ref[0, s * n_c:(s + 1) * n_c, :] = _bf(blk)

    cr, ci = _s5_power(abr_re, abr_im, jnp.full(abr_re.shape, lc, jnp.int32), lc)
    for j in range(lvl_ref.shape[1]):
        lvl_ref[0, j] = jnp.concatenate([jnp.concatenate([cr, cr], axis=1),
                                         jnp.concatenate([-ci, ci], axis=1)], axis=0)
        cr, ci = cr * cr - ci * ci, 2.0 * cr * ci


def s5_operators(log_dt, a_re, a_im, b_re, b_im, c_re, c_im, n_chunks, lc):
    n_g, n_p = a_re.shape
    n_c = b_re.shape[-1]
    lw = lc * n_c
    n_lvl = max(1, int(math.ceil(math.log2(n_chunks))))
    til = (jnp.arange(lw)[None, :] % n_c == jnp.arange(n_c)[:, None]).astype(jnp.float32)
    t3 = lambda a: jnp.swapaxes(a, 1, 2)
    args = [log_dt.reshape(n_g, 1, 1), a_re.reshape(n_g, 1, n_p), a_im.reshape(n_g, 1, n_p),
            a_re.reshape(n_g, n_p, 1), a_im.reshape(n_g, n_p, 1),
            t3(b_re), t3(b_im), t3(c_re), t3(c_im)]
    per_g = lambda a: pl.BlockSpec((1,) + a.shape[1:], lambda g: (g,) + (0,) * (a.ndim - 1))
    return pl.pallas_call(
        functools.partial(_s5ops_kernel, lc=lc),
        grid=(n_g,),
        in_specs=[per_g(a) for a in args] + [pl.BlockSpec(til.shape, lambda g: (0, 0))],
        out_specs=[pl.BlockSpec((1, lw, lw), lambda g: (g, 0, 0)),
                   pl.BlockSpec((1, lw, 2 * n_p), lambda g: (g, 0, 0)),
                   pl.BlockSpec((1, 2 * n_p, lw), lambda g: (g, 0, 0)),
                   pl.BlockSpec((1, n_lvl, 2, 2 * n_p), lambda g: (g, 0, 0, 0))],
        out_shape=[jax.ShapeDtypeStruct((n_g, lw, lw), jnp.bfloat16),
                   jax.ShapeDtypeStruct((n_g, lw, 2 * n_p), jnp.bfloat16),
                   jax.ShapeDtypeStruct((n_g, 2 * n_p, lw), jnp.bfloat16),
                   jax.ShapeDtypeStruct((n_g, n_lvl, 2, 2 * n_p), jnp.float32)],
        compiler_params=_cparams("arbitrary"),
        name="s5ops",
    )(*args, til)


def _s5_kernel(u_ref, toep_ref, pm_ref, q_ref, lvl_ref, y_ref, *, n_chunks):
    u = _bf(u_ref[0])
    e = _dot(u, pm_ref[0])
    rows, two_p = e.shape
    half = two_p // 2
    cidx = lax.broadcasted_iota(jnp.int32, e.shape, 0) % n_chunks
    x = e
    n_lvl = lvl_ref.shape[1]
    for j in range(n_lvl):
        sh = 1 << j
        if sh >= n_chunks:
            break
        xs = jnp.where(cidx >= sh, pltpu.roll(x, sh, axis=0), 0.0)
        cf = lvl_ref[0, j]
        x = x + xs * cf[0:1, :] + pltpu.roll(xs, half, axis=1) * cf[1:2, :]
    x_in = jnp.where(cidx >= 1, pltpu.roll(x, 1, axis=0), 0.0)
    x_hi = _bf(x_in)
    x_lo = _bf(x_in - x_hi.astype(jnp.float32))
    q = q_ref[0]
    y_ref[0] = _dot(u, toep_ref[0]) + _dot(x_hi, q) + _dot(x_lo, q)


def _to_groups_kernel(u_ref, o_ref, *, lc, n_c):
    per, nb, lw = o_ref.shape
    slot = lax.broadcasted_iota(jnp.int32, (nb, LANES), 1) // n_c
    a = [u_ref[pl.ds(s, nb, stride=lc), :] for s in range(lc)]
    for gi in range(per):
        for j in range(lw // LANES):
            acc = None
            for ai in range(per):
                shift = ((ai - gi) % per) * n_c
                src = a[j * per + ai]
                piece = src if shift == 0 else pltpu.roll(src, shift, axis=1)
                acc = piece if acc is None else jnp.where(slot == ai, piece, acc)
            o_ref[gi, :, j * LANES:(j + 1) * LANES] = acc


def _from_groups_kernel(y_ref, o_ref, *, lc, n_c):
    per, nb, lw = y_ref.shape
    slot = lax.broadcasted_iota(jnp.int32, (nb, LANES), 1) // n_c
    for s in range(lc):
        j, ai = divmod(s, per)
        acc = None
        for gi in range(per):
            shift = ((gi - ai) % per) * n_c
            src = y_ref[gi, :, j * LANES:(j + 1) * LANES]
            piece = src if shift == 0 else pltpu.roll(src, shift, axis=1)
            acc = piece if acc is None else jnp.where(slot == gi, piece, acc)
        o_ref[pl.ds(s, nb, stride=lc), :] = acc


def _group_relayout(x, n_g, lc, to_groups, tile):
    if to_groups:
        n_tok, width = x.shape
    else:
        n_tok, width = x.shape[1] * lc, x.shape[2] // lc * n_g
    n_c = width // n_g
    per = LANES // n_c
    nb = tile // lc
    tok_spec = pl.BlockSpec((tile, LANES), lambda i, vb: (i, vb))
    grp_spec = pl.BlockSpec((per, nb, lc * n_c), lambda i, vb: (vb, i, 0))
    kern = _to_groups_kernel if to_groups else _from_groups_kernel
    out_shape = (n_g, n_tok // lc, lc * n_c) if to_groups else (n_tok, width)
    return pl.pallas_call(
        functools.partial(kern, lc=lc, n_c=n_c),
        grid=(n_tok // tile, n_g // per),
        in_specs=[tok_spec if to_groups else grp_spec],
        out_specs=grp_spec if to_groups else tok_spec,
        out_shape=jax.ShapeDtypeStruct(out_shape, x.dtype),
        compiler_params=_cparams("arbitrary", "arbitrary"),
        name="to_groups" if to_groups else "from_groups",
    )(x)


def s5_core(u, log_dt, a_re, a_im, b_re, b_im, c_re, c_im):
    bsz, seq, width = u.shape
    n_g, n_p = a_re.shape
    n_c = width // n_g
    lc = min(S5_CHUNK, seq)
    n_chunks = seq // lc
    toep, pm, q, lvl = s5_operators(log_dt, a_re, a_im, b_re, b_im, c_re, c_im, n_chunks, lc)
    tile = min(2048, bsz * seq)
    ug = _group_relayout(u.reshape(bsz * seq, width), n_g, lc, True, tile)
    rows, lw = bsz * n_chunks, lc * n_c
    yg = pl.pallas_call(
        functools.partial(_s5_kernel, n_chunks=n_chunks),
        grid=(n_g,),
        in_specs=[pl.BlockSpec((1, rows, lw), lambda g: (g, 0, 0)),
                  pl.BlockSpec((1, lw, lw), lambda g: (g, 0, 0)),
                  pl.BlockSpec((1, lw, 2 * n_p), lambda g: (g, 0, 0)),
                  pl.BlockSpec((1, 2 * n_p, lw), lambda g: (g, 0, 0)),
                  pl.BlockSpec((1,) + lvl.shape[1:], lambda g: (g, 0, 0, 0))],
        out_specs=pl.BlockSpec((1, rows, lw), lambda g: (g, 0, 0)),
        out_shape=jax.ShapeDtypeStruct((n_g, rows, lw), jnp.float32),
        compiler_params=_cparams("arbitrary"),
        name="s5",
    )(ug, toep, pm, q, lvl)
    return _group_relayout(yg, n_g, lc, False, tile).reshape(bsz, seq, width)


def _gelu_tanh(y):
    return 0.5 * y * (1.0 + jnp.tanh(math.sqrt(2.0 / math.pi) * (y + 0.044715 * (y * y * y))))


def _outproj_kernel(yr_ref, ys_ref, u_ref, x_ref, mods_ref, d_ref, wglu_ref, bglu_ref, wout_ref,
                    gpost_ref, gpre_ref, wrt_ref, x1_ref, h2_ref, lg_ref):
    m = mods_ref[0]
    yr = yr_ref[...]
    y5 = _gelu_tanh(ys_ref[...] + d_ref[...] * u_ref[...])
    y5 = y5 * _sigmoid(_dot(_bf(y5), wglu_ref[...]) + bglu_ref[...])
    wr = yr.shape[-1]
    mixed = _dot(_bf(yr), wout_ref[0:wr, :]) + _dot(_bf(y5), wout_ref[wr:, :])
    x1 = x_ref[...] + m[2:3, :] * _rms(mixed, gpost_ref[...])
    x1_ref[...] = x1
    h2 = _rms(x1, gpre_ref[...]) * (1.0 + m[4:5, :]) + m[3:4, :]
    half = h2.shape[-1] // 2
    _store_row_tiles(h2_ref, _pack_bf16_pair(h2[:, :half], h2[:, half:]))
    w_hi, w_lo = wrt_ref[0], wrt_ref[1]
    h_hi = _bf(h2)
    h_lo = _bf(h2 - h_hi.astype(jnp.float32))
    lg_ref[...] = _dot_nt(w_hi, h_hi) + (_dot_nt(w_hi, h_lo) + _dot_nt(w_lo, h_hi))


def out_proj(y_rwkv, y_s5, u, x, mods3, s5_d, w_glu_bf, b_glu, w_out_bf, g_post, g_pre, w_router_t,
             seq, tm):
    n_tok, dm = x.shape
    wr, ws = y_rwkv.shape[-1], y_s5.shape[-1]
    n_e = w_router_t.shape[0]
    tpb = seq // tm
    tok = lambda w: pl.BlockSpec((tm, w), lambda i: (i, 0))
    full = lambda a: pl.BlockSpec(a.shape, lambda i: (0,) * a.ndim)
    row = lambda t: t.reshape(1, -1)
    wr_hi = _bf(w_router_t)
    wr_split = jnp.stack([wr_hi, _bf(w_router_t - wr_hi.astype(jnp.float32))])
    consts = [row(s5_d), w_glu_bf, row(b_glu), w_out_bf, row(g_post), row(g_pre), wr_split]
    return pl.pallas_call(
        _outproj_kernel,
        grid=(n_tok // tm,),
        in_specs=[tok(wr), tok(ws), tok(ws), tok(dm),
                  pl.BlockSpec((1, N_MODS, dm), lambda i: (i // tpb, 0, 0))] + [full(a) for a in consts],
        out_specs=[tok(dm), _row_tile_spec(tm, dm // 2, lambda i: i),
                   pl.BlockSpec((n_e, tm), lambda i: (0, i))],
        out_shape=[jax.ShapeDtypeStruct((n_tok, dm), jnp.float32),
                   jax.ShapeDtypeStruct((n_tok * (dm // 2 // LANES), LANES), jnp.uint32),
                   jax.ShapeDtypeStruct((n_e, n_tok), jnp.float32)],
        compiler_params=_cparams("arbitrary"),
        name="outproj",
    )(y_rwkv, y_s5, u, x, mods3, *consts)


def _route_kernel(lg_ref, bias_ref, tri_ref, e_ref, w_ref, r_ref, cnt_ref, carry_ref):
    i = pl.program_id(0)

    @pl.when(i == 0)
    def _():
        carry_ref[...] = jnp.zeros_like(carry_ref)

    neg = -jnp.inf
    scores = _sigmoid(lg_ref[...])
    n_e, tm = scores.shape
    choice = scores + bias_ref[...]
    gsz = n_e // ROUTE_GROUPS
    c3 = choice.reshape(ROUTE_GROUPS, gsz, tm)
    io = lax.broadcasted_iota(jnp.int32, c3.shape, 1)
    m1 = jnp.max(c3, axis=1, keepdims=True)
    first = jnp.min(jnp.where(c3 == m1, io, gsz), axis=1, keepdims=True)
    m2 = jnp.max(jnp.where(io == first, neg, c3), axis=1, keepdims=True)
    gs = m1 + m2
    gi = lax.broadcasted_iota(jnp.int32, gs.shape, 0)
    rank = jnp.zeros(gs.shape, jnp.int32)
    for j in range(ROUTE_GROUPS):
        gj = gs[j:j + 1]
        beats = (gj > gs) | ((gj == gs) & (gi > j))
        rank = rank + beats.astype(jnp.int32)
    masked = jnp.where(rank < ROUTE_TOPK_GROUPS, c3, neg).reshape(n_e, tm)

    eio = lax.broadcasted_iota(jnp.int32, (n_e, tm), 0)
    ids, ws = [], []
    mhot = jnp.zeros((n_e, tm), jnp.float32)
    for _ in range(TOP_K):
        m = jnp.max(masked, axis=0, keepdims=True)
        idx = jnp.min(jnp.where(masked == m, eio, n_e), axis=0, keepdims=True)
        sel = eio == idx
        ws.append(jnp.sum(jnp.where(sel, scores, 0.0), axis=0, keepdims=True))
        ids.append(idx)
        masked = jnp.where(sel, neg, masked)
        mhot = jnp.where(sel, 1.0, mhot)
    wsum = ws[0]
    for t in ws[1:]:
        wsum = wsum + t
    before = _dot(_bf(mhot), tri_ref[...]) + carry_ref[...]
    ranks = [jnp.sum(jnp.where(eio == idx, before, 0.0), axis=0, keepdims=True) for idx in ids]
    e_ref[...] = jnp.concatenate(ids, axis=0)
    w_ref[...] = jnp.concatenate(ws, axis=0) / wsum * ROUTE_SCALE
    r_ref[...] = jnp.concatenate(ranks, axis=0).astype(jnp.int32)
    carry_ref[...] = carry_ref[...] + jnp.sum(mhot, axis=1, keepdims=True)
    cnt_ref[...] = carry_ref[...]


def route(logits_t, router_bias, tm):
    n_e, n_tok = logits_t.shape
    tri = (jnp.arange(tm)[:, None] < jnp.arange(tm)[None, :]).astype(jnp.bfloat16)
    kt = lambda: pl.BlockSpec((TOP_K, tm), lambda i: (0, i))
    return pl.pallas_call(
        _route_kernel,
        grid=(n_tok // tm,),
        in_specs=[pl.BlockSpec((n_e, tm), lambda i: (0, i)),
                  pl.BlockSpec((n_e, 1), lambda i: (0, 0)),
                  pl.BlockSpec((tm, tm), lambda i: (0, 0))],
        out_specs=[kt(), kt(), kt(), pl.BlockSpec((n_e, 1), lambda i: (0, 0))],
        out_shape=[jax.ShapeDtypeStruct((TOP_K, n_tok), jnp.int32),
                   jax.ShapeDtypeStruct((TOP_K, n_tok), jnp.float32),
                   jax.ShapeDtypeStruct((TOP_K, n_tok), jnp.int32),
                   jax.ShapeDtypeStruct((n_e, 1), jnp.float32)],
        scratch_shapes=[pltpu.VMEM((n_e, 1), jnp.float32)],
        compiler_params=_cparams("arbitrary"),
        name="route",
    )(logits_t, router_bias.reshape(n_e, 1), tri)


def _dest_kernel(e_ref, r_ref, ps_ref, d_ref):
    n_e = ps_ref.shape[0]
    tm = e_ref.shape[1]
    eio = lax.broadcasted_iota(jnp.int32, (n_e, tm), 0)
    ps = ps_ref[...]
    rows = [jnp.sum(jnp.where(eio == e_ref[k:k + 1, :], ps, 0.0), axis=0, keepdims=True)
            for k in range(TOP_K)]
    d_ref[...] = jnp.concatenate(rows, axis=0).astype(jnp.int32) + r_ref[...]


def dest_rows(e_idx, rank, pad_start, tm):
    n_tok = e_idx.shape[1]
    n_e = pad_start.shape[0]
    kt = lambda: pl.BlockSpec((TOP_K, tm), lambda i: (0, i))
    return pl.pallas_call(
        _dest_kernel,
        grid=(n_tok // tm,),
        in_specs=[kt(), kt(), pl.BlockSpec((n_e, 1), lambda i: (0, 0))],
        out_specs=kt(),
        out_shape=jax.ShapeDtypeStruct((TOP_K, n_tok), jnp.int32),
        compiler_params=_cparams("arbitrary"),
        name="dest",
    )(e_idx, rank, pad_start.astype(jnp.float32).reshape(n_e, 1))


def _dispatch_kernel(fill_start_ref, fill_len_ref, nu_ref, dest_ref, h_ref, xs_ref, zeros, sem, zsem,
                     *, n_s, bm, experts_per_step, tiles_per_step, n_tiles):
    i = pl.program_id(0)
    td = dest_ref.shape[1]

    @pl.when(i == 0)
    def _():
        zeros[...] = jnp.zeros_like(zeros)

    def body(t, carry):
        for k in range(TOP_K):
            pltpu.make_async_copy(_row(h_ref, t, n_s), _row(xs_ref, dest_ref[k, t], n_s),
                                  sem).start(priority=k % 2)
        return carry

    lax.fori_loop(0, td, body, 0)

    fills = []
    for j in range(experts_per_step):
        e = i * experts_per_step + j
        start, length = fill_start_ref[e], fill_len_ref[e]
        piece = bm // 2
        while piece >= 1:
            off = start + (length & ~(2 * piece - 1))
            fills.append(((length & piece) != 0,
                          pltpu.make_async_copy(zeros.at[pl.ds(0, piece * n_s), :],
                                                xs_ref.at[pl.ds(off * n_s, piece * n_s), :], zsem)))
            piece //= 2
    for j in range(tiles_per_step):
        tile = i * tiles_per_step + j
        fills.append(((tile >= nu_ref[0]) & (tile < n_tiles),
                      pltpu.make_async_copy(zeros, xs_ref.at[pl.ds(tile * (bm * n_s), bm * n_s), :], zsem)))
    for cond, cp in fills:
        pl.when(cond)(cp.start)

    for _ in range(TOP_K):
        pltpu.make_async_copy(h_ref, xs_ref.at[pl.ds(0, td * n_s), :], sem).wait()
    for cond, cp in fills:
        pl.when(cond)(cp.wait)


def dispatch(dest, h2p, fill_start, fill_len, n_used, n_tok, n_tiles, td, bm):
    n_s = h2p.shape[0] // n_tok
    n_steps = n_tok // td
    n_e = fill_start.shape[0]
    experts_per_step = -(-n_e // n_steps)
    tiles_per_step = -(-n_tiles // n_steps)
    pad = n_steps * experts_per_step - n_e
    fill_start = jnp.pad(fill_start, (0, pad))
    fill_len = jnp.pad(fill_len, (0, pad))
    return pl.pallas_call(
        functools.partial(_dispatch_kernel, n_s=n_s, bm=bm, experts_per_step=experts_per_step,
                          tiles_per_step=tiles_per_step, n_tiles=n_tiles),
        grid_spec=pltpu.PrefetchScalarGridSpec(
            num_scalar_prefetch=3,
            grid=(n_steps,),
            in_specs=[pl.BlockSpec((TOP_K, td), lambda i, *_: (0, i), memory_space=pltpu.SMEM),
                      _row_tile_spec(td, n_s * LANES, lambda i, *_: i)],
            out_specs=pl.BlockSpec(memory_space=pl.ANY),
            scratch_shapes=[pltpu.VMEM((bm * n_s, LANES), h2p.dtype),
                            pltpu.SemaphoreType.DMA(()), pltpu.SemaphoreType.DMA(())]),
        out_shape=jax.ShapeDtypeStruct((n_tiles * bm * n_s, LANES), h2p.dtype),
        compiler_params=_cparams("arbitrary"),
        name="dispatch",
    )(fill_start, fill_len, n_used, dest, h2p)


def _expert_kernel(te_ref, nu_ref, nt_ref, x_ref, wg_hbm, wu_hbm, wd_hbm, o_ref,
                   wg_st, wu_st, wd_st, wgu_bf, wd_bf, grp, sems, *, bm):
    i = pl.program_id(0)
    ff = wd_bf.shape[0]
    n_used = nu_ref[0]

    def fetch(e, slot):
        return (pltpu.make_async_copy(wg_hbm.at[e], wg_st.at[slot], sems.at[slot]),
                pltpu.make_async_copy(wu_hbm.at[e], wu_st.at[slot], sems.at[slot]),
                pltpu.make_async_copy(wd_hbm.at[e], wd_st.at[slot], sems.at[slot]))

    @pl.when(i == 0)
    def _():
        grp[0] = 0
        for cp in fetch(te_ref[0], 0):
            cp.start()

    @pl.when(i < n_used)
    def _():
        e = te_ref[i]

        @pl.when((i == 0) | (e != te_ref[jnp.maximum(i - 1, 0)]))
        def _():
            slot = grp[0] & 1
            grp[0] = grp[0] + 1
            for cp in fetch(e, slot):
                cp.wait()
            nxt = i + nt_ref[e]

            @pl.when(nxt < n_used)
            def _():
                for cp in fetch(te_ref[jnp.minimum(nxt, n_used - 1)], 1 - slot):
                    cp.start()

            wgu_bf[:, :ff] = _bf(wg_st[slot])
            wgu_bf[:, ff:] = _bf(wu_st[slot])
            wd_bf[...] = _bf(wd_st[slot])

        hi, lo = _unpack_bf16_pair(_load_row_tiles(x_ref, bm))
        x = jnp.concatenate([hi, lo], axis=1)
        gu = _dot(x, wgu_bf[...])
        hid = _silu(gu[:, :ff]) * gu[:, ff:]
        y = _dot(_bf(hid), wd_bf[...])
        half = y.shape[1] // 2
        _store_row_tiles(o_ref, _pack_bf16_pair(y[:, :half], y[:, half:]))

    @pl.when(i >= nu_ref[0])
    def _():
        o_ref[...] = jnp.zeros_like(o_ref)


def expert_ffn(tile_e, n_used, tiles_per_expert, xs, n_rows, w_gate, w_up, w_down, bm):
    dm = w_gate.shape[1]
    half = dm // 2
    ff = w_gate.shape[-1]
    rows_in = _row_tile_spec(bm, half, lambda i, te, nu, nt: jnp.minimum(i, nu[0] - 1))
    rows_out = _row_tile_spec(bm, half, lambda i, te, nu, nt: i)
    hbm = pl.BlockSpec(memory_space=pl.ANY)
    return pl.pallas_call(
        functools.partial(_expert_kernel, bm=bm),
        grid_spec=pltpu.PrefetchScalarGridSpec(
            num_scalar_prefetch=3,
            grid=(n_rows // bm,),
            in_specs=[rows_in, hbm, hbm, hbm],
            out_specs=rows_out,
            scratch_shapes=[pltpu.VMEM((2, dm, ff), jnp.float32),
                            pltpu.VMEM((2, dm, ff), jnp.float32),
                            pltpu.VMEM((2, ff, dm), jnp.float32),
                            pltpu.VMEM((dm, 2 * ff), jnp.bfloat16),
                            pltpu.VMEM((ff, dm), jnp.bfloat16),
                            pltpu.SMEM((1,), jnp.int32),
                            pltpu.SemaphoreType.DMA((2,))]),
        out_shape=jax.ShapeDtypeStruct((n_rows * (half // LANES), LANES), jnp.uint32),
        compiler_params=_cparams("arbitrary"),
        name="expert",
    )(tile_e, n_used, tiles_per_expert, xs, w_gate, w_up, w_down)


def _combine_kernel(dest_ref, dnext_ref, w_ref, x1_ref, h2_ref, mods_ref, g_ref, wsg_ref, wsu_ref,
                    wsd_ref, ys_ref, o_ref, gbuf, rbuf, wrep, sems, *, n_s):
    i = pl.program_id(0)
    n_steps = pl.num_programs(0)
    tc = x1_ref.shape[0]
    slot = i % 2

    def gather(d_ref, slot_, t):
        for k in range(TOP_K):
            pltpu.make_async_copy(_row(ys_ref, d_ref[k, t], n_s), _row(gbuf.at[slot_, k], t, n_s),
                                  sems.at[slot_]).start(priority=k % 2)

    def weighted(t):
        acc_hi = acc_lo = None
        for k in range(TOP_K):
            wk = jnp.broadcast_to(wrep[k, pl.ds(t, 1), :], (n_s, LANES))
            hi, lo = _unpack_pair_f32(_row(gbuf.at[slot, k], t, n_s)[...])
            t_hi, t_lo = wk * hi, wk * lo
            acc_hi = t_hi if acc_hi is None else acc_hi + t_hi
            acc_lo = t_lo if acc_lo is None else acc_lo + t_lo
        _row(rbuf.at[0], t, n_s)[...] = acc_hi
        _row(rbuf.at[1], t, n_s)[...] = acc_lo

    @pl.when(i == 0)
    def _():
        def body(t, carry):
            gather(dest_ref, 0, t)
            return carry
        lax.fori_loop(0, tc, body, 0)

    w = w_ref[...]
    for k in range(TOP_K):
        wrep[k] = jnp.broadcast_to(w[:, k:k + 1], (tc, LANES))

    for k in range(TOP_K):
        pltpu.make_async_copy(ys_ref.at[pl.ds(0, tc * n_s), :], gbuf.at[slot, k], sems.at[slot]).wait()

    @pl.when(i + 1 < n_steps)
    def _():
        def body(t, carry):
            gather(dnext_ref, 1 - slot, t)
            weighted(t)
            return carry
        lax.fori_loop(0, tc, body, 0)

    @pl.when(i + 1 == n_steps)
    def _():
        def body(t, carry):
            weighted(t)
            return carry
        lax.fori_loop(0, tc, body, 0)

    hi, lo = _unpack_bf16_pair(_load_row_tiles(h2_ref, tc))
    h = jnp.concatenate([hi, lo], axis=1)
    hid = _silu(_dot(h, wsg_ref[...])) * _dot(h, wsu_ref[...])
    routed = jnp.concatenate([_load_row_tiles(rbuf.at[0], tc), _load_row_tiles(rbuf.at[1], tc)], axis=1)
    y = _dot(_bf(hid), wsd_ref[...]) + routed
    m = mods_ref[0]
    o_ref[...] = x1_ref[...] + m[5:6, :] * _rms(y, g_ref[...])


def combine(dest, w_tk, x1, h2p, mods3, g_post, ws_gate_bf, ws_up_bf, ws_down_bf, ys, seq, tc):
    n_tok, dm = x1.shape
    n_s = dm // 2 // LANES
    n_steps = n_tok // tc
    tpb = seq // tc
    full = lambda a: pl.BlockSpec(a.shape, lambda i: (0,) * a.ndim)
    tok = lambda w: pl.BlockSpec((tc, w), lambda i: (i, 0))
    g2 = g_post.reshape(1, dm)
    return pl.pallas_call(
        functools.partial(_combine_kernel, n_s=n_s),
        grid=(n_steps,),
        in_specs=[pl.BlockSpec((TOP_K, tc), lambda i: (0, i), memory_space=pltpu.SMEM),
                  pl.BlockSpec((TOP_K, tc), lambda i: (0, jnp.minimum(i + 1, n_steps - 1)),
                               memory_space=pltpu.SMEM),
                  tok(TOP_K),
                  tok(dm), _row_tile_spec(tc, dm // 2, lambda i: i),
                  pl.BlockSpec((1, N_MODS, dm), lambda i: (i // tpb, 0, 0)),
                  full(g2), full(ws_gate_bf), full(ws_up_bf), full(ws_down_bf),
                  pl.BlockSpec(memory_space=pl.ANY)],
        out_specs=tok(dm),
        out_shape=jax.ShapeDtypeStruct((n_tok, dm), jnp.float32),
        scratch_shapes=[pltpu.VMEM((2, TOP_K, tc * n_s, LANES), jnp.uint32),
                        pltpu.VMEM((2, tc * n_s, LANES), jnp.float32),
                        pltpu.VMEM((TOP_K, tc, LANES), jnp.float32),
                        pltpu.SemaphoreType.DMA((2,))],
        compiler_params=_cparams("arbitrary"),
        name="combine",
    )(dest, dest, w_tk, x1, h2p, mods3, g2, ws_gate_bf, ws_up_bf, ws_down_bf, ys)


def moe_ffn(h2, x1, logits_t, mods3, router_bias, w_gate, w_up, w_down, ws_gate, ws_up, ws_down,
            g_post, seq):
    n_tok, dm = x1.shape
    n_e = w_gate.shape[0]
    bm = EXPERT_ROW_TILE
    e_idx, w_kt, rank, cnt = route(logits_t, router_bias, min(512, n_tok))
    counts = cnt[:, 0].astype(jnp.int32)
    padded = (counts + bm - 1) // bm * bm
    pad_end = jnp.cumsum(padded)
    pad_start = (pad_end - padded).astype(jnp.int32)
    n_tiles = -(-(n_tok * TOP_K) // bm) + n_e
    n_used = (pad_end[-1] // bm).astype(jnp.int32)
    tile_start = jnp.arange(n_tiles, dtype=jnp.int32) * bm
    tile_e = jnp.minimum(jnp.sum((pad_end[None, :] <= tile_start[:, None]).astype(jnp.int32), axis=1),
                         n_e - 1).astype(jnp.int32)
    tile_e = jnp.where(jnp.arange(n_tiles) < n_used, tile_e, tile_e[n_used - 1])
    dest = dest_rows(e_idx, rank, pad_start, min(512, n_tok))
    n_used1 = n_used.reshape(1)
    xs = dispatch(dest, h2, pad_start + counts, padded - counts, n_used1, n_tok, n_tiles,
                  min(256, n_tok), bm)
    ys = expert_ffn(tile_e, n_used1, (padded // bm).astype(jnp.int32), xs, n_tiles * bm,
                    w_gate, w_up, w_down, bm)
    bf = lambda a: a.astype(jnp.bfloat16)
    return combine(dest, w_kt.T, x1, h2, mods3, g_post, bf(ws_gate), bf(ws_up), bf(ws_down), ys,
                   seq, min(256, n_tok))


def kernel(x, c, w_ada, b_ada, g_pre_mix, g_post_mix, g_pre_ffn, g_post_ffn, w_in, mu_shift, rwkv_w0, rwkv_w2, rwkv_a0, rwkv_a2, rwkv_g2, rwkv_k_k, rwkv_k_a, rwkv_r_k, rwkv_ln_w, rwkv_ln_b, s5_log_dt, s5_a_re, s5_a_im, s5_b_re, s5_b_im, s5_c_re, s5_c_im, s5_d, s5_w_glu, s5_b_glu, w_out, w_router, router_bias, w_gate, w_up, w_down, ws_gate, ws_up, ws_down):
    bsz, seq, dm = x.shape
    depth = w_ada.shape[0]
    bf = lambda a: a.astype(jnp.bfloat16)
    tm = min(512, seq)
    for l in range(depth):
        mods3 = ada_mods(c, w_ada[l], b_ada[l]).reshape(bsz, N_MODS, dm)
        n_rwkv = mu_shift.shape[-1]
        p, u = in_proj(x, mods3, g_pre_mix[l], bf(w_in[l]), n_rwkv, tm)
        y_rwkv = rwkv_mixer(p, mu_shift[l], rwkv_w0[l], rwkv_w2[l], rwkv_a0[l], rwkv_a2[l],
                            rwkv_g2[l], rwkv_k_k[l], rwkv_k_a[l], rwkv_r_k[l].reshape(-1),
                            rwkv_ln_w[l], rwkv_ln_b[l])
        y_s5 = s5_core(u, s5_log_dt[l], s5_a_re[l], s5_a_im[l], s5_b_re[l], s5_b_im[l],
                       s5_c_re[l], s5_c_im[l])
        flat = lambda a: a.reshape(bsz * seq, a.shape[-1])
        x1, h2, logits_t = out_proj(flat(y_rwkv), flat(y_s5), flat(u), flat(x), mods3, s5_d[l],
                                    bf(s5_w_glu[l]), s5_b_glu[l], bf(w_out[l]), g_post_mix[l],
                                    g_pre_ffn[l], w_router[l].T, seq, tm)
        out = moe_ffn(h2, x1, logits_t, mods3, router_bias[l], w_gate[l], w_up[l], w_down[l],
                      ws_gate[l], ws_up[l], ws_down[l], g_post_ffn[l], seq)
        x = out.reshape(bsz, seq, dm)
    return x
```

```python
import functools
import math

import jax
import jax.numpy as jnp
from jax import lax
from jax.experimental import pallas as pl
from jax.experimental.pallas import tpu as pltpu

NORM_EPS = 1e-6
LNX_EPS = 64e-5
L2_EPS = 1e-12
S5_MAX_REAL = -1e-4
ROUTE_SCALE = 2.5
N_MODS = 6

RWKV_HEAD_DIM = 64
DECAY_LORA = 64
ICLR_LORA = 64
GATE_LORA = 128
S5_GROUP = 16
S5_STATE = 64
TOP_K = 8
ROUTE_GROUPS = 8
ROUTE_TOPK_GROUPS = 4

RWKV_CHUNK = 64
S5_CHUNK = 16
EXPERT_ROW_TILE = 512

VMEM_LIMIT = 56 * 1024 * 1024

HI = lax.Precision.HIGHEST


def _cparams(*sem):
    return pltpu.CompilerParams(dimension_semantics=sem, vmem_limit_bytes=VMEM_LIMIT)


def _dot(a, b, precision=None):
    return jnp.dot(a, b, preferred_element_type=jnp.float32, precision=precision)


def _dot_nt(a, b, precision=None):
    return lax.dot_general(a, b, (((1,), (1,)), ((), ())),
                           preferred_element_type=jnp.float32, precision=precision)


def _dot_tn(a, b, precision=None):
    return lax.dot_general(a, b, (((0,), (0,)), ((), ())),
                           preferred_element_type=jnp.float32, precision=precision)


def _bdot(a, b):
    return lax.dot_general(a, b, (((2,), (1,)), ((0,), (0,))), preferred_element_type=jnp.float32)


def _bdot_nt(a, b):
    return lax.dot_general(a, b, (((2,), (2,)), ((0,), (0,))), preferred_element_type=jnp.float32)


def _bf(x):
    return x.astype(jnp.bfloat16)


def _pack_bf16_pair(hi, lo):
    hb = lax.bitcast_convert_type(_bf(hi).astype(jnp.float32), jnp.uint32)
    lb = lax.bitcast_convert_type(_bf(lo).astype(jnp.float32), jnp.uint32)
    return (hb & jnp.uint32(0xFFFF0000)) | (lb >> 16)


def _unpack_pair_f32(w):
    hi = lax.bitcast_convert_type(w & jnp.uint32(0xFFFF0000), jnp.float32)
    lo = lax.bitcast_convert_type(w << 16, jnp.float32)
    return hi, lo


def _unpack_bf16_pair(w):
    hi, lo = _unpack_pair_f32(w)
    return _bf(hi), _bf(lo)


LANES = 128


def _row_tile_spec(rows, width, row_block):
    return pl.BlockSpec((rows * (width // LANES), LANES), lambda *a: (row_block(*a), 0))


def _store_row_tiles(ref, x):
    rows = x.shape[0]
    n_s = ref.shape[0] // rows
    for s in range(n_s):
        ref[pl.ds(s, rows, stride=n_s), :] = x[:, s * LANES:(s + 1) * LANES]


def _load_row_tiles(ref, rows):
    n_s = ref.shape[0] // rows
    return jnp.concatenate([ref[pl.ds(s, rows, stride=n_s), :] for s in range(n_s)], axis=1)


def _row(ref, r, n_s):
    return ref.at[pl.ds(pl.multiple_of(r * n_s, n_s), n_s), :]


def _sigmoid(x):
    return 1.0 / (1.0 + jnp.exp(-x))


def _silu(x):
    return x * _sigmoid(x)


def _rms(x, gain):
    return x * lax.rsqrt(jnp.mean(x * x, axis=-1, keepdims=True) + NORM_EPS) * gain


def _ada_kernel(c_ref, w_ref, b_ref, o_ref):
    c = c_ref[...]
    o_ref[...] = _dot(_silu(c), w_ref[...], HI) + b_ref[...]


def ada_mods(c, w_ada, b_ada):
    bsz, dm = c.shape
    n = w_ada.shape[1]
    tn = dm
    return pl.pallas_call(
        _ada_kernel,
        grid=(n // tn,),
        in_specs=[pl.BlockSpec((bsz, dm), lambda j: (0, 0)),
                  pl.BlockSpec((dm, tn), lambda j: (0, j)),
                  pl.BlockSpec((1, tn), lambda j: (0, j))],
        out_specs=pl.BlockSpec((bsz, tn), lambda j: (0, j)),
        out_shape=jax.ShapeDtypeStruct((bsz, n), jnp.float32),
        compiler_params=_cparams("arbitrary"),
        name="ada",
    )(c, w_ada, b_ada.reshape(1, n))


def _inproj_kernel(x_ref, mods_ref, g_ref, w_ref, p_ref, u_ref):
    x = x_ref[0]
    m = mods_ref[0]
    h = _rms(x, g_ref[...]) * (1.0 + m[1:2, :]) + m[0:1, :]
    proj = _dot(_bf(h), w_ref[...])
    n_p = p_ref.shape[-1]
    p_ref[0] = proj[:, :n_p]
    u_ref[0] = proj[:, n_p:]


def in_proj(x, mods3, g_pre, w_in_bf, n_rwkv, tm):
    bsz, seq, dm = x.shape
    n = w_in_bf.shape[1]
    n_s5 = n - n_rwkv
    return pl.pallas_call(
        _inproj_kernel,
        grid=(bsz, seq // tm),
        in_specs=[pl.BlockSpec((1, tm, dm), lambda b, i: (b, i, 0)),
                  pl.BlockSpec((1, N_MODS, dm), lambda b, i: (b, 0, 0)),
                  pl.BlockSpec((1, dm), lambda b, i: (0, 0)),
                  pl.BlockSpec((dm, n), lambda b, i: (0, 0))],
        out_specs=[pl.BlockSpec((1, tm, n_rwkv), lambda b, i: (b, i, 0)),
                   pl.BlockSpec((1, tm, n_s5), lambda b, i: (b, i, 0))],
        out_shape=[jax.ShapeDtypeStruct((bsz, seq, n_rwkv), jnp.float32),
                   jax.ShapeDtypeStruct((bsz, seq, n_s5), jnp.float32)],
        compiler_params=_cparams("arbitrary", "arbitrary"),
        name="inproj",
    )(x, mods3, g_pre.reshape(1, dm), w_in_bf)


RWKV_TILE = 256


def _split_dot(x, m01, terms):
    blk = m01.shape[0]
    pieces = []
    rem = x
    for _ in range(terms):
        piece = _bf(rem)
        pieces.append(piece)
        rem = rem - piece.astype(jnp.float32)
    cols = []
    for c in range(x.shape[1] // blk):
        acc = None
        for piece in pieces:
            part = _dot(piece[:, c * blk:(c + 1) * blk], m01)
            acc = part if acc is None else acc + part
        cols.append(acc)
    return cols[0] if len(cols) == 1 else jnp.concatenate(cols, axis=1)


def _rwkv_kernel(p_ref, mu_ref, w0_ref, w2_ref, a0_ref, a2_ref, g2_ref, kk_ref, ka_ref,
                  rk_ref, lnw_ref, lnb_ref, bd_ref, o_ref,
                  s_ref, carry_ref, rt_ref, at_ref, bt_ref, kt_ref, v_ref, wl_ref, y_ref,
                  *, width, chunk):
    hd = RWKV_HEAD_DIM
    pw = 2 * hd
    pairs = width // pw
    i = pl.program_id(1)

    @pl.when(i == 0)
    def _():
        s_ref[...] = jnp.zeros_like(s_ref)
        carry_ref[...] = jnp.zeros_like(carry_ref)

    p = p_ref[0]
    n_t = p.shape[0]
    n_chunks = n_t // chunk
    row = lax.broadcasted_iota(jnp.int32, p.shape, 0)
    prev = jnp.where(row == 0, carry_ref[0:1, :], pltpu.roll(p, 1, axis=0))
    carry_ref[0:1, :] = p[n_t - 1:n_t, :]
    pm = p + (prev - p) * mu_ref[...]

    r = pm[:, 0:width]
    k = pm[:, width:2 * width]
    v = pm[:, 2 * width:3 * width]
    c0 = 3 * width
    w_lo = pm[:, c0:c0 + DECAY_LORA]
    a_lo = pm[:, c0 + DECAY_LORA:c0 + DECAY_LORA + ICLR_LORA]
    g_lo = pm[:, c0 + DECAY_LORA + ICLR_LORA:]

    z = w0_ref[...] + _dot(_bf(jnp.tanh(w_lo)), w2_ref[...])
    softplus_neg = jnp.maximum(-z, 0.0) + jnp.log(1.0 + jnp.exp(-jnp.abs(z)))
    logd = -jnp.exp(-softplus_neg - 0.5)
    iclr = _sigmoid(a0_ref[...] + _dot(_bf(a_lo), a2_ref[...]))
    gate = _dot(_bf(_sigmoid(g_lo)), g2_ref[...])

    bd = bd_ref[...]
    kk = k * kk_ref[...]
    kk = kk / jnp.maximum(jnp.sqrt(_split_dot(kk * kk, bd, 2)), L2_EPS)
    k2 = k * (1.0 + (iclr - 1.0) * ka_ref[...])
    bonus = _split_dot(r * k2 * rk_ref[...], bd, 2) * v

    ti = lax.broadcasted_iota(jnp.int32, (n_t, n_t), 0)
    si = lax.broadcasted_iota(jnp.int32, (n_t, n_t), 1)
    tri = jnp.where((ti >= si) & (ti // chunk == si // chunk), 1.0, 0.0).astype(jnp.bfloat16)
    cum = _split_dot_lhs(tri, logd, 3)
    e_pos = jnp.exp(cum)
    e_neg = jnp.exp(-cum)
    rt_ref[...] = _bf(r * e_pos)
    at_ref[...] = _bf(-kk * jnp.exp(cum - logd))
    bt_ref[...] = _bf(kk * iclr * e_neg)
    kt_ref[...] = _bf(k2 * e_neg)
    v_ref[...] = _bf(v)
    for c in range(n_chunks):
        wl_ref[c:c + 1, :] = e_pos[(c + 1) * chunk - 1:(c + 1) * chunk, :]

    two_l = 2 * chunk
    lane = lax.broadcasted_iota(jnp.int32, (chunk, pw), 1)
    lane0 = lane < hd
    bi = lax.broadcasted_iota(jnp.int32, (two_l, two_l), 0)
    bj = lax.broadcasted_iota(jnp.int32, (two_l, two_l), 1)
    same = (bi // chunk) == (bj // chunk)
    low_strict = same & (bi > bj)
    low_incl = same & (bi >= bj)
    eye_t = jnp.where(bi == bj, 1.0, 0.0)
    pi_ = lax.broadcasted_iota(jnp.int32, (pw, pw), 0)
    pj_ = lax.broadcasted_iota(jnp.int32, (pw, pw), 1)
    eye_p = jnp.where(pi_ == pj_, 1.0, 0.0)
    n_sq = max(1, int(math.ceil(math.log2(chunk))) - 1)
    zero = jnp.zeros((), jnp.bfloat16)

    def stack(x):
        return jnp.concatenate([jnp.where(lane0, x, zero), jnp.where(lane0, zero, x)], axis=0)

    def gather(ref):
        return jnp.stack([stack(ref[c * chunk:(c + 1) * chunk, hp * pw:(hp + 1) * pw])
                          for c in range(n_chunks) for hp in range(pairs)], axis=0)

    a_s, b_s, k_s, r_s, v_s = (gather(ref) for ref in (at_ref, bt_ref, kt_ref, rt_ref, v_ref))
    wl = jnp.stack([wl_ref[c:c + 1, hp * pw:(hp + 1) * pw]
                    for c in range(n_chunks) for hp in range(pairs)], axis=0)
    gram = _bdot_nt(jnp.concatenate([a_s, r_s], axis=1), jnp.concatenate([b_s, k_s], axis=1))
    m_ab = jnp.where(low_strict, gram[:, :two_l, :two_l], 0.0)
    m_ak = jnp.where(low_strict, gram[:, :two_l, two_l:], 0.0)
    n_rb = jnp.where(low_incl, gram[:, two_l:, :two_l], 0.0)
    n_rk = jnp.where(low_incl, gram[:, two_l:, two_l:], 0.0)
    t_inv = eye_t + m_ab
    m_pow = _bf(m_ab)
    for _ in range(n_sq):
        m_pow = _bf(_bdot(m_pow, m_pow))
        t_inv = t_inv + _bdot(_bf(t_inv), m_pow)
    makv = _bdot(_bf(m_ak), v_s)
    tx_bf = _bf(_bdot(_bf(t_inv), jnp.concatenate([a_s, _bf(makv)], axis=2)))
    nx = _bdot(_bf(n_rb), tx_bf)
    rbar = _bf(r_s.astype(jnp.float32) + nx[:, :, :pw])
    y0 = nx[:, :, pw:] + _bdot(_bf(n_rk), v_s)
    tb = _bdot(jnp.swapaxes(tx_bf, 1, 2), b_s)
    pmat = _bf((eye_p + tb[:, :pw, :]) * wl)
    dmat = (tb[:, pw:, :] + _bdot(jnp.swapaxes(v_s, 1, 2), k_s)) * wl

    s = s_ref[...]
    for c in range(n_chunks):
        sel = slice(c * pairs, (c + 1) * pairs)
        s_bf = _bf(s)
        ys = _bdot_nt(rbar[sel], s_bf) + y0[sel]
        yc_ = ys[:, :chunk, :] + ys[:, chunk:, :]
        for hp in range(pairs):
            y_ref[c * chunk:(c + 1) * chunk, hp * pw:(hp + 1) * pw] = yc_[hp]
        s = _bdot(s_bf, pmat[sel]) + dmat[sel]
    s_ref[...] = s

    y = y_ref[...]
    inv_hd = 1.0 / hd
    mean = _split_dot(y, bd, 2) * inv_hd
    yc = y - mean
    var = _split_dot(yc * yc, bd, 2) * inv_hd
    yn = yc * lax.rsqrt(var + LNX_EPS) * lnw_ref[...] + lnb_ref[...]
    o_ref[0] = (yn + bonus) * gate


def _split_dot_lhs(m01, x, terms):
    acc = None
    rem = x
    for _ in range(terms):
        piece = _bf(rem)
        part = _dot(m01, piece)
        acc = part if acc is None else acc + part
        rem = rem - piece.astype(jnp.float32)
    return acc


def rwkv_mixer(p, mu, w0, w2, a0, a2, g2, k_k, k_a, r_k, ln_w, ln_b):
    bsz, seq, n_p = p.shape
    width = w0.shape[-1]
    pairs = width // (2 * RWKV_HEAD_DIM)
    chunk = min(RWKV_CHUNK, seq)
    tile = min(RWKV_TILE, seq)
    hid = jnp.arange(2 * RWKV_HEAD_DIM, dtype=jnp.int32) // RWKV_HEAD_DIM
    bd = (hid[:, None] == hid[None, :]).astype(jnp.bfloat16)
    row = lambda t: t.reshape(1, -1)
    full = lambda a: pl.BlockSpec(a.shape, lambda b, i: (0,) * a.ndim)
    consts = [row(mu), row(w0), _bf(w2), row(a0), _bf(a2), _bf(g2), row(k_k), row(k_a), row(r_k),
              row(ln_w), row(ln_b), bd]
    act = lambda: pltpu.VMEM((tile, width), jnp.bfloat16)
    return pl.pallas_call(
        functools.partial(_rwkv_kernel, width=width, chunk=chunk),
        grid=(bsz, seq // tile),
        in_specs=[pl.BlockSpec((1, tile, n_p), lambda b, i: (b, i, 0))] + [full(a) for a in consts],
        out_specs=pl.BlockSpec((1, tile, width), lambda b, i: (b, i, 0)),
        out_shape=jax.ShapeDtypeStruct((bsz, seq, width), jnp.float32),
        scratch_shapes=[pltpu.VMEM((pairs, 2 * RWKV_HEAD_DIM, 2 * RWKV_HEAD_DIM), jnp.float32),
                        pltpu.VMEM((8, n_p), jnp.float32),
                        act(), act(), act(), act(), act(),
                        pltpu.VMEM((max(8, tile // chunk), width), jnp.float32),
                        pltpu.VMEM((tile, width), jnp.float32)],
        compiler_params=_cparams("arbitrary", "arbitrary"),
        name="rwkv",
    )(p, *consts)


def _s5_discretise(a_re, a_im, dt):
    lam_re = jnp.minimum(a_re, S5_MAX_REAL)
    lam_im = a_im
    mag = jnp.exp(lam_re * dt)
    ang = lam_im * dt
    ab_re, ab_im = mag * jnp.cos(ang), mag * jnp.sin(ang)
    den = lam_re * lam_re + lam_im * lam_im
    n_re, n_im = ab_re - 1.0, ab_im
    q_re = (n_re * lam_re + n_im * lam_im) / den
    q_im = (n_im * lam_re - n_re * lam_im) / den
    return ab_re, ab_im, q_re, q_im


def _s5_power(ab_re, ab_im, t, t_max):
    pr = jnp.ones(t.shape, jnp.float32)
    pi = jnp.zeros(t.shape, jnp.float32)
    br, bi = ab_re, ab_im
    for j in range(max(1, int(t_max).bit_length())):
        bit = ((t >> j) & 1) == 1
        pr, pi = jnp.where(bit, pr * br - pi * bi, pr), jnp.where(bit, pr * bi + pi * br, pi)
        br, bi = br * br - bi * bi, 2.0 * br * bi
    return pr, pi


def _s5ops_kernel(ldt_ref, ar_row, ai_row, ar_col, ai_col, bt_re, bt_im, ct_re, ct_im, til_ref,
                  toep_ref, pm_ref, q_ref, lvl_ref, *, lc):
    n_c, n_p = bt_re.shape[1], bt_re.shape[2]
    lw = lc * n_c
    dt = jnp.exp(ldt_ref[0])

    abr_re, abr_im, q_re, q_im = _s5_discretise(ar_row[0], ai_row[0], dt)
    bbt_re = q_re * bt_re[0] - q_im * bt_im[0]
    bbt_im = q_re * bt_im[0] + q_im * bt_re[0]
    s_row = lax.broadcasted_iota(jnp.int32, (lw, n_p), 0) // n_c
    pw_re, pw_im = _s5_power(abr_re, abr_im, lc - 1 - s_row, lc)
    tb_re = jnp.concatenate([bbt_re] * lc, axis=0)
    tb_im = jnp.concatenate([bbt_im] * lc, axis=0)
    pm_ref[0] = _bf(jnp.concatenate([pw_re * tb_re - pw_im * tb_im,
                                     pw_re * tb_im + pw_im * tb_re], axis=1))

    abc_re, abc_im, _, _ = _s5_discretise(ar_col[0], ai_col[0], dt)
    til = til_ref[...]
    c_re = _dot(ct_re[0], til, HI)
    c_im = _dot(ct_im[0], til, HI)
    t_lane = lax.broadcasted_iota(jnp.int32, (n_p, lw), 1) // n_c
    p0_re, p0_im = _s5_power(abc_re, abc_im, t_lane, lc)
    p1_re, p1_im = p0_re * abc_re - p0_im * abc_im, p0_re * abc_im + p0_im * abc_re
    ca0_re, ca0_im = p0_re * c_re - p0_im * c_im, p0_re * c_im + p0_im * c_re
    q_ref[0] = _bf(jnp.concatenate([p1_re * c_re - p1_im * c_im,
                                    -(p1_re * c_im + p1_im * c_re)], axis=0))
    r0 = _dot(bbt_re, ca0_re, HI) - _dot(bbt_im, ca0_im, HI)
    lane = lax.broadcasted_iota(jnp.int32, (n_c, lw), 1)
    for s in range(lc):
        blk = r0 if s == 0 else jnp.where(lane >= s * n_c, pltpu.roll(r0, s * n_c, axis=1), 0.0)
        toep_ref[0, s * n_c:(s + 1) * n_c, :] = _bf(blk)

    cr, ci = _s5_power(abr_re, abr_im, jnp.full(abr_re.shape, lc, jnp.int32), lc)
    for j in range(lvl_ref.shape[1]):
        lvl_ref[0, j] = jnp.concatenate([jnp.concatenate([cr, cr], axis=1),
                                         jnp.concatenate([-ci, ci], axis=1)], axis=0)
        cr, ci = cr * cr - ci * ci, 2.0 * cr * ci


def s5_operators(log_dt, a_re, a_im, b_re, b_im, c_re, c_im, n_chunks, lc):
    n_g, n_p = a_re.shape
    n_c = b_re.shape[-1]
    lw = lc * n_c
    n_lvl = max(1, int(math.ceil(math.log2(n_chunks))))
    til = (jnp.arange(lw)[None, :] % n_c == jnp.arange(n_c)[:, None]).astype(jnp.float32)
    t3 = lambda a: jnp.swapaxes(a, 1, 2)
    args = [log_dt.reshape(n_g, 1, 1), a_re.reshape(n_g, 1, n_p), a_im.reshape(n_g, 1, n_p),
            a_re.reshape(n_g, n_p, 1), a_im.reshape(n_g, n_p, 1),
            t3(b_re), t3(b_im), t3(c_re), t3(c_im)]
    per_g = lambda a: pl.BlockSpec((1,) + a.shape[1:], lambda g: (g,) + (0,) * (a.ndim - 1))
    return pl.pallas_call(
        functools.partial(_s5ops_kernel, lc=lc),
        grid=(n_g,),
        in_specs=[per_g(a) for a in args] + [pl.BlockSpec(til.shape, lambda g: (0, 0))],
        out_specs=[pl.BlockSpec((1, lw, lw), lambda g: (g, 0, 0)),
                   pl.BlockSpec((1, lw, 2 * n_p), lambda g: (g, 0, 0)),
                   pl.BlockSpec((1, 2 * n_p, lw), lambda g: (g, 0, 0)),
                   pl.BlockSpec((1, n_lvl, 2, 2 * n_p), lambda g: (g, 0, 0, 0))],
        out_shape=[jax.ShapeDtypeStruct((n_g, lw, lw), jnp.bfloat16),
                   jax.ShapeDtypeStruct((n_g, lw, 2 * n_p), jnp.bfloat16),
                   jax.ShapeDtypeStruct((n_g, 2 * n_p, lw), jnp.bfloat16),
                   jax.ShapeDtypeStruct((n_g, n_lvl, 2, 2 * n_p), jnp.float32)],
        compiler_params=_cparams("arbitrary"),
        name="s5ops",
    )(*args, til)


def _s5_kernel(u_ref, toep_ref, pm_ref, q_ref, lvl_ref, y_ref, *, n_chunks):
    u = _bf(u_ref[0])
    e = _dot(u, pm_ref[0])
    rows, two_p = e.shape
    half = two_p // 2
    cidx = lax.broadcasted_iota(jnp.int32, e.shape, 0) % n_chunks
    x = e
    n_lvl = lvl_ref.shape[1]
    for j in range(n_lvl):
        sh = 1 << j
        if sh >= n_chunks:
            break
        xs = jnp.where(cidx >= sh, pltpu.roll(x, sh, axis=0), 0.0)
        cf = lvl_ref[0, j]
        x = x + xs * cf[0:1, :] + pltpu.roll(xs, half, axis=1) * cf[1:2, :]
    x_in = jnp.where(cidx >= 1, pltpu.roll(x, 1, axis=0), 0.0)
    x_hi = _bf(x_in)
    x_lo = _bf(x_in - x_hi.astype(jnp.float32))
    q = q_ref[0]
    y_ref[0] = _dot(u, toep_ref[0]) + _dot(x_hi, q) + _dot(x_lo, q)


def _to_groups_kernel(u_ref, o_ref, *, lc, n_c):
    per, nb, lw = o_ref.shape
    slot = lax.broadcasted_iota(jnp.int32, (nb, LANES), 1) // n_c
    a = [u_ref[pl.ds(s, nb, stride=lc), :] for s in range(lc)]
    for gi in range(per):
        for j in range(lw // LANES):
            acc = None
            for ai in range(per):
                shift = ((ai - gi) % per) * n_c
                src = a[j * per + ai]
                piece = src if shift == 0 else pltpu.roll(src, shift, axis=1)
                acc = piece if acc is None else jnp.where(slot == ai, piece, acc)
            o_ref[gi, :, j * LANES:(j + 1) * LANES] = acc


def _from_groups_kernel(y_ref, o_ref, *, lc, n_c):
    per, nb, lw = y_ref.shape
    slot = lax.broadcasted_iota(jnp.int32, (nb, LANES), 1) // n_c
    for s in range(lc):
        j, ai = divmod(s, per)
        acc = None
        for gi in range(per):
            shift = ((gi - ai) % per) * n_c
            src = y_ref[gi, :, j * LANES:(j + 1) * LANES]
            piece = src if shift == 0 else pltpu.roll(src, shift, axis=1)
            acc = piece if acc is None else jnp.where(slot == gi, piece, acc)
        o_ref[pl.ds(s, nb, stride=lc), :] = acc


def _group_relayout(x, n_g, lc, to_groups, tile):
    if to_groups:
        n_tok, width = x.shape
    else:
        n_tok, width = x.shape[1] * lc, x.shape[2] // lc * n_g
    n_c = width // n_g
    per = LANES // n_c
    nb = tile // lc
    tok_spec = pl.BlockSpec((tile, LANES), lambda i, vb: (i, vb))
    grp_spec = pl.BlockSpec((per, nb, lc * n_c), lambda i, vb: (vb, i, 0))
    kern = _to_groups_kernel if to_groups else _from_groups_kernel
    out_shape = (n_g, n_tok // lc, lc * n_c) if to_groups else (n_tok, width)
    return pl.pallas_call(
        functools.partial(kern, lc=lc, n_c=n_c),
        grid=(n_tok // tile, n_g // per),
        in_specs=[tok_spec if to_groups else grp_spec],
        out_specs=grp_spec if to_groups else tok_spec,
        out_shape=jax.ShapeDtypeStruct(out_shape, x.dtype),
        compiler_params=_cparams("arbitrary", "arbitrary"),
        name="to_groups" if to_groups else "from_groups",
    )(x)


def s5_core(u, log_dt, a_re, a_im, b_re, b_im, c_re, c_im):
    bsz, seq, width = u.shape
    n_g, n_p = a_re.shape
    n_c = width // n_g
    lc = min(S5_CHUNK, seq)
    n_chunks = seq // lc
    toep, pm, q, lvl = s5_operators(log_dt, a_re, a_im, b_re, b_im, c_re, c_im, n_chunks, lc)
    tile = min(2048, bsz * seq)
    ug = _group_relayout(u.reshape(bsz * seq, width), n_g, lc, True, tile)
    rows, lw = bsz * n_chunks, lc * n_c
    yg = pl.pallas_call(
        functools.partial(_s5_kernel, n_chunks=n_chunks),
        grid=(n_g,),
        in_specs=[pl.BlockSpec((1, rows, lw), lambda g: (g, 0, 0)),
                  pl.BlockSpec((1, lw, lw), lambda g: (g, 0, 0)),
                  pl.BlockSpec((1, lw, 2 * n_p), lambda g: (g, 0, 0)),
                  pl.BlockSpec((1, 2 * n_p, lw), lambda g: (g, 0, 0)),
                  pl.BlockSpec((1,) + lvl.shape[1:], lambda g: (g, 0, 0, 0))],
        out_specs=pl.BlockSpec((1, rows, lw), lambda g: (g, 0, 0)),
        out_shape=jax.ShapeDtypeStruct((n_g, rows, lw), jnp.float32),
        compiler_params=_cparams("arbitrary"),
        name="s5",
    )(ug, toep, pm, q, lvl)
    return _group_relayout(yg, n_g, lc, False, tile).reshape(bsz, seq, width)


def _gelu_tanh(y):
    return 0.5 * y * (1.0 + jnp.tanh(math.sqrt(2.0 / math.pi) * (y + 0.044715 * (y * y * y))))


def _outproj_kernel(yr_ref, ys_ref, u_ref, x_ref, mods_ref, d_ref, wglu_ref, bglu_ref, wout_ref,
                    gpost_ref, gpre_ref, wrt_ref, x1_ref, h2_ref, lg_ref):
    m = mods_ref[0]
    yr = yr_ref[...]
    y5 = _gelu_tanh(ys_ref[...] + d_ref[...] * u_ref[...])
    y5 = y5 * _sigmoid(_dot(_bf(y5), wglu_ref[...]) + bglu_ref[...])
    wr = yr.shape[-1]
    mixed = _dot(_bf(yr), wout_ref[0:wr, :]) + _dot(_bf(y5), wout_ref[wr:, :])
    x1 = x_ref[...] + m[2:3, :] * _rms(mixed, gpost_ref[...])
    x1_ref[...] = x1
    h2 = _rms(x1, gpre_ref[...]) * (1.0 + m[4:5, :]) + m[3:4, :]
    half = h2.shape[-1] // 2
    _store_row_tiles(h2_ref, _pack_bf16_pair(h2[:, :half], h2[:, half:]))
    w_hi, w_lo = wrt_ref[0], wrt_ref[1]
    h_hi = _bf(h2)
    h_lo = _bf(h2 - h_hi.astype(jnp.float32))
    lg_ref[...] = _dot_nt(w_hi, h_hi) + (_dot_nt(w_hi, h_lo) + _dot_nt(w_lo, h_hi))


def out_proj(y_rwkv, y_s5, u, x, mods3, s5_d, w_glu_bf, b_glu, w_out_bf, g_post, g_pre, w_router_t,
             seq, tm):
    n_tok, dm = x.shape
    wr, ws = y_rwkv.shape[-1], y_s5.shape[-1]
    n_e = w_router_t.shape[0]
    tpb = seq // tm
    tok = lambda w: pl.BlockSpec((tm, w), lambda i: (i, 0))
    full = lambda a: pl.BlockSpec(a.shape, lambda i: (0,) * a.ndim)
    row = lambda t: t.reshape(1, -1)
    wr_hi = _bf(w_router_t)
    wr_split = jnp.stack([wr_hi, _bf(w_router_t - wr_hi.astype(jnp.float32))])
    consts = [row(s5_d), w_glu_bf, row(b_glu), w_out_bf, row(g_post), row(g_pre), wr_split]
    return pl.pallas_call(
        _outproj_kernel,
        grid=(n_tok // tm,),
        in_specs=[tok(wr), tok(ws), tok(ws), tok(dm),
                  pl.BlockSpec((1, N_MODS, dm), lambda i: (i // tpb, 0, 0))] + [full(a) for a in consts],
        out_specs=[tok(dm), _row_tile_spec(tm, dm // 2, lambda i: i),
                   pl.BlockSpec((n_e, tm), lambda i: (0, i))],
        out_shape=[jax.ShapeDtypeStruct((n_tok, dm), jnp.float32),
                   jax.ShapeDtypeStruct((n_tok * (dm // 2 // LANES), LANES), jnp.uint32),
                   jax.ShapeDtypeStruct((n_e, n_tok), jnp.float32)],
        compiler_params=_cparams("arbitrary"),
        name="outproj",
    )(y_rwkv, y_s5, u, x, mods3, *consts)


def _route_kernel(lg_ref, bias_ref, tri_ref, e_ref, w_ref, r_ref, cnt_ref, carry_ref):
    i = pl.program_id(0)

    @pl.when(i == 0)
    def _():
        carry_ref[...] = jnp.zeros_like(carry_ref)

    neg = -jnp.inf
    scores = _sigmoid(lg_ref[...])
    n_e, tm = scores.shape
    choice = scores + bias_ref[...]
    gsz = n_e // ROUTE_GROUPS
    c3 = choice.reshape(ROUTE_GROUPS, gsz, tm)
    io = lax.broadcasted_iota(jnp.int32, c3.shape, 1)
    m1 = jnp.max(c3, axis=1, keepdims=True)
    first = jnp.min(jnp.where(c3 == m1, io, gsz), axis=1, keepdims=True)
    m2 = jnp.max(jnp.where(io == first, neg, c3), axis=1, keepdims=True)
    gs = m1 + m2
    gi = lax.broadcasted_iota(jnp.int32, gs.shape, 0)
    rank = jnp.zeros(gs.shape, jnp.int32)
    for j in range(ROUTE_GROUPS):
        gj = gs[j:j + 1]
        beats = (gj > gs) | ((gj == gs) & (gi > j))
        rank = rank + beats.astype(jnp.int32)
    masked = jnp.where(rank < ROUTE_TOPK_GROUPS, c3, neg).reshape(n_e, tm)

    eio = lax.broadcasted_iota(jnp.int32, (n_e, tm), 0)
    ids, ws = [], []
    mhot = jnp.zeros((n_e, tm), jnp.float32)
    for _ in range(TOP_K):
        m = jnp.max(masked, axis=0, keepdims=True)
        idx = jnp.min(jnp.where(masked == m, eio, n_e), axis=0, keepdims=True)
        sel = eio == idx
        ws.append(jnp.sum(jnp.where(sel, scores, 0.0), axis=0, keepdims=True))
        ids.append(idx)
        masked = jnp.where(sel, neg, masked)
        mhot = jnp.where(sel, 1.0, mhot)
    wsum = ws[0]
    for t in ws[1:]:
        wsum = wsum + t
    before = _dot(_bf(mhot), tri_ref[...]) + carry_ref[...]
    ranks = [jnp.sum(jnp.where(eio == idx, before, 0.0), axis=0, keepdims=True) for idx in ids]
    e_ref[...] = jnp.concatenate(ids, axis=0)
    w_ref[...] = jnp.concatenate(ws, axis=0) / wsum * ROUTE_SCALE
    r_ref[...] = jnp.concatenate(ranks, axis=0).astype(jnp.int32)
    carry_ref[...] = carry_ref[...] + jnp.sum(mhot, axis=1, keepdims=True)
    cnt_ref[...] = carry_ref[...]


def route(logits_t, router_bias, tm):
    n_e, n_tok = logits_t.shape
    tri = (jnp.arange(tm)[:, None] < jnp.arange(tm)[None, :]).astype(jnp.bfloat16)
    kt = lambda: pl.BlockSpec((TOP_K, tm), lambda i: (0, i))
    return pl.pallas_call(
        _route_kernel,
        grid=(n_tok // tm,),
        in_specs=[pl.BlockSpec((n_e, tm), lambda i: (0, i)),
                  pl.BlockSpec((n_e, 1), lambda i: (0, 0)),
                  pl.BlockSpec((tm, tm), lambda i: (0, 0))],
        out_specs=[kt(), kt(), kt(), pl.BlockSpec((n_e, 1), lambda i: (0, 0))],
        out_shape=[jax.ShapeDtypeStruct((TOP_K, n_tok), jnp.int32),
                   jax.ShapeDtypeStruct((TOP_K, n_tok), jnp.float32),
                   jax.ShapeDtypeStruct((TOP_K, n_tok), jnp.int32),
                   jax.ShapeDtypeStruct((n_e, 1), jnp.float32)],
        scratch_shapes=[pltpu.VMEM((n_e, 1), jnp.float32)],
        compiler_params=_cparams("arbitrary"),
        name="route",
    )(logits_t, router_bias.reshape(n_e, 1), tri)


def _dest_kernel(e_ref, r_ref, ps_ref, d_ref):
    n_e = ps_ref.shape[0]
    tm = e_ref.shape[1]
    eio = lax.broadcasted_iota(jnp.int32, (n_e, tm), 0)
    ps = ps_ref[...]
    rows = [jnp.sum(jnp.where(eio == e_ref[k:k + 1, :], ps, 0.0), axis=0, keepdims=True)
            for k in range(TOP_K)]
    d_ref[...] = jnp.concatenate(rows, axis=0).astype(jnp.int32) + r_ref[...]


def dest_rows(e_idx, rank, pad_start, tm):
    n_tok = e_idx.shape[1]
    n_e = pad_start.shape[0]
    kt = lambda: pl.BlockSpec((TOP_K, tm), lambda i: (0, i))
    return pl.pallas_call(
        _dest_kernel,
        grid=(n_tok // tm,),
        in_specs=[kt(), kt(), pl.BlockSpec((n_e, 1), lambda i: (0, 0))],
        out_specs=kt(),
        out_shape=jax.ShapeDtypeStruct((TOP_K, n_tok), jnp.int32),
        compiler_params=_cparams("arbitrary"),
        name="dest",
    )(e_idx, rank, pad_start.astype(jnp.float32).reshape(n_e, 1))


def _dispatch_kernel(fill_start_ref, fill_len_ref, nu_ref, dest_ref, h_ref, xs_ref, zeros, sem, zsem,
                     *, n_s, bm, experts_per_step, tiles_per_step, n_tiles):
    i = pl.program_id(0)
    td = dest_ref.shape[1]

    @pl.when(i == 0)
    def _():
        zeros[...] = jnp.zeros_like(zeros)

    def body(t, carry):
        for k in range(TOP_K):
            pltpu.make_async_copy(_row(h_ref, t, n_s), _row(xs_ref, dest_ref[k, t], n_s),
                                  sem).start(priority=k % 2)
        return carry

    lax.fori_loop(0, td, body, 0, unroll=2)

    fills = []
    for j in range(experts_per_step):
        e = i * experts_per_step + j
        start, length = fill_start_ref[e], fill_len_ref[e]
        piece = bm // 2
        while piece >= 1:
            off = start + (length & ~(2 * piece - 1))
            fills.append(((length & piece) != 0,
                          pltpu.make_async_copy(zeros.at[pl.ds(0, piece * n_s), :],
                                                xs_ref.at[pl.ds(off * n_s, piece * n_s), :], zsem)))
            piece //= 2
    for j in range(tiles_per_step):
        tile = i * tiles_per_step + j
        fills.append(((tile >= nu_ref[0]) & (tile < n_tiles),
                      pltpu.make_async_copy(zeros, xs_ref.at[pl.ds(tile * (bm * n_s), bm * n_s), :], zsem)))
    for cond, cp in fills:
        pl.when(cond)(cp.start)

    for _ in range(TOP_K):
        pltpu.make_async_copy(h_ref, xs_ref.at[pl.ds(0, td * n_s), :], sem).wait()
    for cond, cp in fills:
        pl.when(cond)(cp.wait)


def dispatch(dest, h2p, fill_start, fill_len, n_used, n_tok, n_tiles, td, bm):
    n_s = h2p.shape[0] // n_tok
    n_steps = n_tok // td
    n_e = fill_start.shape[0]
    experts_per_step = -(-n_e // n_steps)
    tiles_per_step = -(-n_tiles // n_steps)
    pad = n_steps * experts_per_step - n_e
    fill_start = jnp.pad(fill_start, (0, pad))
    fill_len = jnp.pad(fill_len, (0, pad))
    return pl.pallas_call(
        functools.partial(_dispatch_kernel, n_s=n_s, bm=bm, experts_per_step=experts_per_step,
                          tiles_per_step=tiles_per_step, n_tiles=n_tiles),
        grid_spec=pltpu.PrefetchScalarGridSpec(
            num_scalar_prefetch=3,
            grid=(n_steps,),
            in_specs=[pl.BlockSpec((TOP_K, td), lambda i, *_: (0, i), memory_space=pltpu.SMEM),
                      _row_tile_spec(td, n_s * LANES, lambda i, *_: i)],
            out_specs=pl.BlockSpec(memory_space=pl.ANY),
            scratch_shapes=[pltpu.VMEM((bm * n_s, LANES), h2p.dtype),
                            pltpu.SemaphoreType.DMA(()), pltpu.SemaphoreType.DMA(())]),
        out_shape=jax.ShapeDtypeStruct((n_tiles * bm * n_s, LANES), h2p.dtype),
        compiler_params=_cparams("arbitrary"),
        name="dispatch",
    )(fill_start, fill_len, n_used, dest, h2p)


def _expert_kernel(te_ref, nu_ref, nt_ref, x_ref, wg_hbm, wu_hbm, wd_hbm, o_ref,
                   wg_st, wu_st, wd_st, wgu_bf, wd_bf, grp, sems, *, bm):
    i = pl.program_id(0)
    ff = wd_bf.shape[0]
    n_used = nu_ref[0]

    def fetch(e, slot):
        return (pltpu.make_async_copy(wg_hbm.at[e], wg_st.at[slot], sems.at[slot]),
                pltpu.make_async_copy(wu_hbm.at[e], wu_st.at[slot], sems.at[slot]),
                pltpu.make_async_copy(wd_hbm.at[e], wd_st.at[slot], sems.at[slot]))

    @pl.when(i == 0)
    def _():
        grp[0] = 0
        for cp in fetch(te_ref[0], 0):
            cp.start()

    @pl.when(i < n_used)
    def _():
        e = te_ref[i]

        @pl.when((i == 0) | (e != te_ref[jnp.maximum(i - 1, 0)]))
        def _():
            slot = grp[0] & 1
            grp[0] = grp[0] + 1
            for cp in fetch(e, slot):
                cp.wait()
            nxt = i + nt_ref[e]

            @pl.when(nxt < n_used)
            def _():
                for cp in fetch(te_ref[jnp.minimum(nxt, n_used - 1)], 1 - slot):
                    cp.start()

            wgu_bf[:, :ff] = _bf(wg_st[slot])
            wgu_bf[:, ff:] = _bf(wu_st[slot])
            wd_bf[...] = _bf(wd_st[slot])

        hi, lo = _unpack_bf16_pair(_load_row_tiles(x_ref, bm))
        x = jnp.concatenate([hi, lo], axis=1)
        gu = _dot(x, wgu_bf[...])
        hid = _silu(gu[:, :ff]) * gu[:, ff:]
        y = _dot(_bf(hid), wd_bf[...])
        half = y.shape[1] // 2
        _store_row_tiles(o_ref, _pack_bf16_pair(y[:, :half], y[:, half:]))

    @pl.when(i >= nu_ref[0])
    def _():
        o_ref[...] = jnp.zeros_like(o_ref)


def expert_ffn(tile_e, n_used, tiles_per_expert, xs, n_rows, w_gate, w_up, w_down, bm):
    dm = w_gate.shape[1]
    half = dm // 2
    ff = w_gate.shape[-1]
    rows_in = _row_tile_spec(bm, half, lambda i, te, nu, nt: jnp.minimum(i, nu[0] - 1))
    rows_out = _row_tile_spec(bm, half, lambda i, te, nu, nt: i)
    hbm = pl.BlockSpec(memory_space=pl.ANY)
    return pl.pallas_call(
        functools.partial(_expert_kernel, bm=bm),
        grid_spec=pltpu.PrefetchScalarGridSpec(
            num_scalar_prefetch=3,
            grid=(n_rows // bm,),
            in_specs=[rows_in, hbm, hbm, hbm],
            out_specs=rows_out,
            scratch_shapes=[pltpu.VMEM((2, dm, ff), jnp.float32),
                            pltpu.VMEM((2, dm, ff), jnp.float32),
                            pltpu.VMEM((2, ff, dm), jnp.float32),
                            pltpu.VMEM((dm, 2 * ff), jnp.bfloat16),
                            pltpu.VMEM((ff, dm), jnp.bfloat16),
                            pltpu.SMEM((1,), jnp.int32),
                            pltpu.SemaphoreType.DMA((2,))]),
        out_shape=jax.ShapeDtypeStruct((n_rows * (half // LANES), LANES), jnp.uint32),
        compiler_params=_cparams("arbitrary"),
        name="expert",
    )(tile_e, n_used, tiles_per_expert, xs, w_gate, w_up, w_down)


def _combine_kernel(dest_ref, dnext_ref, w_ref, x1_ref, h2_ref, mods_ref, g_ref, wsg_ref, wsu_ref,
                    wsd_ref, ys_ref, o_ref, gbuf, sems, *, n_s):
    i = pl.program_id(0)
    n_steps = pl.num_programs(0)
    tc = x1_ref.shape[0]

    def issue(d_ref, slot):
        def body(t, carry):
            for k in range(TOP_K):
                pltpu.make_async_copy(_row(ys_ref, d_ref[k, t], n_s), _row(gbuf.at[slot, k], t, n_s),
                                      sems.at[slot]).start(priority=k % 2)
            return carry
        lax.fori_loop(0, tc, body, 0, unroll=2)

    def finish(slot):
        for k in range(TOP_K):
            pltpu.make_async_copy(ys_ref.at[pl.ds(0, tc * n_s), :], gbuf.at[slot, k],
                                  sems.at[slot]).wait()
        hi, lo = _unpack_bf16_pair(_load_row_tiles(h2_ref, tc))
        h = jnp.concatenate([hi, lo], axis=1)
        hid = _silu(_dot(h, wsg_ref[...])) * _dot(h, wsu_ref[...])
        y = _dot(_bf(hid), wsd_ref[...])
        w = w_ref[...]
        acc_hi = acc_lo = None
        for k in range(TOP_K):
            g_hi, g_lo = _unpack_pair_f32(_load_row_tiles(gbuf.at[slot, k], tc))
            wk = w[:, k:k + 1]
            acc_hi = wk * g_hi if acc_hi is None else acc_hi + wk * g_hi
            acc_lo = wk * g_lo if acc_lo is None else acc_lo + wk * g_lo
        y = y + jnp.concatenate([acc_hi, acc_lo], axis=1)
        m = mods_ref[0]
        o_ref[...] = x1_ref[...] + m[5:6, :] * _rms(y, g_ref[...])

    @pl.when(i == 0)
    def _():
        issue(dest_ref, 0)

    for slot in range(2):
        @pl.when(i % 2 == slot)
        def _(slot=slot):
            @pl.when(i + 1 < n_steps)
            def _():
                issue(dnext_ref, 1 - slot)
            finish(slot)


def combine(dest, w_tk, x1, h2p, mods3, g_post, ws_gate_bf, ws_up_bf, ws_down_bf, ys, seq, tc):
    n_tok, dm = x1.shape
    n_s = dm // 2 // LANES
    n_steps = n_tok // tc
    tpb = seq // tc
    full = lambda a: pl.BlockSpec(a.shape, lambda i: (0,) * a.ndim)
    tok = lambda w: pl.BlockSpec((tc, w), lambda i: (i, 0))
    g2 = g_post.reshape(1, dm)
    return pl.pallas_call(
        functools.partial(_combine_kernel, n_s=n_s),
        grid=(n_steps,),
        in_specs=[pl.BlockSpec((TOP_K, tc), lambda i: (0, i), memory_space=pltpu.SMEM),
                  pl.BlockSpec((TOP_K, tc), lambda i: (0, jnp.minimum(i + 1, n_steps - 1)),
                               memory_space=pltpu.SMEM),
                  tok(TOP_K),
                  tok(dm), _row_tile_spec(tc, dm // 2, lambda i: i),
                  pl.BlockSpec((1, N_MODS, dm), lambda i: (i // tpb, 0, 0)),
                  full(g2), full(ws_gate_bf), full(ws_up_bf), full(ws_down_bf),
                  pl.BlockSpec(memory_space=pl.ANY)],
        out_specs=tok(dm),
        out_shape=jax.ShapeDtypeStruct((n_tok, dm), jnp.float32),
        scratch_shapes=[pltpu.VMEM((2, TOP_K, tc * n_s, LANES), jnp.uint32),
                        pltpu.SemaphoreType.DMA((2,))],
        compiler_params=_cparams("arbitrary"),
        name="combine",
    )(dest, dest, w_tk, x1, h2p, mods3, g2, ws_gate_bf, ws_up_bf, ws_down_bf, ys)


def moe_ffn(h2, x1, logits_t, mods3, router_bias, w_gate, w_up, w_down, ws_gate, ws_up, ws_down,
            g_post, seq):
    n_tok, dm = x1.shape
    n_e = w_gate.shape[0]
    bm = EXPERT_ROW_TILE
    e_idx, w_kt, rank, cnt = route(logits_t, router_bias, min(512, n_tok))
    counts = cnt[:, 0].astype(jnp.int32)
    padded = (counts + bm - 1) // bm * bm
    pad_end = jnp.cumsum(padded)
    pad_start = (pad_end - padded).astype(jnp.int32)
    n_tiles = -(-(n_tok * TOP_K) // bm) + n_e
    n_used = (pad_end[-1] // bm).astype(jnp.int32)
    tile_start = jnp.arange(n_tiles, dtype=jnp.int32) * bm
    tile_e = jnp.minimum(jnp.sum((pad_end[None, :] <= tile_start[:, None]).astype(jnp.int32), axis=1),
                         n_e - 1).astype(jnp.int32)
    tile_e = jnp.where(jnp.arange(n_tiles) < n_used, tile_e, tile_e[n_used - 1])
    dest = dest_rows(e_idx, rank, pad_start, min(512, n_tok))
    n_used1 = n_used.reshape(1)
    xs = dispatch(dest, h2, pad_start + counts, padded - counts, n_used1, n_tok, n_tiles,
                  min(256, n_tok), bm)
    ys = expert_ffn(tile_e, n_used1, (padded // bm).astype(jnp.int32), xs, n_tiles * bm,
                    w_gate, w_up, w_down, bm)
    bf = lambda a: a.astype(jnp.bfloat16)
    return combine(dest, w_kt.T, x1, h2, mods3, g_post, bf(ws_gate), bf(ws_up), bf(ws_down), ys,
                   seq, min(256, n_tok))


def kernel(x, c, w_ada, b_ada, g_pre_mix, g_post_mix, g_pre_ffn, g_post_ffn, w_in, mu_shift, rwkv_w0, rwkv_w2, rwkv_a0, rwkv_a2, rwkv_g2, rwkv_k_k, rwkv_k_a, rwkv_r_k, rwkv_ln_w, rwkv_ln_b, s5_log_dt, s5_a_re, s5_a_im, s5_b_re, s5_b_im, s5_c_re, s5_c_im, s5_d, s5_w_glu, s5_b_glu, w_out, w_router, router_bias, w_gate, w_up, w_down, ws_gate, ws_up, ws_down):
    bsz, seq, dm = x.shape
    depth = w_ada.shape[0]
    bf = lambda a: a.astype(jnp.bfloat16)
    tm = min(512, seq)
    for l in range(depth):
        mods3 = ada_mods(c, w_ada[l], b_ada[l]).reshape(bsz, N_MODS, dm)
        n_rwkv = mu_shift.shape[-1]
        p, u = in_proj(x, mods3, g_pre_mix[l], bf(w_in[l]), n_rwkv, tm)
        y_rwkv = rwkv_mixer(p, mu_shift[l], rwkv_w0[l], rwkv_w2[l], rwkv_a0[l], rwkv_a2[l],
                            rwkv_g2[l], rwkv_k_k[l], rwkv_k_a[l], rwkv_r_k[l].reshape(-1),
                            rwkv_ln_w[l], rwkv_ln_b[l])
        y_s5 = s5_core(u, s5_log_dt[l], s5_a_re[l], s5_a_im[l], s5_b_re[l], s5_b_im[l],
                       s5_c_re[l], s5_c_im[l])
        flat = lambda a: a.reshape(bsz * seq, a.shape[-1])
        x1, h2, logits_t = out_proj(flat(y_rwkv), flat(y_s5), flat(u), flat(x), mods3, s5_d[l],
                                    bf(s5_w_glu[l]), s5_b_glu[l], bf(w_out[l]), g_post_mix[l],
                                    g_pre_ffn[l], w_router[l].T, seq, tm)
        out = moe_ffn(h2, x1, logits_t, mods3, router_bias[l], w_gate[l], w_up[l], w_down[l],
                      ws_gate[l], ws_up[l], ws_down[l], g_post_ffn[l], seq)
        x = out.reshape(bsz, seq, dm)
    return x
```

```python
import functools
import math

import jax
import jax.numpy as jnp
from jax import lax
from jax.experimental import pallas as pl
from jax.experimental.pallas import tpu as pltpu

NORM_EPS = 1e-6
LNX_EPS = 64e-5
L2_EPS = 1e-12
S5_MAX_REAL = -1e-4
ROUTE_SCALE = 2.5
N_MODS = 6

RWKV_HEAD_DIM = 64
DECAY_LORA = 64
ICLR_LORA = 64
GATE_LORA = 128
TOP_K = 8
ROUTE_GROUPS = 8
ROUTE_TOPK_GROUPS = 4

RWKV_CHUNK = 64
S5_CHUNK = 16
EXPERT_ROW_TILE = 512
MOE_TOKEN_TILE = 512

VMEM_LIMIT = 56 * 1024 * 1024

HI = lax.Precision.HIGHEST


def _cparams(*sem):
    return pltpu.CompilerParams(dimension_semantics=sem, vmem_limit_bytes=VMEM_LIMIT)


def _dot(a, b, precision=None):
    return jnp.dot(a, b, preferred_element_type=jnp.float32, precision=precision)


def _dot_nt(a, b, precision=None):
    return lax.dot_general(a, b, (((1,), (1,)), ((), ())),
                           preferred_element_type=jnp.float32, precision=precision)


def _bdot(a, b):
    return lax.dot_general(a, b, (((2,), (1,)), ((0,), (0,))), preferred_element_type=jnp.float32)


def _bdot_nt(a, b):
    return lax.dot_general(a, b, (((2,), (2,)), ((0,), (0,))), preferred_element_type=jnp.float32)


def _bf(x):
    return x.astype(jnp.bfloat16)


def _pack_bf16_pair(hi, lo):
    hb = lax.bitcast_convert_type(_bf(hi).astype(jnp.float32), jnp.uint32)
    lb = lax.bitcast_convert_type(_bf(lo).astype(jnp.float32), jnp.uint32)
    return (hb & jnp.uint32(0xFFFF0000)) | (lb >> 16)


def _unpack_pair_f32(w):
    hi = lax.bitcast_convert_type(w & jnp.uint32(0xFFFF0000), jnp.float32)
    lo = lax.bitcast_convert_type(w << 16, jnp.float32)
    return hi, lo


def _unpack_bf16_pair(w):
    hi, lo = _unpack_pair_f32(w)
    return _bf(hi), _bf(lo)


LANES = 128


def _row_tile_spec(rows, width, row_block):
    return pl.BlockSpec((rows * (width // LANES), LANES), lambda *a: (row_block(*a), 0))


def _store_row_tiles(ref, x):
    rows = x.shape[0]
    n_s = ref.shape[0] // rows
    for s in range(n_s):
        ref[pl.ds(s, rows, stride=n_s), :] = x[:, s * LANES:(s + 1) * LANES]


def _load_row_tiles(ref, rows):
    n_s = ref.shape[0] // rows
    return jnp.concatenate([ref[pl.ds(s, rows, stride=n_s), :] for s in range(n_s)], axis=1)


def _row(ref, r, n_s):
    return ref.at[pl.ds(pl.multiple_of(r * n_s, n_s), n_s), :]


def _sigmoid(x):
    return 1.0 / (1.0 + jnp.exp(-x))


def _silu(x):
    return x * _sigmoid(x)


def _rms(x, gain):
    return x * lax.rsqrt(jnp.mean(x * x, axis=-1, keepdims=True) + NORM_EPS) * gain


def _ada_kernel(c_ref, w_ref, b_ref, o_ref):
    c = c_ref[...]
    o_ref[...] = _dot(_silu(c), w_ref[...], HI) + b_ref[...]


def ada_mods(c, w_ada, b_ada):
    bsz, dm = c.shape
    n = w_ada.shape[1]
    tn = dm
    return pl.pallas_call(
        _ada_kernel,
        grid=(n // tn,),
        in_specs=[pl.BlockSpec((bsz, dm), lambda j: (0, 0)),
                  pl.BlockSpec((dm, tn), lambda j: (0, j)),
                  pl.BlockSpec((1, tn), lambda j: (0, j))],
        out_specs=pl.BlockSpec((bsz, tn), lambda j: (0, j)),
        out_shape=jax.ShapeDtypeStruct((bsz, n), jnp.float32),
        compiler_params=_cparams("arbitrary"),
        name="ada",
    )(c, w_ada, b_ada.reshape(1, n))


def _inproj_kernel(x_ref, mods_ref, g_ref, w_ref, p_ref, u_ref):
    x = x_ref[0]
    m = mods_ref[0]
    h = _rms(x, g_ref[...]) * (1.0 + m[1:2, :]) + m[0:1, :]
    proj = _dot(_bf(h), w_ref[...])
    n_p = p_ref.shape[-1]
    p_ref[0] = proj[:, :n_p]
    u_ref[0] = proj[:, n_p:]


def in_proj(x, mods3, g_pre, w_in_bf, n_rwkv, tm):
    bsz, seq, dm = x.shape
    n = w_in_bf.shape[1]
    n_s5 = n - n_rwkv
    return pl.pallas_call(
        _inproj_kernel,
        grid=(bsz, seq // tm),
        in_specs=[pl.BlockSpec((1, tm, dm), lambda b, i: (b, i, 0)),
                  pl.BlockSpec((1, N_MODS, dm), lambda b, i: (b, 0, 0)),
                  pl.BlockSpec((1, dm), lambda b, i: (0, 0)),
                  pl.BlockSpec((dm, n), lambda b, i: (0, 0))],
        out_specs=[pl.BlockSpec((1, tm, n_rwkv), lambda b, i: (b, i, 0)),
                   pl.BlockSpec((1, tm, n_s5), lambda b, i: (b, i, 0))],
        out_shape=[jax.ShapeDtypeStruct((bsz, seq, n_rwkv), jnp.float32),
                   jax.ShapeDtypeStruct((bsz, seq, n_s5), jnp.float32)],
        compiler_params=_cparams("arbitrary", "arbitrary"),
        name="inproj",
    )(x, mods3, g_pre.reshape(1, dm), w_in_bf)


RWKV_TILE = 256


def _split_dot(x, m01, terms):
    blk = m01.shape[0]
    pieces = []
    rem = x
    for _ in range(terms):
        piece = _bf(rem)
        pieces.append(piece)
        rem = rem - piece.astype(jnp.float32)
    cols = []
    for c in range(x.shape[1] // blk):
        acc = None
        for piece in pieces:
            part = _dot(piece[:, c * blk:(c + 1) * blk], m01)
            acc = part if acc is None else acc + part
        cols.append(acc)
    return cols[0] if len(cols) == 1 else jnp.concatenate(cols, axis=1)


def _rwkv_kernel(p_ref, mu_ref, w0_ref, w2_ref, a0_ref, a2_ref, g2_ref, kk_ref, ka_ref,
                  rk_ref, lnw_ref, lnb_ref, bd_ref, o_ref,
                  s_ref, carry_ref, rt_ref, at_ref, bt_ref, kt_ref, v_ref, wl_ref, y_ref,
                  *, width, chunk):
    hd = RWKV_HEAD_DIM
    pw = 2 * hd
    pairs = width // pw
    i = pl.program_id(1)

    @pl.when(i == 0)
    def _():
        s_ref[...] = jnp.zeros_like(s_ref)
        carry_ref[...] = jnp.zeros_like(carry_ref)

    p = p_ref[0]
    n_t = p.shape[0]
    n_chunks = n_t // chunk
    row = lax.broadcasted_iota(jnp.int32, p.shape, 0)
    prev = jnp.where(row == 0, carry_ref[0:1, :], pltpu.roll(p, 1, axis=0))
    carry_ref[0:1, :] = p[n_t - 1:n_t, :]
    pm = p + (prev - p) * mu_ref[...]

    r = pm[:, 0:width]
    k = pm[:, width:2 * width]
    v = pm[:, 2 * width:3 * width]
    c0 = 3 * width
    w_lo = pm[:, c0:c0 + DECAY_LORA]
    a_lo = pm[:, c0 + DECAY_LORA:c0 + DECAY_LORA + ICLR_LORA]
    g_lo = pm[:, c0 + DECAY_LORA + ICLR_LORA:]

    z = w0_ref[...] + _dot(_bf(jnp.tanh(w_lo)), w2_ref[...])
    softplus_neg = jnp.maximum(-z, 0.0) + jnp.log(1.0 + jnp.exp(-jnp.abs(z)))
    logd = -jnp.exp(-softplus_neg - 0.5)
    iclr = _sigmoid(a0_ref[...] + _dot(_bf(a_lo), a2_ref[...]))
    gate = _dot(_bf(_sigmoid(g_lo)), g2_ref[...])

    bd = bd_ref[...]
    kk = k * kk_ref[...]
    kk = kk / jnp.maximum(jnp.sqrt(_split_dot(kk * kk, bd, 2)), L2_EPS)
    k2 = k * (1.0 + (iclr - 1.0) * ka_ref[...])
    bonus = _split_dot(r * k2 * rk_ref[...], bd, 2) * v

    ti = lax.broadcasted_iota(jnp.int32, (n_t, n_t), 0)
    si = lax.broadcasted_iota(jnp.int32, (n_t, n_t), 1)
    tri = jnp.where((ti >= si) & (ti // chunk == si // chunk), 1.0, 0.0).astype(jnp.bfloat16)
    cum = _split_dot_lhs(tri, logd, 3)
    e_pos = jnp.exp(cum)
    e_neg = jnp.exp(-cum)
    rt_ref[...] = _bf(r * e_pos)
    at_ref[...] = _bf(-kk * jnp.exp(cum - logd))
    bt_ref[...] = _bf(kk * iclr * e_neg)
    kt_ref[...] = _bf(k2 * e_neg)
    v_ref[...] = _bf(v)
    for c in range(n_chunks):
        wl_ref[c:c + 1, :] = e_pos[(c + 1) * chunk - 1:(c + 1) * chunk, :]

    two_l = 2 * chunk
    lane = lax.broadcasted_iota(jnp.int32, (chunk, pw), 1)
    lane0 = lane < hd
    bi = lax.broadcasted_iota(jnp.int32, (two_l, two_l), 0)
    bj = lax.broadcasted_iota(jnp.int32, (two_l, two_l), 1)
    same = (bi // chunk) == (bj // chunk)
    low_strict = same & (bi > bj)
    low_incl = same & (bi >= bj)
    eye_t = jnp.where(bi == bj, 1.0, 0.0)
    pi_ = lax.broadcasted_iota(jnp.int32, (pw, pw), 0)
    pj_ = lax.broadcasted_iota(jnp.int32, (pw, pw), 1)
    eye_p = jnp.where(pi_ == pj_, 1.0, 0.0)
    n_sq = max(1, int(math.ceil(math.log2(chunk))) - 1)
    zero = jnp.zeros((), jnp.bfloat16)

    def stack(x):
        return jnp.concatenate([jnp.where(lane0, x, zero), jnp.where(lane0, zero, x)], axis=0)

    def gather(ref):
        return jnp.stack([stack(ref[c * chunk:(c + 1) * chunk, hp * pw:(hp + 1) * pw])
                          for c in range(n_chunks) for hp in range(pairs)], axis=0)

    a_s, b_s, k_s, r_s, v_s = (gather(ref) for ref in (at_ref, bt_ref, kt_ref, rt_ref, v_ref))
    wl = jnp.stack([wl_ref[c:c + 1, hp * pw:(hp + 1) * pw]
                    for c in range(n_chunks) for hp in range(pairs)], axis=0)
    gram = _bdot_nt(jnp.concatenate([a_s, r_s], axis=1), jnp.concatenate([b_s, k_s], axis=1))
    m_ab = jnp.where(low_strict, gram[:, :two_l, :two_l], 0.0)
    m_ak = jnp.where(low_strict, gram[:, :two_l, two_l:], 0.0)
    n_rb = jnp.where(low_incl, gram[:, two_l:, :two_l], 0.0)
    n_rk = jnp.where(low_incl, gram[:, two_l:, two_l:], 0.0)
    t_inv = eye_t + m_ab
    m_pow = _bf(m_ab)
    for _ in range(n_sq):
        m_pow = _bf(_bdot(m_pow, m_pow))
        t_inv = t_inv + _bdot(_bf(t_inv), m_pow)
    makv = _bdot(_bf(m_ak), v_s)
    tx_bf = _bf(_bdot(_bf(t_inv), jnp.concatenate([a_s, _bf(makv)], axis=2)))
    nx = _bdot(_bf(n_rb), tx_bf)
    rbar = _bf(r_s.astype(jnp.float32) + nx[:, :, :pw])
    y0 = nx[:, :, pw:] + _bdot(_bf(n_rk), v_s)
    tb = _bdot(jnp.swapaxes(tx_bf, 1, 2), b_s)
    pmat = _bf((eye_p + tb[:, :pw, :]) * wl)
    dmat = (tb[:, pw:, :] + _bdot(jnp.swapaxes(v_s, 1, 2), k_s)) * wl

    s = s_ref[...]
    for c in range(n_chunks):
        sel = slice(c * pairs, (c + 1) * pairs)
        s_bf = _bf(s)
        ys = _bdot_nt(rbar[sel], s_bf) + y0[sel]
        yc_ = ys[:, :chunk, :] + ys[:, chunk:, :]
        for hp in range(pairs):
            y_ref[c * chunk:(c + 1) * chunk, hp * pw:(hp + 1) * pw] = yc_[hp]
        s = _bdot(s_bf, pmat[sel]) + dmat[sel]
    s_ref[...] = s

    y = y_ref[...]
    inv_hd = 1.0 / hd
    mean = _split_dot(y, bd, 2) * inv_hd
    yc = y - mean
    var = _split_dot(yc * yc, bd, 2) * inv_hd
    yn = yc * lax.rsqrt(var + LNX_EPS) * lnw_ref[...] + lnb_ref[...]
    o_ref[0] = (yn + bonus) * gate


def _split_dot_lhs(m01, x, terms):
    acc = None
    rem = x
    for _ in range(terms):
        piece = _bf(rem)
        part = _dot(m01, piece)
        acc = part if acc is None else acc + part
        rem = rem - piece.astype(jnp.float32)
    return acc


def rwkv_mixer(p, mu, w0, w2, a0, a2, g2, k_k, k_a, r_k, ln_w, ln_b):
    bsz, seq, n_p = p.shape
    width = w0.shape[-1]
    pairs = width // (2 * RWKV_HEAD_DIM)
    chunk = min(RWKV_CHUNK, seq)
    tile = min(RWKV_TILE, seq)
    hid = jnp.arange(2 * RWKV_HEAD_DIM, dtype=jnp.int32) // RWKV_HEAD_DIM
    bd = (hid[:, None] == hid[None, :]).astype(jnp.bfloat16)
    row = lambda t: t.reshape(1, -1)
    full = lambda a: pl.BlockSpec(a.shape, lambda b, i: (0,) * a.ndim)
    consts = [row(mu), row(w0), _bf(w2), row(a0), _bf(a2), _bf(g2), row(k_k), row(k_a), row(r_k),
              row(ln_w), row(ln_b), bd]
    act = lambda: pltpu.VMEM((tile, width), jnp.bfloat16)
    return pl.pallas_call(
        functools.partial(_rwkv_kernel, width=width, chunk=chunk),
        grid=(bsz, seq // tile),
        in_specs=[pl.BlockSpec((1, tile, n_p), lambda b, i: (b, i, 0))] + [full(a) for a in consts],
        out_specs=pl.BlockSpec((1, tile, width), lambda b, i: (b, i, 0)),
        out_shape=jax.ShapeDtypeStruct((bsz, seq, width), jnp.float32),
        scratch_shapes=[pltpu.VMEM((pairs, 2 * RWKV_HEAD_DIM, 2 * RWKV_HEAD_DIM), jnp.float32),
                        pltpu.VMEM((8, n_p), jnp.float32),
                        act(), act(), act(), act(), act(),
                        pltpu.VMEM((max(8, tile // chunk), width), jnp.float32),
                        pltpu.VMEM((tile, width), jnp.float32)],
        compiler_params=_cparams("arbitrary", "arbitrary"),
        name="rwkv",
    )(p, *consts)


def _s5_discretise(a_re, a_im, dt):
    lam_re = jnp.minimum(a_re, S5_MAX_REAL)
    lam_im = a_im
    mag = jnp.exp(lam_re * dt)
    ang = lam_im * dt
    ab_re, ab_im = mag * jnp.cos(ang), mag * jnp.sin(ang)
    den = lam_re * lam_re + lam_im * lam_im
    n_re, n_im = ab_re - 1.0, ab_im
    q_re = (n_re * lam_re + n_im * lam_im) / den
    q_im = (n_im * lam_re - n_re * lam_im) / den
    return ab_re, ab_im, q_re, q_im


def _s5_power(ab_re, ab_im, t, t_max):
    pr = jnp.ones(t.shape, jnp.float32)
    pi = jnp.zeros(t.shape, jnp.float32)
    br, bi = ab_re, ab_im
    for j in range(max(1, int(t_max).bit_length())):
        bit = ((t >> j) & 1) == 1
        pr, pi = jnp.where(bit, pr * br - pi * bi, pr), jnp.where(bit, pr * bi + pi * br, pi)
        br, bi = br * br - bi * bi, 2.0 * br * bi
    return pr, pi


def _s5ops_kernel(ldt_ref, ar_row, ai_row, ar_col, ai_col, bt_re, bt_im, ct_re, ct_im, til_ref,
                  toep_ref, pm_ref, q_ref, lvl_ref, *, lc):
    n_c, n_p = bt_re.shape[1], bt_re.shape[2]
    lw = lc * n_c
    dt = jnp.exp(ldt_ref[0])

    abr_re, abr_im, q_re, q_im = _s5_discretise(ar_row[0], ai_row[0], dt)
    bbt_re = q_re * bt_re[0] - q_im * bt_im[0]
    bbt_im = q_re * bt_im[0] + q_im * bt_re[0]
    s_row = lax.broadcasted_iota(jnp.int32, (lw, n_p), 0) // n_c
    pw_re, pw_im = _s5_power(abr_re, abr_im, lc - 1 - s_row, lc)
    tb_re = jnp.concatenate([bbt_re] * lc, axis=0)
    tb_im = jnp.concatenate([bbt_im] * lc, axis=0)
    pm_ref[0] = _bf(jnp.concatenate([pw_re * tb_re - pw_im * tb_im,
                                     pw_re * tb_im + pw_im * tb_re], axis=1))

    abc_re, abc_im, _, _ = _s5_discretise(ar_col[0], ai_col[0], dt)
    til = til_ref[...]
    c_re = _dot(ct_re[0], til, HI)
    c_im = _dot(ct_im[0], til, HI)
    t_lane = lax.broadcasted_iota(jnp.int32, (n_p, lw), 1) // n_c
    p0_re, p0_im = _s5_power(abc_re, abc_im, t_lane, lc)
    p1_re, p1_im = p0_re * abc_re - p0_im * abc_im, p0_re * abc_im + p0_im * abc_re
    ca0_re, ca0_im = p0_re * c_re - p0_im * c_im, p0_re * c_im + p0_im * c_re
    q_ref[0] = _bf(jnp.concatenate([p1_re * c_re - p1_im * c_im,
                                    -(p1_re * c_im + p1_im * c_re)], axis=0))
    r0 = _dot(bbt_re, ca0_re, HI) - _dot(bbt_im, ca0_im, HI)
    lane = lax.broadcasted_iota(jnp.int32, (n_c, lw), 1)
    for s in range(lc):
        blk = r0 if s == 0 else jnp.where(lane >= s * n_c, pltpu.roll(r0, s * n_c, axis=1), 0.0)
        toep_ref[0, s * n_c:(s + 1) * n_c, :] = _bf(blk)

    cr, ci = _s5_power(abr_re, abr_im, jnp.full(abr_re.shape, lc, jnp.int32), lc)
    for j in range(lvl_ref.shape[1]):
        lvl_ref[0, j] = jnp.concatenate([jnp.concatenate([cr, cr], axis=1),
                                         jnp.concatenate([-ci, ci], axis=1)], axis=0)
        cr, ci = cr * cr - ci * ci, 2.0 * cr * ci


def s5_operators(log_dt, a_re, a_im, b_re, b_im, c_re, c_im, n_chunks, lc):
    n_g, n_p = a_re.shape
    n_c = b_re.shape[-1]
    lw = lc * n_c
    n_lvl = max(1, int(math.ceil(math.log2(n_chunks))))
    til = (jnp.arange(lw)[None, :] % n_c == jnp.arange(n_c)[:, None]).astype(jnp.float32)
    t3 = lambda a: jnp.swapaxes(a, 1, 2)
    args = [log_dt.reshape(n_g, 1, 1), a_re.reshape(n_g, 1, n_p), a_im.reshape(n_g, 1, n_p),
            a_re.reshape(n_g, n_p, 1), a_im.reshape(n_g, n_p, 1),
            t3(b_re), t3(b_im), t3(c_re), t3(c_im)]
    per_g = lambda a: pl.BlockSpec((1,) + a.shape[1:], lambda g: (g,) + (0,) * (a.ndim - 1))
    return pl.pallas_call(
        functools.partial(_s5ops_kernel, lc=lc),
        grid=(n_g,),
        in_specs=[per_g(a) for a in args] + [pl.BlockSpec(til.shape, lambda g: (0, 0))],
        out_specs=[pl.BlockSpec((1, lw, lw), lambda g: (g, 0, 0)),
                   pl.BlockSpec((1, lw, 2 * n_p), lambda g: (g, 0, 0)),
                   pl.BlockSpec((1, 2 * n_p, lw), lambda g: (g, 0, 0)),
                   pl.BlockSpec((1, n_lvl, 2, 2 * n_p), lambda g: (g, 0, 0, 0))],
        out_shape=[jax.ShapeDtypeStruct((n_g, lw, lw), jnp.bfloat16),
                   jax.ShapeDtypeStruct((n_g, lw, 2 * n_p), jnp.bfloat16),
                   jax.ShapeDtypeStruct((n_g, 2 * n_p, lw), jnp.bfloat16),
                   jax.ShapeDtypeStruct((n_g, n_lvl, 2, 2 * n_p), jnp.float32)],
        compiler_params=_cparams("arbitrary"),
        name="s5ops",
    )(*args, til)


def _s5_kernel(u_ref, toep_ref, pm_ref, q_ref, lvl_ref, y_ref, *, n_chunks):
    u = _bf(u_ref[0])
    e = _dot(u, pm_ref[0])
    rows, two_p = e.shape
    half = two_p // 2
    cidx = lax.broadcasted_iota(jnp.int32, e.shape, 0) % n_chunks
    x = e
    n_lvl = lvl_ref.shape[1]
    for j in range(n_lvl):
        sh = 1 << j
        if sh >= n_chunks:
            break
        xs = jnp.where(cidx >= sh, pltpu.roll(x, sh, axis=0), 0.0)
        cf = lvl_ref[0, j]
        x = x + xs * cf[0:1, :] + pltpu.roll(xs, half, axis=1) * cf[1:2, :]
    x_in = jnp.where(cidx >= 1, pltpu.roll(x, 1, axis=0), 0.0)
    x_hi = _bf(x_in)
    x_lo = _bf(x_in - x_hi.astype(jnp.float32))
    q = q_ref[0]
    y_ref[0] = _dot(u, toep_ref[0]) + _dot(x_hi, q) + _dot(x_lo, q)


def _to_groups_kernel(u_ref, o_ref, *, lc, n_c):
    per, nb, lw = o_ref.shape
    slot = lax.broadcasted_iota(jnp.int32, (nb, LANES), 1) // n_c
    a = [u_ref[pl.ds(s, nb, stride=lc), :] for s in range(lc)]
    for gi in range(per):
        for j in range(lw // LANES):
            acc = None
            for ai in range(per):
                shift = ((ai - gi) % per) * n_c
                src = a[j * per + ai]
                piece = src if shift == 0 else pltpu.roll(src, shift, axis=1)
                acc = piece if acc is None else jnp.where(slot == ai, piece, acc)
            o_ref[gi, :, j * LANES:(j + 1) * LANES] = acc


def _from_groups_kernel(y_ref, o_ref, *, lc, n_c):
    per, nb, lw = y_ref.shape
    slot = lax.broadcasted_iota(jnp.int32, (nb, LANES), 1) // n_c
    for s in range(lc):
        j, ai = divmod(s, per)
        acc = None
        for gi in range(per):
            shift = ((gi - ai) % per) * n_c
            src = y_ref[gi, :, j * LANES:(j + 1) * LANES]
            piece = src if shift == 0 else pltpu.roll(src, shift, axis=1)
            acc = piece if acc is None else jnp.where(slot == gi, piece, acc)
        o_ref[pl.ds(s, nb, stride=lc), :] = acc


def _group_relayout(x, n_g, lc, to_groups, tile):
    if to_groups:
        n_tok, width = x.shape
    else:
        n_tok, width = x.shape[1] * lc, x.shape[2] // lc * n_g
    n_c = width // n_g
    per = LANES // n_c
    nb = tile // lc
    tok_spec = pl.BlockSpec((tile, LANES), lambda i, vb: (i, vb))
    grp_spec = pl.BlockSpec((per, nb, lc * n_c), lambda i, vb: (vb, i, 0))
    kern = _to_groups_kernel if to_groups else _from_groups_kernel
    out_shape = (n_g, n_tok // lc, lc * n_c) if to_groups else (n_tok, width)
    return pl.pallas_call(
        functools.partial(kern, lc=lc, n_c=n_c),
        grid=(n_tok // tile, n_g // per),
        in_specs=[tok_spec if to_groups else grp_spec],
        out_specs=grp_spec if to_groups else tok_spec,
        out_shape=jax.ShapeDtypeStruct(out_shape, x.dtype),
        compiler_params=_cparams("arbitrary", "arbitrary"),
        name="to_groups" if to_groups else "from_groups",
    )(x)


def s5_core(u, log_dt, a_re, a_im, b_re, b_im, c_re, c_im):
    bsz, seq, width = u.shape
    n_g, n_p = a_re.shape
    n_c = width // n_g
    lc = min(S5_CHUNK, seq)
    n_chunks = seq // lc
    toep, pm, q, lvl = s5_operators(log_dt, a_re, a_im, b_re, b_im, c_re, c_im, n_chunks, lc)
    tile = min(2048, bsz * seq)
    ug = _group_relayout(u.reshape(bsz * seq, width), n_g, lc, True, tile)
    rows, lw = bsz * n_chunks, lc * n_c
    yg = pl.pallas_call(
        functools.partial(_s5_kernel, n_chunks=n_chunks),
        grid=(n_g,),
        in_specs=[pl.BlockSpec((1, rows, lw), lambda g: (g, 0, 0)),
                  pl.BlockSpec((1, lw, lw), lambda g: (g, 0, 0)),
                  pl.BlockSpec((1, lw, 2 * n_p), lambda g: (g, 0, 0)),
                  pl.BlockSpec((1, 2 * n_p, lw), lambda g: (g, 0, 0)),
                  pl.BlockSpec((1,) + lvl.shape[1:], lambda g: (g, 0, 0, 0))],
        out_specs=pl.BlockSpec((1, rows, lw), lambda g: (g, 0, 0)),
        out_shape=jax.ShapeDtypeStruct((n_g, rows, lw), jnp.float32),
        compiler_params=_cparams("arbitrary"),
        name="s5",
    )(ug, toep, pm, q, lvl)
    return _group_relayout(yg, n_g, lc, False, tile).reshape(bsz, seq, width)


def _gelu_tanh(y):
    return 0.5 * y * (1.0 + jnp.tanh(math.sqrt(2.0 / math.pi) * (y + 0.044715 * (y * y * y))))


def _outproj_kernel(yr_ref, ys_ref, u_ref, x_ref, mods_ref, d_ref, wglu_ref, bglu_ref, wout_ref,
                    gpost_ref, gpre_ref, wrt_ref, x1_ref, h2_ref, lg_ref):
    m = mods_ref[0]
    yr = yr_ref[...]
    y5 = _gelu_tanh(ys_ref[...] + d_ref[...] * u_ref[...])
    y5 = y5 * _sigmoid(_dot(_bf(y5), wglu_ref[...]) + bglu_ref[...])
    wr = yr.shape[-1]
    mixed = _dot(_bf(yr), wout_ref[0:wr, :]) + _dot(_bf(y5), wout_ref[wr:, :])
    x1 = x_ref[...] + m[2:3, :] * _rms(mixed, gpost_ref[...])
    x1_ref[...] = x1
    h2 = _rms(x1, gpre_ref[...]) * (1.0 + m[4:5, :]) + m[3:4, :]
    half = h2.shape[-1] // 2
    _store_row_tiles(h2_ref, _pack_bf16_pair(h2[:, :half], h2[:, half:]))
    w_hi, w_lo = wrt_ref[0], wrt_ref[1]
    h_hi = _bf(h2)
    h_lo = _bf(h2 - h_hi.astype(jnp.float32))
    lg_ref[...] = _dot_nt(w_hi, h_hi) + (_dot_nt(w_hi, h_lo) + _dot_nt(w_lo, h_hi))


def out_proj(y_rwkv, y_s5, u, x, mods3, s5_d, w_glu_bf, b_glu, w_out_bf, g_post, g_pre, w_router_t,
             seq, tm):
    n_tok, dm = x.shape
    wr, ws = y_rwkv.shape[-1], y_s5.shape[-1]
    n_e = w_router_t.shape[0]
    tpb = seq // tm
    tok = lambda w: pl.BlockSpec((tm, w), lambda i: (i, 0))
    full = lambda a: pl.BlockSpec(a.shape, lambda i: (0,) * a.ndim)
    row = lambda t: t.reshape(1, -1)
    wr_hi = _bf(w_router_t)
    wr_split = jnp.stack([wr_hi, _bf(w_router_t - wr_hi.astype(jnp.float32))])
    consts = [row(s5_d), w_glu_bf, row(b_glu), w_out_bf, row(g_post), row(g_pre), wr_split]
    return pl.pallas_call(
        _outproj_kernel,
        grid=(n_tok // tm,),
        in_specs=[tok(wr), tok(ws), tok(ws), tok(dm),
                  pl.BlockSpec((1, N_MODS, dm), lambda i: (i // tpb, 0, 0))] + [full(a) for a in consts],
        out_specs=[tok(dm), _row_tile_spec(tm, dm // 2, lambda i: i),
                   pl.BlockSpec((n_e, tm), lambda i: (0, i))],
        out_shape=[jax.ShapeDtypeStruct((n_tok, dm), jnp.float32),
                   jax.ShapeDtypeStruct((n_tok * (dm // 2 // LANES), LANES), jnp.uint32),
                   jax.ShapeDtypeStruct((n_e, n_tok), jnp.float32)],
        compiler_params=_cparams("arbitrary"),
        name="outproj",
    )(y_rwkv, y_s5, u, x, mods3, *consts)


def _route_kernel(lg_ref, bias_ref, tri_ref, e_ref, w_ref, r_ref, cnt_ref, carry_ref):
    i = pl.program_id(0)

    @pl.when(i == 0)
    def _():
        carry_ref[...] = jnp.zeros_like(carry_ref)

    neg = -jnp.inf
    scores = _sigmoid(lg_ref[...])
    n_e, tm = scores.shape
    choice = scores + bias_ref[...]
    gsz = n_e // ROUTE_GROUPS
    c3 = choice.reshape(ROUTE_GROUPS, gsz, tm)
    io = lax.broadcasted_iota(jnp.int32, c3.shape, 1)
    m1 = jnp.max(c3, axis=1, keepdims=True)
    first = jnp.min(jnp.where(c3 == m1, io, gsz), axis=1, keepdims=True)
    m2 = jnp.max(jnp.where(io == first, neg, c3), axis=1, keepdims=True)
    gs = m1 + m2
    gi = lax.broadcasted_iota(jnp.int32, gs.shape, 0)
    rank = jnp.zeros(gs.shape, jnp.int32)
    for j in range(ROUTE_GROUPS):
        gj = gs[j:j + 1]
        beats = (gj > gs) | ((gj == gs) & (gi > j))
        rank = rank + beats.astype(jnp.int32)
    masked = jnp.where(rank < ROUTE_TOPK_GROUPS, c3, neg).reshape(n_e, tm)

    eio = lax.broadcasted_iota(jnp.int32, (n_e, tm), 0)
    ids, ws = [], []
    mhot = jnp.zeros((n_e, tm), jnp.float32)
    for _ in range(TOP_K):
        m = jnp.max(masked, axis=0, keepdims=True)
        idx = jnp.min(jnp.where(masked == m, eio, n_e), axis=0, keepdims=True)
        sel = eio == idx
        ws.append(jnp.sum(jnp.where(sel, scores, 0.0), axis=0, keepdims=True))
        ids.append(idx)
        masked = jnp.where(sel, neg, masked)
        mhot = jnp.where(sel, 1.0, mhot)
    wsum = ws[0]
    for t in ws[1:]:
        wsum = wsum + t
    before = _dot(_bf(mhot), tri_ref[...]) + carry_ref[...]
    ranks = [jnp.sum(jnp.where(eio == idx, before, 0.0), axis=0, keepdims=True) for idx in ids]
    e_ref[...] = jnp.concatenate(ids, axis=0)
    w_ref[...] = jnp.concatenate(ws, axis=0) / wsum * ROUTE_SCALE
    r_ref[...] = jnp.concatenate(ranks, axis=0).astype(jnp.int32)
    carry_ref[...] = carry_ref[...] + jnp.sum(mhot, axis=1, keepdims=True)
    cnt_ref[...] = carry_ref[...]


def route(logits_t, router_bias, tm):
    n_e, n_tok = logits_t.shape
    tri = (jnp.arange(tm)[:, None] < jnp.arange(tm)[None, :]).astype(jnp.bfloat16)
    kt = lambda: pl.BlockSpec((TOP_K, tm), lambda i: (0, i))
    return pl.pallas_call(
        _route_kernel,
        grid=(n_tok // tm,),
        in_specs=[pl.BlockSpec((n_e, tm), lambda i: (0, i)),
                  pl.BlockSpec((n_e, 1), lambda i: (0, 0)),
                  pl.BlockSpec((tm, tm), lambda i: (0, 0))],
        out_specs=[kt(), kt(), kt(), pl.BlockSpec((n_e, 1), lambda i: (0, 0))],
        out_shape=[jax.ShapeDtypeStruct((TOP_K, n_tok), jnp.int32),
                   jax.ShapeDtypeStruct((TOP_K, n_tok), jnp.float32),
                   jax.ShapeDtypeStruct((TOP_K, n_tok), jnp.int32),
                   jax.ShapeDtypeStruct((n_e, 1), jnp.float32)],
        scratch_shapes=[pltpu.VMEM((n_e, 1), jnp.float32)],
        compiler_params=_cparams("arbitrary"),
        name="route",
    )(logits_t, router_bias.reshape(n_e, 1), tri)


def _dest_kernel(e_ref, r_ref, ps_ref, d_ref):
    n_e = ps_ref.shape[0]
    tm = e_ref.shape[1]
    eio = lax.broadcasted_iota(jnp.int32, (n_e, tm), 0)
    ps = ps_ref[...]
    rows = [jnp.sum(jnp.where(eio == e_ref[k:k + 1, :], ps, 0.0), axis=0, keepdims=True)
            for k in range(TOP_K)]
    d_ref[...] = jnp.concatenate(rows, axis=0).astype(jnp.int32) + r_ref[...]


def dest_rows(e_idx, rank, pad_start, tm):
    n_tok = e_idx.shape[1]
    n_e = pad_start.shape[0]
    kt = lambda: pl.BlockSpec((TOP_K, tm), lambda i: (0, i))
    return pl.pallas_call(
        _dest_kernel,
        grid=(n_tok // tm,),
        in_specs=[kt(), kt(), pl.BlockSpec((n_e, 1), lambda i: (0, 0))],
        out_specs=kt(),
        out_shape=jax.ShapeDtypeStruct((TOP_K, n_tok), jnp.int32),
        compiler_params=_cparams("arbitrary"),
        name="dest",
    )(e_idx, rank, pad_start.astype(jnp.float32).reshape(n_e, 1))


def _dispatch_kernel(fill_start_ref, fill_len_ref, nu_ref, dest_ref, h_ref, xs_ref, zeros, sem, zsem,
                     *, n_s, bm, experts_per_step, tiles_per_step, n_tiles):
    i = pl.program_id(0)
    td = dest_ref.shape[1]

    @pl.when(i == 0)
    def _():
        zeros[...] = jnp.zeros_like(zeros)

    def body(t, carry):
        for k in range(TOP_K):
            pltpu.make_async_copy(_row(h_ref, t, n_s), _row(xs_ref, dest_ref[k, t], n_s),
                                  sem).start(priority=k % 2)
        return carry

    lax.fori_loop(0, td, body, 0, unroll=2)

    fills = []
    for j in range(experts_per_step):
        e = i * experts_per_step + j
        start, length = fill_start_ref[e], fill_len_ref[e]
        piece = bm // 2
        while piece >= 1:
            off = start + (length & ~(2 * piece - 1))
            fills.append(((length & piece) != 0,
                          pltpu.make_async_copy(zeros.at[pl.ds(0, piece * n_s), :],
                                                xs_ref.at[pl.ds(off * n_s, piece * n_s), :], zsem)))
            piece //= 2
    for j in range(tiles_per_step):
        tile = i * tiles_per_step + j
        fills.append(((tile >= nu_ref[0]) & (tile < n_tiles),
                      pltpu.make_async_copy(zeros, xs_ref.at[pl.ds(tile * (bm * n_s), bm * n_s), :], zsem)))
    for cond, cp in fills:
        pl.when(cond)(cp.start)

    for _ in range(TOP_K):
        pltpu.make_async_copy(h_ref, xs_ref.at[pl.ds(0, td * n_s), :], sem).wait()
    for cond, cp in fills:
        pl.when(cond)(cp.wait)


def dispatch(dest, h2p, fill_start, fill_len, n_used, n_tok, n_tiles, td, bm):
    n_s = h2p.shape[0] // n_tok
    n_steps = n_tok // td
    n_e = fill_start.shape[0]
    experts_per_step = -(-n_e // n_steps)
    tiles_per_step = -(-n_tiles // n_steps)
    pad = n_steps * experts_per_step - n_e
    fill_start = jnp.pad(fill_start, (0, pad))
    fill_len = jnp.pad(fill_len, (0, pad))
    return pl.pallas_call(
        functools.partial(_dispatch_kernel, n_s=n_s, bm=bm, experts_per_step=experts_per_step,
                          tiles_per_step=tiles_per_step, n_tiles=n_tiles),
        grid_spec=pltpu.PrefetchScalarGridSpec(
            num_scalar_prefetch=3,
            grid=(n_steps,),
            in_specs=[pl.BlockSpec((TOP_K, td), lambda i, *_: (0, i), memory_space=pltpu.SMEM),
                      _row_tile_spec(td, n_s * LANES, lambda i, *_: i)],
            out_specs=pl.BlockSpec(memory_space=pl.ANY),
            scratch_shapes=[pltpu.VMEM((bm * n_s, LANES), h2p.dtype),
                            pltpu.SemaphoreType.DMA(()), pltpu.SemaphoreType.DMA(())]),
        out_shape=jax.ShapeDtypeStruct((n_tiles * bm * n_s, LANES), h2p.dtype),
        compiler_params=_cparams("arbitrary"),
        name="dispatch",
    )(fill_start, fill_len, n_used, dest, h2p)


def _expert_kernel(te_ref, nu_ref, nt_ref, x_ref, wg_hbm, wu_hbm, wd_hbm, o_ref,
                   wg_st, wu_st, wd_st, wgu_bf, wd_bf, grp, sems, *, bm):
    i = pl.program_id(0)
    ff = wd_bf.shape[0]
    n_used = nu_ref[0]

    def fetch(e, slot):
        return (pltpu.make_async_copy(wg_hbm.at[e], wg_st.at[slot], sems.at[slot]),
                pltpu.make_async_copy(wu_hbm.at[e], wu_st.at[slot], sems.at[slot]),
                pltpu.make_async_copy(wd_hbm.at[e], wd_st.at[slot], sems.at[slot]))

    @pl.when(i == 0)
    def _():
        grp[0] = 0
        for cp in fetch(te_ref[0], 0):
            cp.start()

    @pl.when(i < n_used)
    def _():
        e = te_ref[i]

        @pl.when((i == 0) | (e != te_ref[jnp.maximum(i - 1, 0)]))
        def _():
            slot = grp[0] & 1
            grp[0] = grp[0] + 1
            for cp in fetch(e, slot):
                cp.wait()
            nxt = i + nt_ref[e]

            @pl.when(nxt < n_used)
            def _():
                for cp in fetch(te_ref[jnp.minimum(nxt, n_used - 1)], 1 - slot):
                    cp.start()

            wgu_bf[:, :ff] = _bf(wg_st[slot])
            wgu_bf[:, ff:] = _bf(wu_st[slot])
            wd_bf[...] = _bf(wd_st[slot])

        hi, lo = _unpack_bf16_pair(_load_row_tiles(x_ref, bm))
        x = jnp.concatenate([hi, lo], axis=1)
        gu = _dot(x, wgu_bf[...])
        hid = _silu(gu[:, :ff]) * gu[:, ff:]
        y = _dot(_bf(hid), wd_bf[...])
        half = y.shape[1] // 2
        _store_row_tiles(o_ref, _pack_bf16_pair(y[:, :half], y[:, half:]))

    @pl.when(i >= nu_ref[0])
    def _():
        o_ref[...] = jnp.zeros_like(o_ref)


def expert_ffn(tile_e, n_used, tiles_per_expert, xs, n_rows, w_gate, w_up, w_down, bm):
    dm = w_gate.shape[1]
    half = dm // 2
    ff = w_gate.shape[-1]
    rows_in = _row_tile_spec(bm, half, lambda i, te, nu, nt: jnp.minimum(i, nu[0] - 1))
    rows_out = _row_tile_spec(bm, half, lambda i, te, nu, nt: i)
    hbm = pl.BlockSpec(memory_space=pl.ANY)
    return pl.pallas_call(
        functools.partial(_expert_kernel, bm=bm),
        grid_spec=pltpu.PrefetchScalarGridSpec(
            num_scalar_prefetch=3,
            grid=(n_rows // bm,),
            in_specs=[rows_in, hbm, hbm, hbm],
            out_specs=rows_out,
            scratch_shapes=[pltpu.VMEM((2, dm, ff), jnp.float32),
                            pltpu.VMEM((2, dm, ff), jnp.float32),
                            pltpu.VMEM((2, ff, dm), jnp.float32),
                            pltpu.VMEM((dm, 2 * ff), jnp.bfloat16),
                            pltpu.VMEM((ff, dm), jnp.bfloat16),
                            pltpu.SMEM((1,), jnp.int32),
                            pltpu.SemaphoreType.DMA((2,))]),
        out_shape=jax.ShapeDtypeStruct((n_rows * (half // LANES), LANES), jnp.uint32),
        compiler_params=_cparams("arbitrary"),
        name="expert",
    )(tile_e, n_used, tiles_per_expert, xs, w_gate, w_up, w_down)


def _combine_kernel(dest_ref, dnext_ref, w_ref, x1_ref, h2_ref, mods_ref, g_ref, wsg_ref, wsu_ref,
                    wsd_ref, ys_ref, o_ref, gbuf, sems, *, n_s):
    i = pl.program_id(0)
    n_steps = pl.num_programs(0)
    tc = x1_ref.shape[0]

    def issue(d_ref, slot):
        def body(t, carry):
            for k in range(TOP_K):
                pltpu.make_async_copy(_row(ys_ref, d_ref[k, t], n_s), _row(gbuf.at[slot, k], t, n_s),
                                      sems.at[slot]).start(priority=k % 2)
            return carry
        lax.fori_loop(0, tc, body, 0, unroll=2)

    def finish(slot):
        for k in range(TOP_K):
            pltpu.make_async_copy(ys_ref.at[pl.ds(0, tc * n_s), :], gbuf.at[slot, k],
                                  sems.at[slot]).wait()
        hi, lo = _unpack_bf16_pair(_load_row_tiles(h2_ref, tc))
        h = jnp.concatenate([hi, lo], axis=1)
        hid = _silu(_dot(h, wsg_ref[...])) * _dot(h, wsu_ref[...])
        y = _dot(_bf(hid), wsd_ref[...])
        w = w_ref[...]
        acc_hi = acc_lo = None
        for k in range(TOP_K):
            g_hi, g_lo = _unpack_pair_f32(_load_row_tiles(gbuf.at[slot, k], tc))
            wk = w[:, k:k + 1]
            acc_hi = wk * g_hi if acc_hi is None else acc_hi + wk * g_hi
            acc_lo = wk * g_lo if acc_lo is None else acc_lo + wk * g_lo
        y = y + jnp.concatenate([acc_hi, acc_lo], axis=1)
        m = mods_ref[0]
        o_ref[...] = x1_ref[...] + m[5:6, :] * _rms(y, g_ref[...])

    @pl.when(i == 0)
    def _():
        issue(dest_ref, 0)

    for slot in range(2):
        @pl.when(i % 2 == slot)
        def _(slot=slot):
            @pl.when(i + 1 < n_steps)
            def _():
                issue(dnext_ref, 1 - slot)
            finish(slot)


def combine(dest, w_tk, x1, h2p, mods3, g_post, ws_gate_bf, ws_up_bf, ws_down_bf, ys, seq, tc):
    n_tok, dm = x1.shape
    n_s = dm // 2 // LANES
    n_steps = n_tok // tc
    tpb = seq // tc
    full = lambda a: pl.BlockSpec(a.shape, lambda i: (0,) * a.ndim)
    tok = lambda w: pl.BlockSpec((tc, w), lambda i: (i, 0))
    g2 = g_post.reshape(1, dm)
    return pl.pallas_call(
        functools.partial(_combine_kernel, n_s=n_s),
        grid=(n_steps,),
        in_specs=[pl.BlockSpec((TOP_K, tc), lambda i: (0, i), memory_space=pltpu.SMEM),
                  pl.BlockSpec((TOP_K, tc), lambda i: (0, jnp.minimum(i + 1, n_steps - 1)),
                               memory_space=pltpu.SMEM),
                  tok(TOP_K),
                  tok(dm), _row_tile_spec(tc, dm // 2, lambda i: i),
                  pl.BlockSpec((1, N_MODS, dm), lambda i: (i // tpb, 0, 0)),
                  full(g2), full(ws_gate_bf), full(ws_up_bf), full(ws_down_bf),
                  pl.BlockSpec(memory_space=pl.ANY)],
        out_specs=tok(dm),
        out_shape=jax.ShapeDtypeStruct((n_tok, dm), jnp.float32),
        scratch_shapes=[pltpu.VMEM((2, TOP_K, tc * n_s, LANES), jnp.uint32),
                        pltpu.SemaphoreType.DMA((2,))],
        compiler_params=_cparams("arbitrary"),
        name="combine",
    )(dest, dest, w_tk, x1, h2p, mods3, g2, ws_gate_bf, ws_up_bf, ws_down_bf, ys)


def moe_ffn(h2, x1, logits_t, mods3, router_bias, w_gate, w_up, w_down, ws_gate, ws_up, ws_down,
            g_post, seq):
    n_tok, dm = x1.shape
    n_e = w_gate.shape[0]
    bm = EXPERT_ROW_TILE
    e_idx, w_kt, rank, cnt = route(logits_t, router_bias, min(512, n_tok))
    counts = cnt[:, 0].astype(jnp.int32)
    padded = (counts + bm - 1) // bm * bm
    pad_end = jnp.cumsum(padded)
    pad_start = (pad_end - padded).astype(jnp.int32)
    n_tiles = -(-(n_tok * TOP_K) // bm) + n_e
    n_used = (pad_end[-1] // bm).astype(jnp.int32)
    tile_start = jnp.arange(n_tiles, dtype=jnp.int32) * bm
    tile_e = jnp.minimum(jnp.sum((pad_end[None, :] <= tile_start[:, None]).astype(jnp.int32), axis=1),
                         n_e - 1).astype(jnp.int32)
    tile_e = jnp.where(jnp.arange(n_tiles) < n_used, tile_e, tile_e[n_used - 1])
    dest = dest_rows(e_idx, rank, pad_start, min(512, n_tok))
    n_used1 = n_used.reshape(1)
    xs = dispatch(dest, h2, pad_start + counts, padded - counts, n_used1, n_tok, n_tiles,
                  min(MOE_TOKEN_TILE, n_tok), bm)
    ys = expert_ffn(tile_e, n_used1, (padded // bm).astype(jnp.int32), xs, n_tiles * bm,
                    w_gate, w_up, w_down, bm)
    bf = lambda a: a.astype(jnp.bfloat16)
    return combine(dest, w_kt.T, x1, h2, mods3, g_post, bf(ws_gate), bf(ws_up), bf(ws_down), ys,
                   seq, min(MOE_TOKEN_TILE, n_tok))


def kernel(x, c, w_ada, b_ada, g_pre_mix, g_post_mix, g_pre_ffn, g_post_ffn, w_in, mu_shift, rwkv_w0, rwkv_w2, rwkv_a0, rwkv_a2, rwkv_g2, rwkv_k_k, rwkv_k_a, rwkv_r_k, rwkv_ln_w, rwkv_ln_b, s5_log_dt, s5_a_re, s5_a_im, s5_b_re, s5_b_im, s5_c_re, s5_c_im, s5_d, s5_w_glu, s5_b_glu, w_out, w_router, router_bias, w_gate, w_up, w_down, ws_gate, ws_up, ws_down):
    bsz, seq, dm = x.shape
    depth = w_ada.shape[0]
    bf = lambda a: a.astype(jnp.bfloat16)
    tm = min(512, seq)
    for l in range(depth):
        mods3 = ada_mods(c, w_ada[l], b_ada[l]).reshape(bsz, N_MODS, dm)
        n_rwkv = mu_shift.shape[-1]
        p, u = in_proj(x, mods3, g_pre_mix[l], bf(w_in[l]), n_rwkv, tm)
        y_rwkv = rwkv_mixer(p, mu_shift[l], rwkv_w0[l], rwkv_w2[l], rwkv_a0[l], rwkv_a2[l],
                            rwkv_g2[l], rwkv_k_k[l], rwkv_k_a[l], rwkv_r_k[l].reshape(-1),
                            rwkv_ln_w[l], rwkv_ln_b[l])
        y_s5 = s5_core(u, s5_log_dt[l], s5_a_re[l], s5_a_im[l], s5_b_re[l], s5_b_im[l],
                       s5_c_re[l], s5_c_im[l])
        flat = lambda a: a.reshape(bsz * seq, a.shape[-1])
        x1, h2, logits_t = out_proj(flat(y_rwkv), flat(y_s5), flat(u), flat(x), mods3, s5_d[l],
                                    bf(s5_w_glu[l]), s5_b_glu[l], bf(w_out[l]), g_post_mix[l],
                                    g_pre_ffn[l], w_router[l].T, seq, tm)
        out = moe_ffn(h2, x1, logits_t, mods3, router_bias[l], w_gate[l], w_up[l], w_down[l],
                      ws_gate[l], ws_up[l], ws_down[l], g_post_ffn[l], seq)
        x = out.reshape(bsz, seq, dm)
    return x
```

```python
import functools
import math

import jax
import jax.numpy as jnp
from jax import lax
from jax.experimental import pallas as pl
from jax.experimental.pallas import tpu as pltpu

NORM_EPS = 1e-6
LNX_EPS = 64e-5
L2_EPS = 1e-12
S5_MAX_REAL = -1e-4
ROUTE_SCALE = 2.5
N_MODS = 6

RWKV_HEAD_DIM = 64
DECAY_LORA = 64
ICLR_LORA = 64
GATE_LORA = 128
TOP_K = 8
ROUTE_GROUPS = 8
ROUTE_TOPK_GROUPS = 4

RWKV_CHUNK = 64
S5_CHUNK = 16
EXPERT_ROW_TILE = 512
MOE_TOKEN_TILE = 512

VMEM_LIMIT = 56 * 1024 * 1024

HI = lax.Precision.HIGHEST


def _cparams(*sem):
    return pltpu.CompilerParams(dimension_semantics=sem, vmem_limit_bytes=VMEM_LIMIT)


def _dot(a, b, precision=None):
    return jnp.dot(a, b, preferred_element_type=jnp.float32, precision=precision)


def _dot_nt(a, b, precision=None):
    return lax.dot_general(a, b, (((1,), (1,)), ((), ())),
                           preferred_element_type=jnp.float32, precision=precision)


def _bdot(a, b):
    return lax.dot_general(a, b, (((2,), (1,)), ((0,), (0,))), preferred_element_type=jnp.float32)


def _bdot_nt(a, b):
    return lax.dot_general(a, b, (((2,), (2,)), ((0,), (0,))), preferred_element_type=jnp.float32)


def _bf(x):
    return x.astype(jnp.bfloat16)


def _pack_bf16_pair(hi, lo):
    hb = lax.bitcast_convert_type(_bf(hi).astype(jnp.float32), jnp.uint32)
    lb = lax.bitcast_convert_type(_bf(lo).astype(jnp.float32), jnp.uint32)
    return (hb & jnp.uint32(0xFFFF0000)) | (lb >> 16)


def _unpack_pair_f32(w):
    hi = lax.bitcast_convert_type(w & jnp.uint32(0xFFFF0000), jnp.float32)
    lo = lax.bitcast_convert_type(w << 16, jnp.float32)
    return hi, lo


def _unpack_bf16_pair(w):
    hi, lo = _unpack_pair_f32(w)
    return _bf(hi), _bf(lo)


LANES = 128


def _row_tile_spec(rows, width, row_block):
    return pl.BlockSpec((rows * (width // LANES), LANES), lambda *a: (row_block(*a), 0))


def _store_row_tiles(ref, x):
    rows = x.shape[0]
    n_s = ref.shape[0] // rows
    for s in range(n_s):
        ref[pl.ds(s, rows, stride=n_s), :] = x[:, s * LANES:(s + 1) * LANES]


def _load_row_tiles(ref, rows):
    n_s = ref.shape[0] // rows
    return jnp.concatenate([ref[pl.ds(s, rows, stride=n_s), :] for s in range(n_s)], axis=1)


def _row(ref, r, n_s):
    return ref.at[pl.ds(pl.multiple_of(r * n_s, n_s), n_s), :]


def _sigmoid(x):
    return 1.0 / (1.0 + jnp.exp(-x))


def _silu(x):
    return x * _sigmoid(x)


def _rms(x, gain):
    return x * lax.rsqrt(jnp.mean(x * x, axis=-1, keepdims=True) + NORM_EPS) * gain


def _ada_kernel(c_ref, w_ref, b_ref, o_ref):
    c = c_ref[...]
    o_ref[...] = _dot(_silu(c), w_ref[...], HI) + b_ref[...]


def ada_mods(c, w_ada, b_ada):
    bsz, dm = c.shape
    n = w_ada.shape[1]
    tn = dm
    return pl.pallas_call(
        _ada_kernel,
        grid=(n // tn,),
        in_specs=[pl.BlockSpec((bsz, dm), lambda j: (0, 0)),
                  pl.BlockSpec((dm, tn), lambda j: (0, j)),
                  pl.BlockSpec((1, tn), lambda j: (0, j))],
        out_specs=pl.BlockSpec((bsz, tn), lambda j: (0, j)),
        out_shape=jax.ShapeDtypeStruct((bsz, n), jnp.float32),
        compiler_params=_cparams("arbitrary"),
        name="ada",
    )(c, w_ada, b_ada.reshape(1, n))


def _inproj_kernel(x_ref, mods_ref, g_ref, w_ref, p_ref, u_ref):
    x = x_ref[0]
    m = mods_ref[0]
    h = _rms(x, g_ref[...]) * (1.0 + m[1:2, :]) + m[0:1, :]
    proj = _dot(_bf(h), w_ref[...])
    n_p = p_ref.shape[-1]
    p_ref[0] = proj[:, :n_p]
    u_ref[0] = proj[:, n_p:]


def in_proj(x, mods3, g_pre, w_in_bf, n_rwkv, tm):
    bsz, seq, dm = x.shape
    n = w_in_bf.shape[1]
    n_s5 = n - n_rwkv
    return pl.pallas_call(
        _inproj_kernel,
        grid=(bsz, seq // tm),
        in_specs=[pl.BlockSpec((1, tm, dm), lambda b, i: (b, i, 0)),
                  pl.BlockSpec((1, N_MODS, dm), lambda b, i: (b, 0, 0)),
                  pl.BlockSpec((1, dm), lambda b, i: (0, 0)),
                  pl.BlockSpec((dm, n), lambda b, i: (0, 0))],
        out_specs=[pl.BlockSpec((1, tm, n_rwkv), lambda b, i: (b, i, 0)),
                   pl.BlockSpec((1, tm, n_s5), lambda b, i: (b, i, 0))],
        out_shape=[jax.ShapeDtypeStruct((bsz, seq, n_rwkv), jnp.float32),
                   jax.ShapeDtypeStruct((bsz, seq, n_s5), jnp.float32)],
        compiler_params=_cparams("arbitrary", "arbitrary"),
        name="inproj",
    )(x, mods3, g_pre.reshape(1, dm), w_in_bf)


RWKV_TILE = 256


def _split_dot(x, m01, terms):
    blk = m01.shape[0]
    pieces = []
    rem = x
    for _ in range(terms):
        piece = _bf(rem)
        pieces.append(piece)
        rem = rem - piece.astype(jnp.float32)
    cols = []
    for c in range(x.shape[1] // blk):
        acc = None
        for piece in pieces:
            part = _dot(piece[:, c * blk:(c + 1) * blk], m01)
            acc = part if acc is None else acc + part
        cols.append(acc)
    return cols[0] if len(cols) == 1 else jnp.concatenate(cols, axis=1)


def _rwkv_kernel(p_ref, mu_ref, w0_ref, w2_ref, a0_ref, a2_ref, g2_ref, kk_ref, ka_ref,
                  rk_ref, lnw_ref, lnb_ref, bd_ref, wg_ref, wu_ref, wd_ref, o_ref, wgu_o, wd_o,
                  s_ref, carry_ref, rt_ref, at_ref, bt_ref, kt_ref, v_ref, wl_ref, y_ref,
                  *, width, chunk):
    ff = wg_ref.shape[-1]
    for j in range(wg_ref.shape[0]):
        wgu_o[j, :, :ff] = _bf(wg_ref[j])
        wgu_o[j, :, ff:] = _bf(wu_ref[j])
        wd_o[j] = _bf(wd_ref[j])

    hd = RWKV_HEAD_DIM
    pw = 2 * hd
    pairs = width // pw
    i = pl.program_id(1)

    @pl.when(i == 0)
    def _():
        s_ref[...] = jnp.zeros_like(s_ref)
        carry_ref[...] = jnp.zeros_like(carry_ref)

    p = p_ref[0]
    n_t = p.shape[0]
    n_chunks = n_t // chunk
    row = lax.broadcasted_iota(jnp.int32, p.shape, 0)
    prev = jnp.where(row == 0, carry_ref[0:1, :], pltpu.roll(p, 1, axis=0))
    carry_ref[0:1, :] = p[n_t - 1:n_t, :]
    pm = p + (prev - p) * mu_ref[...]

    r = pm[:, 0:width]
    k = pm[:, width:2 * width]
    v = pm[:, 2 * width:3 * width]
    c0 = 3 * width
    w_lo = pm[:, c0:c0 + DECAY_LORA]
    a_lo = pm[:, c0 + DECAY_LORA:c0 + DECAY_LORA + ICLR_LORA]
    g_lo = pm[:, c0 + DECAY_LORA + ICLR_LORA:]

    z = w0_ref[...] + _dot(_bf(jnp.tanh(w_lo)), w2_ref[...])
    softplus_neg = jnp.maximum(-z, 0.0) + jnp.log(1.0 + jnp.exp(-jnp.abs(z)))
    logd = -jnp.exp(-softplus_neg - 0.5)
    iclr = _sigmoid(a0_ref[...] + _dot(_bf(a_lo), a2_ref[...]))
    gate = _dot(_bf(_sigmoid(g_lo)), g2_ref[...])

    bd = bd_ref[...]
    kk = k * kk_ref[...]
    kk = kk / jnp.maximum(jnp.sqrt(_split_dot(kk * kk, bd, 2)), L2_EPS)
    k2 = k * (1.0 + (iclr - 1.0) * ka_ref[...])
    bonus = _split_dot(r * k2 * rk_ref[...], bd, 2) * v

    ti = lax.broadcasted_iota(jnp.int32, (n_t, n_t), 0)
    si = lax.broadcasted_iota(jnp.int32, (n_t, n_t), 1)
    tri = jnp.where((ti >= si) & (ti // chunk == si // chunk), 1.0, 0.0).astype(jnp.bfloat16)
    cum = _split_dot_lhs(tri, logd, 3)
    e_pos = jnp.exp(cum)
    e_neg = jnp.exp(-cum)
    rt_ref[...] = _bf(r * e_pos)
    at_ref[...] = _bf(-kk * jnp.exp(cum - logd))
    bt_ref[...] = _bf(kk * iclr * e_neg)
    kt_ref[...] = _bf(k2 * e_neg)
    v_ref[...] = _bf(v)
    for c in range(n_chunks):
        wl_ref[c:c + 1, :] = e_pos[(c + 1) * chunk - 1:(c + 1) * chunk, :]

    two_l = 2 * chunk
    lane = lax.broadcasted_iota(jnp.int32, (chunk, pw), 1)
    lane0 = lane < hd
    bi = lax.broadcasted_iota(jnp.int32, (two_l, two_l), 0)
    bj = lax.broadcasted_iota(jnp.int32, (two_l, two_l), 1)
    same = (bi // chunk) == (bj // chunk)
    low_strict = same & (bi > bj)
    low_incl = same & (bi >= bj)
    eye_t = jnp.where(bi == bj, 1.0, 0.0)
    pi_ = lax.broadcasted_iota(jnp.int32, (pw, pw), 0)
    pj_ = lax.broadcasted_iota(jnp.int32, (pw, pw), 1)
    eye_p = jnp.where(pi_ == pj_, 1.0, 0.0)
    n_sq = max(1, int(math.ceil(math.log2(chunk))) - 1)
    zero = jnp.zeros((), jnp.bfloat16)

    def stack(x):
        return jnp.concatenate([jnp.where(lane0, x, zero), jnp.where(lane0, zero, x)], axis=0)

    def gather(ref):
        return jnp.stack([stack(ref[c * chunk:(c + 1) * chunk, hp * pw:(hp + 1) * pw])
                          for c in range(n_chunks) for hp in range(pairs)], axis=0)

    a_s, b_s, k_s, r_s, v_s = (gather(ref) for ref in (at_ref, bt_ref, kt_ref, rt_ref, v_ref))
    wl = jnp.stack([wl_ref[c:c + 1, hp * pw:(hp + 1) * pw]
                    for c in range(n_chunks) for hp in range(pairs)], axis=0)
    gram = _bdot_nt(jnp.concatenate([a_s, r_s], axis=1), jnp.concatenate([b_s, k_s], axis=1))
    m_ab = jnp.where(low_strict, gram[:, :two_l, :two_l], 0.0)
    m_ak = jnp.where(low_strict, gram[:, :two_l, two_l:], 0.0)
    n_rb = jnp.where(low_incl, gram[:, two_l:, :two_l], 0.0)
    n_rk = jnp.where(low_incl, gram[:, two_l:, two_l:], 0.0)
    t_inv = eye_t + m_ab
    m_pow = _bf(m_ab)
    for _ in range(n_sq):
        m_pow = _bf(_bdot(m_pow, m_pow))
        t_inv = t_inv + _bdot(_bf(t_inv), m_pow)
    makv = _bdot(_bf(m_ak), v_s)
    tx_bf = _bf(_bdot(_bf(t_inv), jnp.concatenate([a_s, _bf(makv)], axis=2)))
    nx = _bdot(_bf(n_rb), tx_bf)
    rbar = _bf(r_s.astype(jnp.float32) + nx[:, :, :pw])
    y0 = nx[:, :, pw:] + _bdot(_bf(n_rk), v_s)
    tb = _bdot(jnp.swapaxes(tx_bf, 1, 2), b_s)
    pmat = _bf((eye_p + tb[:, :pw, :]) * wl)
    dmat = (tb[:, pw:, :] + _bdot(jnp.swapaxes(v_s, 1, 2), k_s)) * wl

    s = s_ref[...]
    for c in range(n_chunks):
        sel = slice(c * pairs, (c + 1) * pairs)
        s_bf = _bf(s)
        ys = _bdot_nt(rbar[sel], s_bf) + y0[sel]
        yc_ = ys[:, :chunk, :] + ys[:, chunk:, :]
        for hp in range(pairs):
            y_ref[c * chunk:(c + 1) * chunk, hp * pw:(hp + 1) * pw] = yc_[hp]
        s = _bdot(s_bf, pmat[sel]) + dmat[sel]
    s_ref[...] = s

    y = y_ref[...]
    inv_hd = 1.0 / hd
    mean = _split_dot(y, bd, 2) * inv_hd
    yc = y - mean
    var = _split_dot(yc * yc, bd, 2) * inv_hd
    yn = yc * lax.rsqrt(var + LNX_EPS) * lnw_ref[...] + lnb_ref[...]
    o_ref[0] = (yn + bonus) * gate


def _split_dot_lhs(m01, x, terms):
    acc = None
    rem = x
    for _ in range(terms):
        piece = _bf(rem)
        part = _dot(m01, piece)
        acc = part if acc is None else acc + part
        rem = rem - piece.astype(jnp.float32)
    return acc


def rwkv_mixer(p, mu, w0, w2, a0, a2, g2, k_k, k_a, r_k, ln_w, ln_b, w_gate, w_up, w_down):
    bsz, seq, n_p = p.shape
    width = w0.shape[-1]
    pairs = width // (2 * RWKV_HEAD_DIM)
    chunk = min(RWKV_CHUNK, seq)
    tile = min(RWKV_TILE, seq)
    n_e, dm, ff = w_gate.shape
    steps_per_b = seq // tile
    n_steps = bsz * steps_per_b
    assert n_e % n_steps == 0, (n_e, n_steps)
    eps = n_e // n_steps
    wspec = lambda a, b_: pl.BlockSpec((eps, a, b_), lambda b, i: (b * steps_per_b + i, 0, 0))
    hid = jnp.arange(2 * RWKV_HEAD_DIM, dtype=jnp.int32) // RWKV_HEAD_DIM
    bd = (hid[:, None] == hid[None, :]).astype(jnp.bfloat16)
    row = lambda t: t.reshape(1, -1)
    full = lambda a: pl.BlockSpec(a.shape, lambda b, i: (0,) * a.ndim)
    consts = [row(mu), row(w0), _bf(w2), row(a0), _bf(a2), _bf(g2), row(k_k), row(k_a), row(r_k),
              row(ln_w), row(ln_b), bd]
    act = lambda: pltpu.VMEM((tile, width), jnp.bfloat16)
    return pl.pallas_call(
        functools.partial(_rwkv_kernel, width=width, chunk=chunk),
        grid=(bsz, seq // tile),
        in_specs=([pl.BlockSpec((1, tile, n_p), lambda b, i: (b, i, 0))] + [full(a) for a in consts]
                  + [wspec(dm, ff), wspec(dm, ff), wspec(ff, dm)]),
        out_specs=[pl.BlockSpec((1, tile, width), lambda b, i: (b, i, 0)),
                   wspec(dm, 2 * ff), wspec(ff, dm)],
        out_shape=[jax.ShapeDtypeStruct((bsz, seq, width), jnp.float32),
                   jax.ShapeDtypeStruct((n_e, dm, 2 * ff), jnp.bfloat16),
                   jax.ShapeDtypeStruct((n_e, ff, dm), jnp.bfloat16)],
        scratch_shapes=[pltpu.VMEM((pairs, 2 * RWKV_HEAD_DIM, 2 * RWKV_HEAD_DIM), jnp.float32),
                        pltpu.VMEM((8, n_p), jnp.float32),
                        act(), act(), act(), act(), act(),
                        pltpu.VMEM((max(8, tile // chunk), width), jnp.float32),
                        pltpu.VMEM((tile, width), jnp.float32)],
        compiler_params=_cparams("arbitrary", "arbitrary"),
        name="rwkv",
    )(p, *consts, w_gate, w_up, w_down)


def _s5_discretise(a_re, a_im, dt):
    lam_re = jnp.minimum(a_re, S5_MAX_REAL)
    lam_im = a_im
    mag = jnp.exp(lam_re * dt)
    ang = lam_im * dt
    ab_re, ab_im = mag * jnp.cos(ang), mag * jnp.sin(ang)
    den = lam_re * lam_re + lam_im * lam_im
    n_re, n_im = ab_re - 1.0, ab_im
    q_re = (n_re * lam_re + n_im * lam_im) / den
    q_im = (n_im * lam_re - n_re * lam_im) / den
    return ab_re, ab_im, q_re, q_im


def _s5_power(ab_re, ab_im, t, t_max):
    pr = jnp.ones(t.shape, jnp.float32)
    pi = jnp.zeros(t.shape, jnp.float32)
    br, bi = ab_re, ab_im
    for j in range(max(1, int(t_max).bit_length())):
        bit = ((t >> j) & 1) == 1
        pr, pi = jnp.where(bit, pr * br - pi * bi, pr), jnp.where(bit, pr * bi + pi * br, pi)
        br, bi = br * br - bi * bi, 2.0 * br * bi
    return pr, pi


def _s5ops_kernel(ldt_ref, ar_row, ai_row, ar_col, ai_col, bt_re, bt_im, ct_re, ct_im, til_ref,
                  toep_ref, pm_ref, q_ref, lvl_ref, *, lc):
    n_c, n_p = bt_re.shape[1], bt_re.shape[2]
    lw = lc * n_c
    dt = jnp.exp(ldt_ref[0])

    abr_re, abr_im, q_re, q_im = _s5_discretise(ar_row[0], ai_row[0], dt)
    bbt_re = q_re * bt_re[0] - q_im * bt_im[0]
    bbt_im = q_re * bt_im[0] + q_im * bt_re[0]
    s_row = lax.broadcasted_iota(jnp.int32, (lw, n_p), 0) // n_c
    pw_re, pw_im = _s5_power(abr_re, abr_im, lc - 1 - s_row, lc)
    tb_re = jnp.concatenate([bbt_re] * lc, axis=0)
    tb_im = jnp.concatenate([bbt_im] * lc, axis=0)
    pm_ref[0] = _bf(jnp.concatenate([pw_re * tb_re - pw_im * tb_im,
                                     pw_re * tb_im + pw_im * tb_re], axis=1))

    abc_re, abc_im, _, _ = _s5_discretise(ar_col[0], ai_col[0], dt)
    til = til_ref[...]
    c_re = _dot(ct_re[0], til, HI)
    c_im = _dot(ct_im[0], til, HI)
    t_lane = lax.broadcasted_iota(jnp.int32, (n_p, lw), 1) // n_c
    p0_re, p0_im = _s5_power(abc_re, abc_im, t_lane, lc)
    p1_re, p1_im = p0_re * abc_re - p0_im * abc_im, p0_re * abc_im + p0_im * abc_re
    ca0_re, ca0_im = p0_re * c_re - p0_im * c_im, p0_re * c_im + p0_im * c_re
    q_ref[0] = _bf(jnp.concatenate([p1_re * c_re - p1_im * c_im,
                                    -(p1_re * c_im + p1_im * c_re)], axis=0))
    r0 = _dot(bbt_re, ca0_re, HI) - _dot(bbt_im, ca0_im, HI)
    lane = lax.broadcasted_iota(jnp.int32, (n_c, lw), 1)
    for s in range(lc):
        blk = r0 if s == 0 else jnp.where(lane >= s * n_c, pltpu.roll(r0, s * n_c, axis=1), 0.0)
        toep_ref[0, s * n_c:(s + 1) * n_c, :] = _bf(blk)

    cr, ci = _s5_power(abr_re, abr_im, jnp.full(abr_re.shape, lc, jnp.int32), lc)
    for j in range(lvl_ref.shape[1]):
        lvl_ref[0, j] = jnp.concatenate([jnp.concatenate([cr, cr], axis=1),
                                         jnp.concatenate([-ci, ci], axis=1)], axis=0)
        cr, ci = cr * cr - ci * ci, 2.0 * cr * ci


def s5_operators(log_dt, a_re, a_im, b_re, b_im, c_re, c_im, n_chunks, lc):
    n_g, n_p = a_re.shape
    n_c = b_re.shape[-1]
    lw = lc * n_c
    n_lvl = max(1, int(math.ceil(math.log2(n_chunks))))
    til = (jnp.arange(lw)[None, :] % n_c == jnp.arange(n_c)[:, None]).astype(jnp.float32)
    t3 = lambda a: jnp.swapaxes(a, 1, 2)
    args = [log_dt.reshape(n_g, 1, 1), a_re.reshape(n_g, 1, n_p), a_im.reshape(n_g, 1, n_p),
            a_re.reshape(n_g, n_p, 1), a_im.reshape(n_g, n_p, 1),
            t3(b_re), t3(b_im), t3(c_re), t3(c_im)]
    per_g = lambda a: pl.BlockSpec((1,) + a.shape[1:], lambda g: (g,) + (0,) * (a.ndim - 1))
    return pl.pallas_call(
        functools.partial(_s5ops_kernel, lc=lc),
        grid=(n_g,),
        in_specs=[per_g(a) for a in args] + [pl.BlockSpec(til.shape, lambda g: (0, 0))],
        out_specs=[pl.BlockSpec((1, lw, lw), lambda g: (g, 0, 0)),
                   pl.BlockSpec((1, lw, 2 * n_p), lambda g: (g, 0, 0)),
                   pl.BlockSpec((1, 2 * n_p, lw), lambda g: (g, 0, 0)),
                   pl.BlockSpec((1, n_lvl, 2, 2 * n_p), lambda g: (g, 0, 0, 0))],
        out_shape=[jax.ShapeDtypeStruct((n_g, lw, lw), jnp.bfloat16),
                   jax.ShapeDtypeStruct((n_g, lw, 2 * n_p), jnp.bfloat16),
                   jax.ShapeDtypeStruct((n_g, 2 * n_p, lw), jnp.bfloat16),
                   jax.ShapeDtypeStruct((n_g, n_lvl, 2, 2 * n_p), jnp.float32)],
        compiler_params=_cparams("arbitrary"),
        name="s5ops",
    )(*args, til)


def _s5_kernel(u_ref, toep_ref, pm_ref, q_ref, lvl_ref, y_ref, *, n_chunks):
    u = _bf(u_ref[0])
    e = _dot(u, pm_ref[0])
    rows, two_p = e.shape
    half = two_p // 2
    cidx = lax.broadcasted_iota(jnp.int32, e.shape, 0) % n_chunks
    x = e
    n_lvl = lvl_ref.shape[1]
    for j in range(n_lvl):
        sh = 1 << j
        if sh >= n_chunks:
            break
        xs = jnp.where(cidx >= sh, pltpu.roll(x, sh, axis=0), 0.0)
        cf = lvl_ref[0, j]
        x = x + xs * cf[0:1, :] + pltpu.roll(xs, half, axis=1) * cf[1:2, :]
    x_in = jnp.where(cidx >= 1, pltpu.roll(x, 1, axis=0), 0.0)
    x_hi = _bf(x_in)
    x_lo = _bf(x_in - x_hi.astype(jnp.float32))
    q = q_ref[0]
    y_ref[0] = _dot(u, toep_ref[0]) + _dot(x_hi, q) + _dot(x_lo, q)


def _to_groups_kernel(u_ref, o_ref, *, lc, n_c):
    per, nb, lw = o_ref.shape
    slot = lax.broadcasted_iota(jnp.int32, (nb, LANES), 1) // n_c
    a = [u_ref[pl.ds(s, nb, stride=lc), :] for s in range(lc)]
    for gi in range(per):
        for j in range(lw // LANES):
            acc = None
            for ai in range(per):
                shift = ((ai - gi) % per) * n_c
                src = a[j * per + ai]
                piece = src if shift == 0 else pltpu.roll(src, shift, axis=1)
                acc = piece if acc is None else jnp.where(slot == ai, piece, acc)
            o_ref[gi, :, j * LANES:(j + 1) * LANES] = acc


def _from_groups_kernel(y_ref, o_ref, *, lc, n_c):
    per, nb, lw = y_ref.shape
    slot = lax.broadcasted_iota(jnp.int32, (nb, LANES), 1) // n_c
    for s in range(lc):
        j, ai = divmod(s, per)
        acc = None
        for gi in range(per):
            shift = ((gi - ai) % per) * n_c
            src = y_ref[gi, :, j * LANES:(j + 1) * LANES]
            piece = src if shift == 0 else pltpu.roll(src, shift, axis=1)
            acc = piece if acc is None else jnp.where(slot == gi, piece, acc)
        o_ref[pl.ds(s, nb, stride=lc), :] = acc


def _group_relayout(x, n_g, lc, to_groups, tile):
    if to_groups:
        n_tok, width = x.shape
    else:
        n_tok, width = x.shape[1] * lc, x.shape[2] // lc * n_g
    n_c = width // n_g
    per = LANES // n_c
    nb = tile // lc
    tok_spec = pl.BlockSpec((tile, LANES), lambda i, vb: (i, vb))
    grp_spec = pl.BlockSpec((per, nb, lc * n_c), lambda i, vb: (vb, i, 0))
    kern = _to_groups_kernel if to_groups else _from_groups_kernel
    out_shape = (n_g, n_tok // lc, lc * n_c) if to_groups else (n_tok, width)
    return pl.pallas_call(
        functools.partial(kern, lc=lc, n_c=n_c),
        grid=(n_tok // tile, n_g // per),
        in_specs=[tok_spec if to_groups else grp_spec],
        out_specs=grp_spec if to_groups else tok_spec,
        out_shape=jax.ShapeDtypeStruct(out_shape, x.dtype),
        compiler_params=_cparams("arbitrary", "arbitrary"),
        name="to_groups" if to_groups else "from_groups",
    )(x)


def s5_core(u, log_dt, a_re, a_im, b_re, b_im, c_re, c_im):
    bsz, seq, width = u.shape
    n_g, n_p = a_re.shape
    n_c = width // n_g
    lc = min(S5_CHUNK, seq)
    n_chunks = seq // lc
    toep, pm, q, lvl = s5_operators(log_dt, a_re, a_im, b_re, b_im, c_re, c_im, n_chunks, lc)
    tile = min(2048, bsz * seq)
    ug = _group_relayout(u.reshape(bsz * seq, width), n_g, lc, True, tile)
    rows, lw = bsz * n_chunks, lc * n_c
    yg = pl.pallas_call(
        functools.partial(_s5_kernel, n_chunks=n_chunks),
        grid=(n_g,),
        in_specs=[pl.BlockSpec((1, rows, lw), lambda g: (g, 0, 0)),
                  pl.BlockSpec((1, lw, lw), lambda g: (g, 0, 0)),
                  pl.BlockSpec((1, lw, 2 * n_p), lambda g: (g, 0, 0)),
                  pl.BlockSpec((1, 2 * n_p, lw), lambda g: (g, 0, 0)),
                  pl.BlockSpec((1,) + lvl.shape[1:], lambda g: (g, 0, 0, 0))],
        out_specs=pl.BlockSpec((1, rows, lw), lambda g: (g, 0, 0)),
        out_shape=jax.ShapeDtypeStruct((n_g, rows, lw), jnp.float32),
        compiler_params=_cparams("arbitrary"),
        name="s5",
    )(ug, toep, pm, q, lvl)
    return _group_relayout(yg, n_g, lc, False, tile).reshape(bsz, seq, width)


def _gelu_tanh(y):
    return 0.5 * y * (1.0 + jnp.tanh(math.sqrt(2.0 / math.pi) * (y + 0.044715 * (y * y * y))))


def _outproj_kernel(yr_ref, ys_ref, u_ref, x_ref, mods_ref, d_ref, wglu_ref, bglu_ref, wout_ref,
                    gpost_ref, gpre_ref, wrt_ref, x1_ref, h2_ref, lg_ref):
    m = mods_ref[0]
    yr = yr_ref[...]
    y5 = _gelu_tanh(ys_ref[...] + d_ref[...] * u_ref[...])
    y5 = y5 * _sigmoid(_dot(_bf(y5), wglu_ref[...]) + bglu_ref[...])
    wr = yr.shape[-1]
    mixed = _dot(_bf(yr), wout_ref[0:wr, :]) + _dot(_bf(y5), wout_ref[wr:, :])
    x1 = x_ref[...] + m[2:3, :] * _rms(mixed, gpost_ref[...])
    x1_ref[...] = x1
    h2 = _rms(x1, gpre_ref[...]) * (1.0 + m[4:5, :]) + m[3:4, :]
    half = h2.shape[-1] // 2
    _store_row_tiles(h2_ref, _pack_bf16_pair(h2[:, :half], h2[:, half:]))
    w_hi, w_lo = wrt_ref[0], wrt_ref[1]
    h_hi = _bf(h2)
    h_lo = _bf(h2 - h_hi.astype(jnp.float32))
    lg_ref[...] = _dot_nt(w_hi, h_hi) + (_dot_nt(w_hi, h_lo) + _dot_nt(w_lo, h_hi))


def out_proj(y_rwkv, y_s5, u, x, mods3, s5_d, w_glu_bf, b_glu, w_out_bf, g_post, g_pre, w_router_t,
             seq, tm):
    n_tok, dm = x.shape
    wr, ws = y_rwkv.shape[-1], y_s5.shape[-1]
    n_e = w_router_t.shape[0]
    tpb = seq // tm
    tok = lambda w: pl.BlockSpec((tm, w), lambda i: (i, 0))
    full = lambda a: pl.BlockSpec(a.shape, lambda i: (0,) * a.ndim)
    row = lambda t: t.reshape(1, -1)
    wr_hi = _bf(w_router_t)
    wr_split = jnp.stack([wr_hi, _bf(w_router_t - wr_hi.astype(jnp.float32))])
    consts = [row(s5_d), w_glu_bf, row(b_glu), w_out_bf, row(g_post), row(g_pre), wr_split]
    return pl.pallas_call(
        _outproj_kernel,
        grid=(n_tok // tm,),
        in_specs=[tok(wr), tok(ws), tok(ws), tok(dm),
                  pl.BlockSpec((1, N_MODS, dm), lambda i: (i // tpb, 0, 0))] + [full(a) for a in consts],
        out_specs=[tok(dm), _row_tile_spec(tm, dm // 2, lambda i: i),
                   pl.BlockSpec((n_e, tm), lambda i: (0, i))],
        out_shape=[jax.ShapeDtypeStruct((n_tok, dm), jnp.float32),
                   jax.ShapeDtypeStruct((n_tok * (dm // 2 // LANES), LANES), jnp.uint32),
                   jax.ShapeDtypeStruct((n_e, n_tok), jnp.float32)],
        compiler_params=_cparams("arbitrary"),
        name="outproj",
    )(y_rwkv, y_s5, u, x, mods3, *consts)


def _route_kernel(lg_ref, bias_ref, tri_ref, e_ref, w_ref, r_ref, cnt_ref, carry_ref):
    i = pl.program_id(0)

    @pl.when(i == 0)
    def _():
        carry_ref[...] = jnp.zeros_like(carry_ref)

    neg = -jnp.inf
    scores = _sigmoid(lg_ref[...])
    n_e, tm = scores.shape
    choice = scores + bias_ref[...]
    gsz = n_e // ROUTE_GROUPS
    c3 = choice.reshape(ROUTE_GROUPS, gsz, tm)
    io = lax.broadcasted_iota(jnp.int32, c3.shape, 1)
    m1 = jnp.max(c3, axis=1, keepdims=True)
    first = jnp.min(jnp.where(c3 == m1, io, gsz), axis=1, keepdims=True)
    m2 = jnp.max(jnp.where(io == first, neg, c3), axis=1, keepdims=True)
    gs = m1 + m2
    gi = lax.broadcasted_iota(jnp.int32, gs.shape, 0)
    rank = jnp.zeros(gs.shape, jnp.int32)
    for j in range(ROUTE_GROUPS):
        gj = gs[j:j + 1]
        beats = (gj > gs) | ((gj == gs) & (gi > j))
        rank = rank + beats.astype(jnp.int32)
    masked = jnp.where(rank < ROUTE_TOPK_GROUPS, c3, neg).reshape(n_e, tm)

    eio = lax.broadcasted_iota(jnp.int32, (n_e, tm), 0)
    ids, ws = [], []
    mhot = jnp.zeros((n_e, tm), jnp.float32)
    for _ in range(TOP_K):
        m = jnp.max(masked, axis=0, keepdims=True)
        idx = jnp.min(jnp.where(masked == m, eio, n_e), axis=0, keepdims=True)
        sel = eio == idx
        ws.append(jnp.sum(jnp.where(sel, scores, 0.0), axis=0, keepdims=True))
        ids.append(idx)
        masked = jnp.where(sel, neg, masked)
        mhot = jnp.where(sel, 1.0, mhot)
    wsum = ws[0]
    for t in ws[1:]:
        wsum = wsum + t
    before = _dot(_bf(mhot), tri_ref[...]) + carry_ref[...]
    ranks = [jnp.sum(jnp.where(eio == idx, before, 0.0), axis=0, keepdims=True) for idx in ids]
    e_ref[...] = jnp.concatenate(ids, axis=0)
    w_ref[...] = jnp.concatenate(ws, axis=0) / wsum * ROUTE_SCALE
    r_ref[...] = jnp.concatenate(ranks, axis=0).astype(jnp.int32)
    carry_ref[...] = carry_ref[...] + jnp.sum(mhot, axis=1, keepdims=True)
    cnt_ref[...] = carry_ref[...]


def route(logits_t, router_bias, tm):
    n_e, n_tok = logits_t.shape
    tri = (jnp.arange(tm)[:, None] < jnp.arange(tm)[None, :]).astype(jnp.bfloat16)
    kt = lambda: pl.BlockSpec((TOP_K, tm), lambda i: (0, i))
    return pl.pallas_call(
        _route_kernel,
        grid=(n_tok // tm,),
        in_specs=[pl.BlockSpec((n_e, tm), lambda i: (0, i)),
                  pl.BlockSpec((n_e, 1), lambda i: (0, 0)),
                  pl.BlockSpec((tm, tm), lambda i: (0, 0))],
        out_specs=[kt(), kt(), kt(), pl.BlockSpec((n_e, 1), lambda i: (0, 0))],
        out_shape=[jax.ShapeDtypeStruct((TOP_K, n_tok), jnp.int32),
                   jax.ShapeDtypeStruct((TOP_K, n_tok), jnp.float32),
                   jax.ShapeDtypeStruct((TOP_K, n_tok), jnp.int32),
                   jax.ShapeDtypeStruct((n_e, 1), jnp.float32)],
        scratch_shapes=[pltpu.VMEM((n_e, 1), jnp.float32)],
        compiler_params=_cparams("arbitrary"),
        name="route",
    )(logits_t, router_bias.reshape(n_e, 1), tri)


def _dest_kernel(e_ref, r_ref, ps_ref, d_ref):
    n_e = ps_ref.shape[0]
    tm = e_ref.shape[1]
    eio = lax.broadcasted_iota(jnp.int32, (n_e, tm), 0)
    ps = ps_ref[...]
    rows = [jnp.sum(jnp.where(eio == e_ref[k:k + 1, :], ps, 0.0), axis=0, keepdims=True)
            for k in range(TOP_K)]
    d_ref[...] = jnp.concatenate(rows, axis=0).astype(jnp.int32) + r_ref[...]


def dest_rows(e_idx, rank, pad_start, tm):
    n_tok = e_idx.shape[1]
    n_e = pad_start.shape[0]
    kt = lambda: pl.BlockSpec((TOP_K, tm), lambda i: (0, i))
    return pl.pallas_call(
        _dest_kernel,
        grid=(n_tok // tm,),
        in_specs=[kt(), kt(), pl.BlockSpec((n_e, 1), lambda i: (0, 0))],
        out_specs=kt(),
        out_shape=jax.ShapeDtypeStruct((TOP_K, n_tok), jnp.int32),
        compiler_params=_cparams("arbitrary"),
        name="dest",
    )(e_idx, rank, pad_start.astype(jnp.float32).reshape(n_e, 1))


def _dispatch_kernel(fill_start_ref, fill_len_ref, nu_ref, dest_ref, h_ref, xs_ref, zeros, sem, zsem,
                     *, n_s, bm, experts_per_step, tiles_per_step, n_tiles):
    i = pl.program_id(0)
    td = dest_ref.shape[1]

    @pl.when(i == 0)
    def _():
        zeros[...] = jnp.zeros_like(zeros)

    def body(t, carry):
        for k in range(TOP_K):
            pltpu.make_async_copy(_row(h_ref, t, n_s), _row(xs_ref, dest_ref[k, t], n_s),
                                  sem).start(priority=k % 2)
        return carry

    lax.fori_loop(0, td, body, 0, unroll=2)

    fills = []
    for j in range(experts_per_step):
        e = i * experts_per_step + j
        start, length = fill_start_ref[e], fill_len_ref[e]
        piece = bm // 2
        while piece >= 1:
            off = start + (length & ~(2 * piece - 1))
            fills.append(((length & piece) != 0,
                          pltpu.make_async_copy(zeros.at[pl.ds(0, piece * n_s), :],
                                                xs_ref.at[pl.ds(off * n_s, piece * n_s), :], zsem)))
            piece //= 2
    for j in range(tiles_per_step):
        tile = i * tiles_per_step + j
        fills.append(((tile >= nu_ref[0]) & (tile < n_tiles),
                      pltpu.make_async_copy(zeros, xs_ref.at[pl.ds(tile * (bm * n_s), bm * n_s), :], zsem)))
    for cond, cp in fills:
        pl.when(cond)(cp.start)

    for _ in range(TOP_K):
        pltpu.make_async_copy(h_ref, xs_ref.at[pl.ds(0, td * n_s), :], sem).wait()
    for cond, cp in fills:
        pl.when(cond)(cp.wait)


def dispatch(dest, h2p, fill_start, fill_len, n_used, n_tok, n_tiles, td, bm):
    n_s = h2p.shape[0] // n_tok
    n_steps = n_tok // td
    n_e = fill_start.shape[0]
    experts_per_step = -(-n_e // n_steps)
    tiles_per_step = -(-n_tiles // n_steps)
    pad = n_steps * experts_per_step - n_e
    fill_start = jnp.pad(fill_start, (0, pad))
    fill_len = jnp.pad(fill_len, (0, pad))
    return pl.pallas_call(
        functools.partial(_dispatch_kernel, n_s=n_s, bm=bm, experts_per_step=experts_per_step,
                          tiles_per_step=tiles_per_step, n_tiles=n_tiles),
        grid_spec=pltpu.PrefetchScalarGridSpec(
            num_scalar_prefetch=3,
            grid=(n_steps,),
            in_specs=[pl.BlockSpec((TOP_K, td), lambda i, *_: (0, i), memory_space=pltpu.SMEM),
                      _row_tile_spec(td, n_s * LANES, lambda i, *_: i)],
            out_specs=pl.BlockSpec(memory_space=pl.ANY),
            scratch_shapes=[pltpu.VMEM((bm * n_s, LANES), h2p.dtype),
                            pltpu.SemaphoreType.DMA(()), pltpu.SemaphoreType.DMA(())]),
        out_shape=jax.ShapeDtypeStruct((n_tiles * bm * n_s, LANES), h2p.dtype),
        compiler_params=_cparams("arbitrary"),
        name="dispatch",
    )(fill_start, fill_len, n_used, dest, h2p)


def _expert_kernel(te_ref, nu_ref, nt_ref, x_ref, wgu_hbm, wd_hbm, o_ref, wgu_st, wd_st, grp, sems,
                   *, bm):
    i = pl.program_id(0)
    ff = wd_st.shape[1]
    n_used = nu_ref[0]

    def fetch(e, slot):
        return (pltpu.make_async_copy(wgu_hbm.at[e], wgu_st.at[slot], sems.at[slot]),
                pltpu.make_async_copy(wd_hbm.at[e], wd_st.at[slot], sems.at[slot]))

    @pl.when(i == 0)
    def _():
        grp[0] = 0
        for cp in fetch(te_ref[0], 0):
            cp.start()

    @pl.when(i < n_used)
    def _():
        e = te_ref[i]

        @pl.when((i == 0) | (e != te_ref[jnp.maximum(i - 1, 0)]))
        def _():
            slot = grp[0] & 1
            grp[0] = grp[0] + 1
            grp[1] = slot
            for cp in fetch(e, slot):
                cp.wait()
            nxt = i + nt_ref[e]

            @pl.when(nxt < n_used)
            def _():
                for cp in fetch(te_ref[jnp.minimum(nxt, n_used - 1)], 1 - slot):
                    cp.start()

        slot = grp[1]
        hi, lo = _unpack_bf16_pair(_load_row_tiles(x_ref, bm))
        x = jnp.concatenate([hi, lo], axis=1)
        gu = _dot(x, wgu_st[slot])
        hid = _silu(gu[:, :ff]) * gu[:, ff:]
        y = _dot(_bf(hid), wd_st[slot])
        half = y.shape[1] // 2
        _store_row_tiles(o_ref, _pack_bf16_pair(y[:, :half], y[:, half:]))

    @pl.when(i >= nu_ref[0])
    def _():
        o_ref[...] = jnp.zeros_like(o_ref)


def expert_ffn(tile_e, n_used, tiles_per_expert, xs, n_rows, w_gate_up_bf, w_down_bf, bm):
    ff, dm = w_down_bf.shape[1:]
    half = dm // 2
    rows_in = _row_tile_spec(bm, half, lambda i, te, nu, nt: jnp.minimum(i, nu[0] - 1))
    rows_out = _row_tile_spec(bm, half, lambda i, te, nu, nt: i)
    hbm = pl.BlockSpec(memory_space=pl.ANY)
    return pl.pallas_call(
        functools.partial(_expert_kernel, bm=bm),
        grid_spec=pltpu.PrefetchScalarGridSpec(
            num_scalar_prefetch=3,
            grid=(n_rows // bm,),
            in_specs=[rows_in, hbm, hbm],
            out_specs=rows_out,
            scratch_shapes=[pltpu.VMEM((2, dm, 2 * ff), jnp.bfloat16),
                            pltpu.VMEM((2, ff, dm), jnp.bfloat16),
                            pltpu.SMEM((2,), jnp.int32),
                            pltpu.SemaphoreType.DMA((2,))]),
        out_shape=jax.ShapeDtypeStruct((n_rows * (half // LANES), LANES), jnp.uint32),
        compiler_params=_cparams("arbitrary"),
        name="expert",
    )(tile_e, n_used, tiles_per_expert, xs, w_gate_up_bf, w_down_bf)


def _combine_kernel(dest_ref, dnext_ref, w_ref, x1_ref, h2_ref, mods_ref, g_ref, wsg_ref, wsu_ref,
                    wsd_ref, ys_ref, o_ref, gbuf, sems, *, n_s):
    i = pl.program_id(0)
    n_steps = pl.num_programs(0)
    tc = x1_ref.shape[0]

    def issue(d_ref, slot):
        def body(t, carry):
            for k in range(TOP_K):
                pltpu.make_async_copy(_row(ys_ref, d_ref[k, t], n_s), _row(gbuf.at[slot, k], t, n_s),
                                      sems.at[slot]).start(priority=k % 2)
            return carry
        lax.fori_loop(0, tc, body, 0, unroll=2)

    def finish(slot):
        for k in range(TOP_K):
            pltpu.make_async_copy(ys_ref.at[pl.ds(0, tc * n_s), :], gbuf.at[slot, k],
                                  sems.at[slot]).wait()
        hi, lo = _unpack_bf16_pair(_load_row_tiles(h2_ref, tc))
        h = jnp.concatenate([hi, lo], axis=1)
        hid = _silu(_dot(h, wsg_ref[...])) * _dot(h, wsu_ref[...])
        y = _dot(_bf(hid), wsd_ref[...])
        w = w_ref[...]
        acc_hi = acc_lo = None
        for k in range(TOP_K):
            g_hi, g_lo = _unpack_pair_f32(_load_row_tiles(gbuf.at[slot, k], tc))
            wk = w[:, k:k + 1]
            acc_hi = wk * g_hi if acc_hi is None else acc_hi + wk * g_hi
            acc_lo = wk * g_lo if acc_lo is None else acc_lo + wk * g_lo
        y = y + jnp.concatenate([acc_hi, acc_lo], axis=1)
        m = mods_ref[0]
        o_ref[...] = x1_ref[...] + m[5:6, :] * _rms(y, g_ref[...])

    @pl.when(i == 0)
    def _():
        issue(dest_ref, 0)

    for slot in range(2):
        @pl.when(i % 2 == slot)
        def _(slot=slot):
            @pl.when(i + 1 < n_steps)
            def _():
                issue(dnext_ref, 1 - slot)
            finish(slot)


def combine(dest, w_tk, x1, h2p, mods3, g_post, ws_gate_bf, ws_up_bf, ws_down_bf, ys, seq, tc):
    n_tok, dm = x1.shape
    n_s = dm // 2 // LANES
    n_steps = n_tok // tc
    tpb = seq // tc
    full = lambda a: pl.BlockSpec(a.shape, lambda i: (0,) * a.ndim)
    tok = lambda w: pl.BlockSpec((tc, w), lambda i: (i, 0))
    g2 = g_post.reshape(1, dm)
    return pl.pallas_call(
        functools.partial(_combine_kernel, n_s=n_s),
        grid=(n_steps,),
        in_specs=[pl.BlockSpec((TOP_K, tc), lambda i: (0, i), memory_space=pltpu.SMEM),
                  pl.BlockSpec((TOP_K, tc), lambda i: (0, jnp.minimum(i + 1, n_steps - 1)),
                               memory_space=pltpu.SMEM),
                  tok(TOP_K),
                  tok(dm), _row_tile_spec(tc, dm // 2, lambda i: i),
                  pl.BlockSpec((1, N_MODS, dm), lambda i: (i // tpb, 0, 0)),
                  full(g2), full(ws_gate_bf), full(ws_up_bf), full(ws_down_bf),
                  pl.BlockSpec(memory_space=pl.ANY)],
        out_specs=tok(dm),
        out_shape=jax.ShapeDtypeStruct((n_tok, dm), jnp.float32),
        scratch_shapes=[pltpu.VMEM((2, TOP_K, tc * n_s, LANES), jnp.uint32),
                        pltpu.SemaphoreType.DMA((2,))],
        compiler_params=_cparams("arbitrary"),
        name="combine",
    )(dest, dest, w_tk, x1, h2p, mods3, g2, ws_gate_bf, ws_up_bf, ws_down_bf, ys)


def moe_ffn(h2, x1, logits_t, mods3, router_bias, w_gate_up_bf, w_down_bf, ws_gate, ws_up, ws_down,
            g_post, seq):
    n_tok, dm = x1.shape
    n_e = w_down_bf.shape[0]
    bm = EXPERT_ROW_TILE
    e_idx, w_kt, rank, cnt = route(logits_t, router_bias, min(512, n_tok))
    counts = cnt[:, 0].astype(jnp.int32)
    padded = (counts + bm - 1) // bm * bm
    pad_end = jnp.cumsum(padded)
    pad_start = (pad_end - padded).astype(jnp.int32)
    n_tiles = -(-(n_tok * TOP_K) // bm) + n_e
    n_used = (pad_end[-1] // bm).astype(jnp.int32)
    tile_start = jnp.arange(n_tiles, dtype=jnp.int32) * bm
    tile_e = jnp.minimum(jnp.sum((pad_end[None, :] <= tile_start[:, None]).astype(jnp.int32), axis=1),
                         n_e - 1).astype(jnp.int32)
    tile_e = jnp.where(jnp.arange(n_tiles) < n_used, tile_e, tile_e[n_used - 1])
    dest = dest_rows(e_idx, rank, pad_start, min(512, n_tok))
    n_used1 = n_used.reshape(1)
    xs = dispatch(dest, h2, pad_start + counts, padded - counts, n_used1, n_tok, n_tiles,
                  min(MOE_TOKEN_TILE, n_tok), bm)
    ys = expert_ffn(tile_e, n_used1, (padded // bm).astype(jnp.int32), xs, n_tiles * bm,
                    w_gate_up_bf, w_down_bf, bm)
    bf = lambda a: a.astype(jnp.bfloat16)
    return combine(dest, w_kt.T, x1, h2, mods3, g_post, bf(ws_gate), bf(ws_up), bf(ws_down), ys,
                   seq, min(MOE_TOKEN_TILE, n_tok))


def kernel(x, c, w_ada, b_ada, g_pre_mix, g_post_mix, g_pre_ffn, g_post_ffn, w_in, mu_shift, rwkv_w0, rwkv_w2, rwkv_a0, rwkv_a2, rwkv_g2, rwkv_k_k, rwkv_k_a, rwkv_r_k, rwkv_ln_w, rwkv_ln_b, s5_log_dt, s5_a_re, s5_a_im, s5_b_re, s5_b_im, s5_c_re, s5_c_im, s5_d, s5_w_glu, s5_b_glu, w_out, w_router, router_bias, w_gate, w_up, w_down, ws_gate, ws_up, ws_down):
    bsz, seq, dm = x.shape
    depth = w_ada.shape[0]
    bf = lambda a: a.astype(jnp.bfloat16)
    tm = min(512, seq)
    for l in range(depth):
        mods3 = ada_mods(c, w_ada[l], b_ada[l]).reshape(bsz, N_MODS, dm)
        n_rwkv = mu_shift.shape[-1]
        p, u = in_proj(x, mods3, g_pre_mix[l], bf(w_in[l]), n_rwkv, tm)
        y_rwkv, w_gate_up_bf, w_down_bf = rwkv_mixer(
            p, mu_shift[l], rwkv_w0[l], rwkv_w2[l], rwkv_a0[l], rwkv_a2[l], rwkv_g2[l], rwkv_k_k[l],
            rwkv_k_a[l], rwkv_r_k[l].reshape(-1), rwkv_ln_w[l], rwkv_ln_b[l],
            w_gate[l], w_up[l], w_down[l])
        y_s5 = s5_core(u, s5_log_dt[l], s5_a_re[l], s5_a_im[l], s5_b_re[l], s5_b_im[l],
                       s5_c_re[l], s5_c_im[l])
        flat = lambda a: a.reshape(bsz * seq, a.shape[-1])
        x1, h2, logits_t = out_proj(flat(y_rwkv), flat(y_s5), flat(u), flat(x), mods3, s5_d[l],
                                    bf(s5_w_glu[l]), s5_b_glu[l], bf(w_out[l]), g_post_mix[l],
                                    g_pre_ffn[l], w_router[l].T, seq, tm)
        out = moe_ffn(h2, x1, logits_t, mods3, router_bias[l], w_gate_up_bf, w_down_bf,
                      ws_gate[l], ws_up[l], ws_down[l], g_post_ffn[l], seq)
        x = out.reshape(bsz, seq, dm)
    return x
```

```python
import functools
import math

import jax
import jax.numpy as jnp
from jax import lax
from jax.experimental import pallas as pl
from jax.experimental.pallas import tpu as pltpu

NORM_EPS = 1e-6
LNX_EPS = 64e-5
L2_EPS = 1e-12
S5_MAX_REAL = -1e-4
ROUTE_SCALE = 2.5
N_MODS = 6

RWKV_HEAD_DIM = 64
DECAY_LORA = 64
ICLR_LORA = 64
GATE_LORA = 128
TOP_K = 8
ROUTE_GROUPS = 8
ROUTE_TOPK_GROUPS = 4

RWKV_CHUNK = 64
S5_CHUNK = 16
EXPERT_ROW_TILE = 512
MOE_TOKEN_TILE = 512

VMEM_LIMIT = 56 * 1024 * 1024

HI = lax.Precision.HIGHEST


def _cparams(*sem):
    return pltpu.CompilerParams(dimension_semantics=sem, vmem_limit_bytes=VMEM_LIMIT)


def _dot(a, b, precision=None):
    return jnp.dot(a, b, preferred_element_type=jnp.float32, precision=precision)


def _dot_nt(a, b, precision=None):
    return lax.dot_general(a, b, (((1,), (1,)), ((), ())),
                           preferred_element_type=jnp.float32, precision=precision)


def _bdot(a, b):
    return lax.dot_general(a, b, (((2,), (1,)), ((0,), (0,))), preferred_element_type=jnp.float32)


def _bdot_nt(a, b):
    return lax.dot_general(a, b, (((2,), (2,)), ((0,), (0,))), preferred_element_type=jnp.float32)


def _bf(x):
    return x.astype(jnp.bfloat16)


def _pack_bf16_pair(hi, lo):
    hb = lax.bitcast_convert_type(_bf(hi).astype(jnp.float32), jnp.uint32)
    lb = lax.bitcast_convert_type(_bf(lo).astype(jnp.float32), jnp.uint32)
    return (hb & jnp.uint32(0xFFFF0000)) | (lb >> 16)


def _unpack_pair_f32(w):
    hi = lax.bitcast_convert_type(w & jnp.uint32(0xFFFF0000), jnp.float32)
    lo = lax.bitcast_convert_type(w << 16, jnp.float32)
    return hi, lo


def _unpack_bf16_pair(w):
    hi, lo = _unpack_pair_f32(w)
    return _bf(hi), _bf(lo)


LANES = 128


def _row_tile_spec(rows, width, row_block):
    return pl.BlockSpec((rows * (width // LANES), LANES), lambda *a: (row_block(*a), 0))


def _store_row_tiles(ref, x):
    rows = x.shape[0]
    n_s = ref.shape[0] // rows
    for s in range(n_s):
        ref[pl.ds(s, rows, stride=n_s), :] = x[:, s * LANES:(s + 1) * LANES]


def _load_row_tiles(ref, rows):
    n_s = ref.shape[0] // rows
    return jnp.concatenate([ref[pl.ds(s, rows, stride=n_s), :] for s in range(n_s)], axis=1)


def _row(ref, r, n_s):
    return ref.at[pl.ds(pl.multiple_of(r * n_s, n_s), n_s), :]


def _sigmoid(x):
    return 1.0 / (1.0 + jnp.exp(-x))


def _silu(x):
    return x * _sigmoid(x)


def _rms(x, gain):
    return x * lax.rsqrt(jnp.mean(x * x, axis=-1, keepdims=True) + NORM_EPS) * gain


def _ada_kernel(c_ref, w_ref, b_ref, o_ref):
    c = c_ref[...]
    o_ref[...] = _dot(_silu(c), w_ref[...], HI) + b_ref[...]


def ada_mods(c, w_ada, b_ada):
    bsz, dm = c.shape
    n = w_ada.shape[1]
    tn = dm
    return pl.pallas_call(
        _ada_kernel,
        grid=(n // tn,),
        in_specs=[pl.BlockSpec((bsz, dm), lambda j: (0, 0)),
                  pl.BlockSpec((dm, tn), lambda j: (0, j)),
                  pl.BlockSpec((1, tn), lambda j: (0, j))],
        out_specs=pl.BlockSpec((bsz, tn), lambda j: (0, j)),
        out_shape=jax.ShapeDtypeStruct((bsz, n), jnp.float32),
        compiler_params=_cparams("arbitrary"),
        name="ada",
    )(c, w_ada, b_ada.reshape(1, n))


def _inproj_kernel(x_ref, mods_ref, g_ref, w_ref, p_ref, u_ref):
    x = x_ref[0]
    m = mods_ref[0]
    h = _rms(x, g_ref[...]) * (1.0 + m[1:2, :]) + m[0:1, :]
    proj = _dot(_bf(h), w_ref[...])
    n_p = p_ref.shape[-1]
    p_ref[0] = proj[:, :n_p]
    u_ref[0] = proj[:, n_p:]


def in_proj(x, mods3, g_pre, w_in_bf, n_rwkv, tm):
    bsz, seq, dm = x.shape
    n = w_in_bf.shape[1]
    n_s5 = n - n_rwkv
    return pl.pallas_call(
        _inproj_kernel,
        grid=(bsz, seq // tm),
        in_specs=[pl.BlockSpec((1, tm, dm), lambda b, i: (b, i, 0)),
                  pl.BlockSpec((1, N_MODS, dm), lambda b, i: (b, 0, 0)),
                  pl.BlockSpec((1, dm), lambda b, i: (0, 0)),
                  pl.BlockSpec((dm, n), lambda b, i: (0, 0))],
        out_specs=[pl.BlockSpec((1, tm, n_rwkv), lambda b, i: (b, i, 0)),
                   pl.BlockSpec((1, tm, n_s5), lambda b, i: (b, i, 0))],
        out_shape=[jax.ShapeDtypeStruct((bsz, seq, n_rwkv), jnp.float32),
                   jax.ShapeDtypeStruct((bsz, seq, n_s5), jnp.float32)],
        compiler_params=_cparams("arbitrary", "arbitrary"),
        name="inproj",
    )(x, mods3, g_pre.reshape(1, dm), w_in_bf)


RWKV_TILE = 256


def _split_dot(x, m01, terms):
    blk = m01.shape[0]
    pieces = []
    rem = x
    for _ in range(terms):
        piece = _bf(rem)
        pieces.append(piece)
        rem = rem - piece.astype(jnp.float32)
    cols = []
    for c in range(x.shape[1] // blk):
        acc = None
        for piece in pieces:
            part = _dot(piece[:, c * blk:(c + 1) * blk], m01)
            acc = part if acc is None else acc + part
        cols.append(acc)
    return cols[0] if len(cols) == 1 else jnp.concatenate(cols, axis=1)


def _rwkv_kernel(p_ref, mu_ref, w0_ref, w2_ref, a0_ref, a2_ref, g2_ref, kk_ref, ka_ref,
                  rk_ref, lnw_ref, lnb_ref, bd_ref, o_ref,
                  s_ref, carry_ref, rt_ref, at_ref, bt_ref, kt_ref, v_ref, wl_ref, y_ref,
                  *, width, chunk):
    hd = RWKV_HEAD_DIM
    pw = 2 * hd
    pairs = width // pw
    i = pl.program_id(1)

    @pl.when(i == 0)
    def _():
        s_ref[...] = jnp.zeros_like(s_ref)
        carry_ref[...] = jnp.zeros_like(carry_ref)

    p = p_ref[0]
    n_t = p.shape[0]
    n_chunks = n_t // chunk
    row = lax.broadcasted_iota(jnp.int32, p.shape, 0)
    prev = jnp.where(row == 0, carry_ref[0:1, :], pltpu.roll(p, 1, axis=0))
    carry_ref[0:1, :] = p[n_t - 1:n_t, :]
    pm = p + (prev - p) * mu_ref[...]

    r = pm[:, 0:width]
    k = pm[:, width:2 * width]
    v = pm[:, 2 * width:3 * width]
    c0 = 3 * width
    w_lo = pm[:, c0:c0 + DECAY_LORA]
    a_lo = pm[:, c0 + DECAY_LORA:c0 + DECAY_LORA + ICLR_LORA]
    g_lo = pm[:, c0 + DECAY_LORA + ICLR_LORA:]

    z = w0_ref[...] + _dot(_bf(jnp.tanh(w_lo)), w2_ref[...])
    softplus_neg = jnp.maximum(-z, 0.0) + jnp.log(1.0 + jnp.exp(-jnp.abs(z)))
    logd = -jnp.exp(-softplus_neg - 0.5)
    iclr = _sigmoid(a0_ref[...] + _dot(_bf(a_lo), a2_ref[...]))
    gate = _dot(_bf(_sigmoid(g_lo)), g2_ref[...])

    bd = bd_ref[...]
    kk = k * kk_ref[...]
    kk = kk / jnp.maximum(jnp.sqrt(_split_dot(kk * kk, bd, 2)), L2_EPS)
    k2 = k * (1.0 + (iclr - 1.0) * ka_ref[...])
    bonus = _split_dot(r * k2 * rk_ref[...], bd, 2) * v

    ti = lax.broadcasted_iota(jnp.int32, (n_t, n_t), 0)
    si = lax.broadcasted_iota(jnp.int32, (n_t, n_t), 1)
    tri = jnp.where((ti >= si) & (ti // chunk == si // chunk), 1.0, 0.0).astype(jnp.bfloat16)
    cum = _split_dot_lhs(tri, logd, 3)
    e_pos = jnp.exp(cum)
    e_neg = jnp.exp(-cum)
    rt_ref[...] = _bf(r * e_pos)
    at_ref[...] = _bf(-kk * jnp.exp(cum - logd))
    bt_ref[...] = _bf(kk * iclr * e_neg)
    kt_ref[...] = _bf(k2 * e_neg)
    v_ref[...] = _bf(v)
    for c in range(n_chunks):
        wl_ref[c:c + 1, :] = e_pos[(c + 1) * chunk - 1:(c + 1) * chunk, :]

    two_l = 2 * chunk
    lane = lax.broadcasted_iota(jnp.int32, (chunk, pw), 1)
    lane0 = lane < hd
    bi = lax.broadcasted_iota(jnp.int32, (two_l, two_l), 0)
    bj = lax.broadcasted_iota(jnp.int32, (two_l, two_l), 1)
    same = (bi // chunk) == (bj // chunk)
    low_strict = same & (bi > bj)
    low_incl = same & (bi >= bj)
    eye_t = jnp.where(bi == bj, 1.0, 0.0)
    pi_ = lax.broadcasted_iota(jnp.int32, (pw, pw), 0)
    pj_ = lax.broadcasted_iota(jnp.int32, (pw, pw), 1)
    eye_p = jnp.where(pi_ == pj_, 1.0, 0.0)
    n_sq = max(1, int(math.ceil(math.log2(chunk))) - 1)
    zero = jnp.zeros((), jnp.bfloat16)

    def stack(x):
        return jnp.concatenate([jnp.where(lane0, x, zero), jnp.where(lane0, zero, x)], axis=0)

    def gather(ref):
        return jnp.stack([stack(ref[c * chunk:(c + 1) * chunk, hp * pw:(hp + 1) * pw])
                          for c in range(n_chunks) for hp in range(pairs)], axis=0)

    a_s, b_s, k_s, r_s, v_s = (gather(ref) for ref in (at_ref, bt_ref, kt_ref, rt_ref, v_ref))
    wl = jnp.stack([wl_ref[c:c + 1, hp * pw:(hp + 1) * pw]
                    for c in range(n_chunks) for hp in range(pairs)], axis=0)
    gram = _bdot_nt(jnp.concatenate([a_s, r_s], axis=1), jnp.concatenate([b_s, k_s], axis=1))
    m_ab = jnp.where(low_strict, gram[:, :two_l, :two_l], 0.0)
    m_ak = jnp.where(low_strict, gram[:, :two_l, two_l:], 0.0)
    n_rb = jnp.where(low_incl, gram[:, two_l:, :two_l], 0.0)
    n_rk = jnp.where(low_incl, gram[:, two_l:, two_l:], 0.0)
    t_inv = eye_t + m_ab
    m_pow = _bf(m_ab)
    for _ in range(n_sq):
        m_pow = _bf(_bdot(m_pow, m_pow))
        t_inv = t_inv + _bdot(_bf(t_inv), m_pow)
    makv = _bdot(_bf(m_ak), v_s)
    tx_bf = _bf(_bdot(_bf(t_inv), jnp.concatenate([a_s, _bf(makv)], axis=2)))
    nx = _bdot(_bf(n_rb), tx_bf)
    rbar = _bf(r_s.astype(jnp.float32) + nx[:, :, :pw])
    y0 = nx[:, :, pw:] + _bdot(_bf(n_rk), v_s)
    tb = _bdot(jnp.swapaxes(tx_bf, 1, 2), b_s)
    pmat = _bf((eye_p + tb[:, :pw, :]) * wl)
    dmat = (tb[:, pw:, :] + _bdot(jnp.swapaxes(v_s, 1, 2), k_s)) * wl

    s = s_ref[...]
    for c in range(n_chunks):
        sel = slice(c * pairs, (c + 1) * pairs)
        s_bf = _bf(s)
        ys = _bdot_nt(rbar[sel], s_bf) + y0[sel]
        yc_ = ys[:, :chunk, :] + ys[:, chunk:, :]
        for hp in range(pairs):
            y_ref[c * chunk:(c + 1) * chunk, hp * pw:(hp + 1) * pw] = yc_[hp]
        s = _bdot(s_bf, pmat[sel]) + dmat[sel]
    s_ref[...] = s

    y = y_ref[...]
    inv_hd = 1.0 / hd
    mean = _split_dot(y, bd, 2) * inv_hd
    yc = y - mean
    var = _split_dot(yc * yc, bd, 2) * inv_hd
    yn = yc * lax.rsqrt(var + LNX_EPS) * lnw_ref[...] + lnb_ref[...]
    o_ref[0] = (yn + bonus) * gate


def _split_dot_lhs(m01, x, terms):
    acc = None
    rem = x
    for _ in range(terms):
        piece = _bf(rem)
        part = _dot(m01, piece)
        acc = part if acc is None else acc + part
        rem = rem - piece.astype(jnp.float32)
    return acc


def rwkv_mixer(p, mu, w0, w2, a0, a2, g2, k_k, k_a, r_k, ln_w, ln_b):
    bsz, seq, n_p = p.shape
    width = w0.shape[-1]
    pairs = width // (2 * RWKV_HEAD_DIM)
    chunk = min(RWKV_CHUNK, seq)
    tile = min(RWKV_TILE, seq)
    hid = jnp.arange(2 * RWKV_HEAD_DIM, dtype=jnp.int32) // RWKV_HEAD_DIM
    bd = (hid[:, None] == hid[None, :]).astype(jnp.bfloat16)
    row = lambda t: t.reshape(1, -1)
    full = lambda a: pl.BlockSpec(a.shape, lambda b, i: (0,) * a.ndim)
    consts = [row(mu), row(w0), _bf(w2), row(a0), _bf(a2), _bf(g2), row(k_k), row(k_a), row(r_k),
              row(ln_w), row(ln_b), bd]
    act = lambda: pltpu.VMEM((tile, width), jnp.bfloat16)
    return pl.pallas_call(
        functools.partial(_rwkv_kernel, width=width, chunk=chunk),
        grid=(bsz, seq // tile),
        in_specs=[pl.BlockSpec((1, tile, n_p), lambda b, i: (b, i, 0))] + [full(a) for a in consts],
        out_specs=pl.BlockSpec((1, tile, width), lambda b, i: (b, i, 0)),
        out_shape=jax.ShapeDtypeStruct((bsz, seq, width), jnp.float32),
        scratch_shapes=[pltpu.VMEM((pairs, 2 * RWKV_HEAD_DIM, 2 * RWKV_HEAD_DIM), jnp.float32),
                        pltpu.VMEM((8, n_p), jnp.float32),
                        act(), act(), act(), act(), act(),
                        pltpu.VMEM((max(8, tile // chunk), width), jnp.float32),
                        pltpu.VMEM((tile, width), jnp.float32)],
        compiler_params=_cparams("arbitrary", "arbitrary"),
        name="rwkv",
    )(p, *consts)


def _s5_discretise(a_re, a_im, dt):
    lam_re = jnp.minimum(a_re, S5_MAX_REAL)
    lam_im = a_im
    mag = jnp.exp(lam_re * dt)
    ang = lam_im * dt
    ab_re, ab_im = mag * jnp.cos(ang), mag * jnp.sin(ang)
    den = lam_re * lam_re + lam_im * lam_im
    n_re, n_im = ab_re - 1.0, ab_im
    q_re = (n_re * lam_re + n_im * lam_im) / den
    q_im = (n_im * lam_re - n_re * lam_im) / den
    return ab_re, ab_im, q_re, q_im


def _s5_power(ab_re, ab_im, t, t_max):
    pr = jnp.ones(t.shape, jnp.float32)
    pi = jnp.zeros(t.shape, jnp.float32)
    br, bi = ab_re, ab_im
    for j in range(max(1, int(t_max).bit_length())):
        bit = ((t >> j) & 1) == 1
        pr, pi = jnp.where(bit, pr * br - pi * bi, pr), jnp.where(bit, pr * bi + pi * br, pi)
        br, bi = br * br - bi * bi, 2.0 * br * bi
    return pr, pi


def _s5ops_kernel(ldt_ref, ar_row, ai_row, ar_col, ai_col, bt_re, bt_im, ct_re, ct_im, til_ref,
                  toep_ref, pm_ref, q_ref, lvl_ref, *, lc):
    n_c, n_p = bt_re.shape[1], bt_re.shape[2]
    lw = lc * n_c
    dt = jnp.exp(ldt_ref[0])

    abr_re, abr_im, q_re, q_im = _s5_discretise(ar_row[0], ai_row[0], dt)
    bbt_re = q_re * bt_re[0] - q_im * bt_im[0]
    bbt_im = q_re * bt_im[0] + q_im * bt_re[0]
    s_row = lax.broadcasted_iota(jnp.int32, (lw, n_p), 0) // n_c
    pw_re, pw_im = _s5_power(abr_re, abr_im, lc - 1 - s_row, lc)
    tb_re = jnp.concatenate([bbt_re] * lc, axis=0)
    tb_im = jnp.concatenate([bbt_im] * lc, axis=0)
    pm_ref[0] = _bf(jnp.concatenate([pw_re * tb_re - pw_im * tb_im,
                                     pw_re * tb_im + pw_im * tb_re], axis=1))

    abc_re, abc_im, _, _ = _s5_discretise(ar_col[0], ai_col[0], dt)
    til = til_ref[...]
    c_re = _dot(ct_re[0], til, HI)
    c_im = _dot(ct_im[0], til, HI)
    t_lane = lax.broadcasted_iota(jnp.int32, (n_p, lw), 1) // n_c
    p0_re, p0_im = _s5_power(abc_re, abc_im, t_lane, lc)
    p1_re, p1_im = p0_re * abc_re - p0_im * abc_im, p0_re * abc_im + p0_im * abc_re
    ca0_re, ca0_im = p0_re * c_re - p0_im * c_im, p0_re * c_im + p0_im * c_re
    q_ref[0] = _bf(jnp.concatenate([p1_re * c_re - p1_im * c_im,
                                    -(p1_re * c_im + p1_im * c_re)], axis=0))
    r0 = _dot(bbt_re, ca0_re, HI) - _dot(bbt_im, ca0_im, HI)
    lane = lax.broadcasted_iota(jnp.int32, (n_c, lw), 1)
    for s in range(lc):
        blk = r0 if s == 0 else jnp.where(lane >= s * n_c, pltpu.roll(r0, s * n_c, axis=1), 0.0)
        toep_ref[0, s * n_c:(s + 1) * n_c, :] = _bf(blk)

    cr, ci = _s5_power(abr_re, abr_im, jnp.full(abr_re.shape, lc, jnp.int32), lc)
    for j in range(lvl_ref.shape[1]):
        lvl_ref[0, j] = jnp.concatenate([jnp.concatenate([cr, cr], axis=1),
                                         jnp.concatenate([-ci, ci], axis=1)], axis=0)
        cr, ci = cr * cr - ci * ci, 2.0 * cr * ci


def s5_operators(log_dt, a_re, a_im, b_re, b_im, c_re, c_im, n_chunks, lc):
    n_g, n_p = a_re.shape
    n_c = b_re.shape[-1]
    lw = lc * n_c
    n_lvl = max(1, int(math.ceil(math.log2(n_chunks))))
    til = (jnp.arange(lw)[None, :] % n_c == jnp.arange(n_c)[:, None]).astype(jnp.float32)
    t3 = lambda a: jnp.swapaxes(a, 1, 2)
    args = [log_dt.reshape(n_g, 1, 1), a_re.reshape(n_g, 1, n_p), a_im.reshape(n_g, 1, n_p),
            a_re.reshape(n_g, n_p, 1), a_im.reshape(n_g, n_p, 1),
            t3(b_re), t3(b_im), t3(c_re), t3(c_im)]
    per_g = lambda a: pl.BlockSpec((1,) + a.shape[1:], lambda g: (g,) + (0,) * (a.ndim - 1))
    return pl.pallas_call(
        functools.partial(_s5ops_kernel, lc=lc),
        grid=(n_g,),
        in_specs=[per_g(a) for a in args] + [pl.BlockSpec(til.shape, lambda g: (0, 0))],
        out_specs=[pl.BlockSpec((1, lw, lw), lambda g: (g, 0, 0)),
                   pl.BlockSpec((1, lw, 2 * n_p), lambda g: (g, 0, 0)),
                   pl.BlockSpec((1, 2 * n_p, lw), lambda g: (g, 0, 0)),
                   pl.BlockSpec((1, n_lvl, 2, 2 * n_p), lambda g: (g, 0, 0, 0))],
        out_shape=[jax.ShapeDtypeStruct((n_g, lw, lw), jnp.bfloat16),
                   jax.ShapeDtypeStruct((n_g, lw, 2 * n_p), jnp.bfloat16),
                   jax.ShapeDtypeStruct((n_g, 2 * n_p, lw), jnp.bfloat16),
                   jax.ShapeDtypeStruct((n_g, n_lvl, 2, 2 * n_p), jnp.float32)],
        compiler_params=_cparams("arbitrary"),
        name="s5ops",
    )(*args, til)


def _s5_kernel(u_ref, toep_ref, pm_ref, q_ref, lvl_ref, y_ref, *, n_chunks):
    u = _bf(u_ref[0])
    e = _dot(u, pm_ref[0])
    rows, two_p = e.shape
    half = two_p // 2
    cidx = lax.broadcasted_iota(jnp.int32, e.shape, 0) % n_chunks
    x = e
    n_lvl = lvl_ref.shape[1]
    for j in range(n_lvl):
        sh = 1 << j
        if sh >= n_chunks:
            break
        xs = jnp.where(cidx >= sh, pltpu.roll(x, sh, axis=0), 0.0)
        cf = lvl_ref[0, j]
        x = x + xs * cf[0:1, :] + pltpu.roll(xs, half, axis=1) * cf[1:2, :]
    x_in = jnp.where(cidx >= 1, pltpu.roll(x, 1, axis=0), 0.0)
    x_hi = _bf(x_in)
    x_lo = _bf(x_in - x_hi.astype(jnp.float32))
    q = q_ref[0]
    y_ref[0] = _dot(u, toep_ref[0]) + _dot(x_hi, q) + _dot(x_lo, q)


def _to_groups_kernel(u_ref, o_ref, *, lc, n_c):
    per, nb, lw = o_ref.shape
    slot = lax.broadcasted_iota(jnp.int32, (nb, LANES), 1) // n_c
    a = [u_ref[pl.ds(s, nb, stride=lc), :] for s in range(lc)]
    for gi in range(per):
        for j in range(lw // LANES):
            acc = None
            for ai in range(per):
                shift = ((ai - gi) % per) * n_c
                src = a[j * per + ai]
                piece = src if shift == 0 else pltpu.roll(src, shift, axis=1)
                acc = piece if acc is None else jnp.where(slot == ai, piece, acc)
            o_ref[gi, :, j * LANES:(j + 1) * LANES] = acc


def _from_groups_kernel(y_ref, o_ref, *, lc, n_c):
    per, nb, lw = y_ref.shape
    slot = lax.broadcasted_iota(jnp.int32, (nb, LANES), 1) // n_c
    for s in range(lc):
        j, ai = divmod(s, per)
        acc = None
        for gi in range(per):
            shift = ((gi - ai) % per) * n_c
            src = y_ref[gi, :, j * LANES:(j + 1) * LANES]
            piece = src if shift == 0 else pltpu.roll(src, shift, axis=1)
            acc = piece if acc is None else jnp.where(slot == gi, piece, acc)
        o_ref[pl.ds(s, nb, stride=lc), :] = acc


def _group_relayout(x, n_g, lc, to_groups, tile):
    if to_groups:
        n_tok, width = x.shape
    else:
        n_tok, width = x.shape[1] * lc, x.shape[2] // lc * n_g
    n_c = width // n_g
    per = LANES // n_c
    nb = tile // lc
    tok_spec = pl.BlockSpec((tile, LANES), lambda i, vb: (i, vb))
    grp_spec = pl.BlockSpec((per, nb, lc * n_c), lambda i, vb: (vb, i, 0))
    kern = _to_groups_kernel if to_groups else _from_groups_kernel
    out_shape = (n_g, n_tok // lc, lc * n_c) if to_groups else (n_tok, width)
    return pl.pallas_call(
        functools.partial(kern, lc=lc, n_c=n_c),
        grid=(n_tok // tile, n_g // per),
        in_specs=[tok_spec if to_groups else grp_spec],
        out_specs=grp_spec if to_groups else tok_spec,
        out_shape=jax.ShapeDtypeStruct(out_shape, x.dtype),
        compiler_params=_cparams("arbitrary", "arbitrary"),
        name="to_groups" if to_groups else "from_groups",
    )(x)


def s5_core(u, log_dt, a_re, a_im, b_re, b_im, c_re, c_im):
    bsz, seq, width = u.shape
    n_g, n_p = a_re.shape
    n_c = width // n_g
    lc = min(S5_CHUNK, seq)
    n_chunks = seq // lc
    toep, pm, q, lvl = s5_operators(log_dt, a_re, a_im, b_re, b_im, c_re, c_im, n_chunks, lc)
    tile = min(2048, bsz * seq)
    ug = _group_relayout(u.reshape(bsz * seq, width), n_g, lc, True, tile)
    rows, lw = bsz * n_chunks, lc * n_c
    yg = pl.pallas_call(
        functools.partial(_s5_kernel, n_chunks=n_chunks),
        grid=(n_g,),
        in_specs=[pl.BlockSpec((1, rows, lw), lambda g: (g, 0, 0)),
                  pl.BlockSpec((1, lw, lw), lambda g: (g, 0, 0)),
                  pl.BlockSpec((1, lw, 2 * n_p), lambda g: (g, 0, 0)),
                  pl.BlockSpec((1, 2 * n_p, lw), lambda g: (g, 0, 0)),
                  pl.BlockSpec((1,) + lvl.shape[1:], lambda g: (g, 0, 0, 0))],
        out_specs=pl.BlockSpec((1, rows, lw), lambda g: (g, 0, 0)),
        out_shape=jax.ShapeDtypeStruct((n_g, rows, lw), jnp.float32),
        compiler_params=_cparams("arbitrary"),
        name="s5",
    )(ug, toep, pm, q, lvl)
    return _group_relayout(yg, n_g, lc, False, tile).reshape(bsz, seq, width)


def _gelu_tanh(y):
    return 0.5 * y * (1.0 + jnp.tanh(math.sqrt(2.0 / math.pi) * (y + 0.044715 * (y * y * y))))


def _outproj_kernel(yr_ref, ys_ref, u_ref, x_ref, mods_ref, d_ref, wglu_ref, bglu_ref, wout_ref,
                    gpost_ref, gpre_ref, wrt_ref, x1_ref, h2_ref, lg_ref):
    m = mods_ref[0]
    yr = yr_ref[...]
    y5 = _gelu_tanh(ys_ref[...] + d_ref[...] * u_ref[...])
    y5 = y5 * _sigmoid(_dot(_bf(y5), wglu_ref[...]) + bglu_ref[...])
    wr = yr.shape[-1]
    mixed = _dot(_bf(yr), wout_ref[0:wr, :]) + _dot(_bf(y5), wout_ref[wr:, :])
    x1 = x_ref[...] + m[2:3, :] * _rms(mixed, gpost_ref[...])
    x1_ref[...] = x1
    h2 = _rms(x1, gpre_ref[...]) * (1.0 + m[4:5, :]) + m[3:4, :]
    half = h2.shape[-1] // 2
    _store_row_tiles(h2_ref, _pack_bf16_pair(h2[:, :half], h2[:, half:]))
    w_hi, w_lo = wrt_ref[0], wrt_ref[1]
    h_hi = _bf(h2)
    h_lo = _bf(h2 - h_hi.astype(jnp.float32))
    lg_ref[...] = _dot_nt(w_hi, h_hi) + (_dot_nt(w_hi, h_lo) + _dot_nt(w_lo, h_hi))


def out_proj(y_rwkv, y_s5, u, x, mods3, s5_d, w_glu_bf, b_glu, w_out_bf, g_post, g_pre, w_router_t,
             seq, tm):
    n_tok, dm = x.shape
    wr, ws = y_rwkv.shape[-1], y_s5.shape[-1]
    n_e = w_router_t.shape[0]
    tpb = seq // tm
    tok = lambda w: pl.BlockSpec((tm, w), lambda i: (i, 0))
    full = lambda a: pl.BlockSpec(a.shape, lambda i: (0,) * a.ndim)
    row = lambda t: t.reshape(1, -1)
    wr_hi = _bf(w_router_t)
    wr_split = jnp.stack([wr_hi, _bf(w_router_t - wr_hi.astype(jnp.float32))])
    consts = [row(s5_d), w_glu_bf, row(b_glu), w_out_bf, row(g_post), row(g_pre), wr_split]
    return pl.pallas_call(
        _outproj_kernel,
        grid=(n_tok // tm,),
        in_specs=[tok(wr), tok(ws), tok(ws), tok(dm),
                  pl.BlockSpec((1, N_MODS, dm), lambda i: (i // tpb, 0, 0))] + [full(a) for a in consts],
        out_specs=[tok(dm), _row_tile_spec(tm, dm // 2, lambda i: i),
                   pl.BlockSpec((n_e, tm), lambda i: (0, i))],
        out_shape=[jax.ShapeDtypeStruct((n_tok, dm), jnp.float32),
                   jax.ShapeDtypeStruct((n_tok * (dm // 2 // LANES), LANES), jnp.uint32),
                   jax.ShapeDtypeStruct((n_e, n_tok), jnp.float32)],
        compiler_params=_cparams("arbitrary"),
        name="outproj",
    )(y_rwkv, y_s5, u, x, mods3, *consts)


def _route_kernel(lg_ref, bias_ref, tri_ref, e_ref, w_ref, r_ref, cnt_ref, carry_ref):
    i = pl.program_id(0)

    @pl.when(i == 0)
    def _():
        carry_ref[...] = jnp.zeros_like(carry_ref)

    neg = -jnp.inf
    scores = _sigmoid(lg_ref[...])
    n_e, tm = scores.shape
    choice = scores + bias_ref[...]
    gsz = n_e // ROUTE_GROUPS
    c3 = choice.reshape(ROUTE_GROUPS, gsz, tm)
    io = lax.broadcasted_iota(jnp.int32, c3.shape, 1)
    m1 = jnp.max(c3, axis=1, keepdims=True)
    first = jnp.min(jnp.where(c3 == m1, io, gsz), axis=1, keepdims=True)
    m2 = jnp.max(jnp.where(io == first, neg, c3), axis=1, keepdims=True)
    gs = m1 + m2
    gi = lax.broadcasted_iota(jnp.int32, gs.shape, 0)
    rank = jnp.zeros(gs.shape, jnp.int32)
    for j in range(ROUTE_GROUPS):
        gj = gs[j:j + 1]
        beats = (gj > gs) | ((gj == gs) & (gi > j))
        rank = rank + beats.astype(jnp.int32)
    masked = jnp.where(rank < ROUTE_TOPK_GROUPS, c3, neg).reshape(n_e, tm)

    eio = lax.broadcasted_iota(jnp.int32, (n_e, tm), 0)
    ids, ws = [], []
    mhot = jnp.zeros((n_e, tm), jnp.float32)
    for _ in range(TOP_K):
        m = jnp.max(masked, axis=0, keepdims=True)
        idx = jnp.min(jnp.where(masked == m, eio, n_e), axis=0, keepdims=True)
        sel = eio == idx
        ws.append(jnp.sum(jnp.where(sel, scores, 0.0), axis=0, keepdims=True))
        ids.append(idx)
        masked = jnp.where(sel, neg, masked)
        mhot = jnp.where(sel, 1.0, mhot)
    wsum = ws[0]
    for t in ws[1:]:
        wsum = wsum + t
    before = _dot(_bf(mhot), tri_ref[...]) + carry_ref[...]
    ranks = [jnp.sum(jnp.where(eio == idx, before, 0.0), axis=0, keepdims=True) for idx in ids]
    e_ref[...] = jnp.concatenate(ids, axis=0)
    w_ref[...] = jnp.concatenate(ws, axis=0) / wsum * ROUTE_SCALE
    r_ref[...] = jnp.concatenate(ranks, axis=0).astype(jnp.int32)
    carry_ref[...] = carry_ref[...] + jnp.sum(mhot, axis=1, keepdims=True)
    cnt_ref[...] = carry_ref[...]


def route(logits_t, router_bias, tm):
    n_e, n_tok = logits_t.shape
    tri = (jnp.arange(tm)[:, None] < jnp.arange(tm)[None, :]).astype(jnp.bfloat16)
    kt = lambda: pl.BlockSpec((TOP_K, tm), lambda i: (0, i))
    return pl.pallas_call(
        _route_kernel,
        grid=(n_tok // tm,),
        in_specs=[pl.BlockSpec((n_e, tm), lambda i: (0, i)),
                  pl.BlockSpec((n_e, 1), lambda i: (0, 0)),
                  pl.BlockSpec((tm, tm), lambda i: (0, 0))],
        out_specs=[kt(), kt(), kt(), pl.BlockSpec((n_e, 1), lambda i: (0, 0))],
        out_shape=[jax.ShapeDtypeStruct((TOP_K, n_tok), jnp.int32),
                   jax.ShapeDtypeStruct((TOP_K, n_tok), jnp.float32),
                   jax.ShapeDtypeStruct((TOP_K, n_tok), jnp.int32),
                   jax.ShapeDtypeStruct((n_e, 1), jnp.float32)],
        scratch_shapes=[pltpu.VMEM((n_e, 1), jnp.float32)],
        compiler_params=_cparams("arbitrary"),
        name="route",
    )(logits_t, router_bias.reshape(n_e, 1), tri)


def _dest_kernel(e_ref, r_ref, ps_ref, d_ref):
    n_e = ps_ref.shape[0]
    tm = e_ref.shape[1]
    eio = lax.broadcasted_iota(jnp.int32, (n_e, tm), 0)
    ps = ps_ref[...]
    rows = [jnp.sum(jnp.where(eio == e_ref[k:k + 1, :], ps, 0.0), axis=0, keepdims=True)
            for k in range(TOP_K)]
    d_ref[...] = jnp.concatenate(rows, axis=0).astype(jnp.int32) + r_ref[...]


def dest_rows(e_idx, rank, pad_start, tm):
    n_tok = e_idx.shape[1]
    n_e = pad_start.shape[0]
    kt = lambda: pl.BlockSpec((TOP_K, tm), lambda i: (0, i))
    return pl.pallas_call(
        _dest_kernel,
        grid=(n_tok // tm,),
        in_specs=[kt(), kt(), pl.BlockSpec((n_e, 1), lambda i: (0, 0))],
        out_specs=kt(),
        out_shape=jax.ShapeDtypeStruct((TOP_K, n_tok), jnp.int32),
        compiler_params=_cparams("arbitrary"),
        name="dest",
    )(e_idx, rank, pad_start.astype(jnp.float32).reshape(n_e, 1))


def _dispatch_kernel(fill_start_ref, fill_len_ref, nu_ref, dest_ref, h_ref, xs_ref, zeros, sem, zsem,
                     *, n_s, bm, experts_per_step, tiles_per_step, n_tiles):
    i = pl.program_id(0)
    td = dest_ref.shape[1]

    @pl.when(i == 0)
    def _():
        zeros[...] = jnp.zeros_like(zeros)

    def body(t, carry):
        for k in range(TOP_K):
            pltpu.make_async_copy(_row(h_ref, t, n_s), _row(xs_ref, dest_ref[k, t], n_s),
                                  sem).start(priority=k % 2)
        return carry

    lax.fori_loop(0, td, body, 0, unroll=2)

    fills = []
    for j in range(experts_per_step):
        e = i * experts_per_step + j
        start, length = fill_start_ref[e], fill_len_ref[e]
        piece = bm // 2
        while piece >= 1:
            off = start + (length & ~(2 * piece - 1))
            fills.append(((length & piece) != 0,
                          pltpu.make_async_copy(zeros.at[pl.ds(0, piece * n_s), :],
                                                xs_ref.at[pl.ds(off * n_s, piece * n_s), :], zsem)))
            piece //= 2
    for j in range(tiles_per_step):
        tile = i * tiles_per_step + j
        fills.append(((tile >= nu_ref[0]) & (tile < n_tiles),
                      pltpu.make_async_copy(zeros, xs_ref.at[pl.ds(tile * (bm * n_s), bm * n_s), :], zsem)))
    for cond, cp in fills:
        pl.when(cond)(cp.start)

    for _ in range(TOP_K):
        pltpu.make_async_copy(h_ref, xs_ref.at[pl.ds(0, td * n_s), :], sem).wait()
    for cond, cp in fills:
        pl.when(cond)(cp.wait)


def dispatch(dest, h2p, fill_start, fill_len, n_used, n_tok, n_tiles, td, bm):
    n_s = h2p.shape[0] // n_tok
    n_steps = n_tok // td
    n_e = fill_start.shape[0]
    experts_per_step = -(-n_e // n_steps)
    tiles_per_step = -(-n_tiles // n_steps)
    pad = n_steps * experts_per_step - n_e
    fill_start = jnp.pad(fill_start, (0, pad))
    fill_len = jnp.pad(fill_len, (0, pad))
    return pl.pallas_call(
        functools.partial(_dispatch_kernel, n_s=n_s, bm=bm, experts_per_step=experts_per_step,
                          tiles_per_step=tiles_per_step, n_tiles=n_tiles),
        grid_spec=pltpu.PrefetchScalarGridSpec(
            num_scalar_prefetch=3,
            grid=(n_steps,),
            in_specs=[pl.BlockSpec((TOP_K, td), lambda i, *_: (0, i), memory_space=pltpu.SMEM),
                      _row_tile_spec(td, n_s * LANES, lambda i, *_: i)],
            out_specs=pl.BlockSpec(memory_space=pl.ANY),
            scratch_shapes=[pltpu.VMEM((bm * n_s, LANES), h2p.dtype),
                            pltpu.SemaphoreType.DMA(()), pltpu.SemaphoreType.DMA(())]),
        out_shape=jax.ShapeDtypeStruct((n_tiles * bm * n_s, LANES), h2p.dtype),
        compiler_params=_cparams("arbitrary"),
        name="dispatch",
    )(fill_start, fill_len, n_used, dest, h2p)


def _expert_kernel(te_ref, nu_ref, nt_ref, x_ref, wg_hbm, wu_hbm, wd_hbm, o_ref,
                   wg_st, wu_st, wd_st, wgu_bf, wd_bf, grp, sems, *, bm):
    i = pl.program_id(0)
    ff = wd_bf.shape[0]
    n_used = nu_ref[0]

    def fetch(e, slot):
        return (pltpu.make_async_copy(wg_hbm.at[e], wg_st.at[slot], sems.at[slot]),
                pltpu.make_async_copy(wu_hbm.at[e], wu_st.at[slot], sems.at[slot]),
                pltpu.make_async_copy(wd_hbm.at[e], wd_st.at[slot], sems.at[slot]))

    @pl.when(i == 0)
    def _():
        grp[0] = 0
        for cp in fetch(te_ref[0], 0):
            cp.start()

    @pl.when(i < n_used)
    def _():
        e = te_ref[i]

        @pl.when((i == 0) | (e != te_ref[jnp.maximum(i - 1, 0)]))
        def _():
            slot = grp[0] & 1
            grp[0] = grp[0] + 1
            for cp in fetch(e, slot):
                cp.wait()
            wgu_bf[:, :ff] = _bf(wg_st[slot])
            wgu_bf[:, ff:] = _bf(wu_st[slot])
            wd_bf[...] = _bf(wd_st[slot])
            nxt = i + nt_ref[e]

            @pl.when(nxt < n_used)
            def _():
                for cp in fetch(te_ref[jnp.minimum(nxt, n_used - 1)], 1 - slot):
                    cp.start()

        hi, lo = _unpack_bf16_pair(_load_row_tiles(x_ref, bm))
        x = jnp.concatenate([hi, lo], axis=1)
        gu = _dot(x, wgu_bf[...])
        hid = _silu(gu[:, :ff]) * gu[:, ff:]
        y = _dot(_bf(hid), wd_bf[...])
        half = y.shape[1] // 2
        _store_row_tiles(o_ref, _pack_bf16_pair(y[:, :half], y[:, half:]))

    @pl.when(i >= nu_ref[0])
    def _():
        o_ref[...] = jnp.zeros_like(o_ref)


def expert_ffn(tile_e, n_used, tiles_per_expert, xs, n_rows, w_gate, w_up, w_down, bm):
    dm = w_gate.shape[1]
    half = dm // 2
    ff = w_gate.shape[-1]
    rows_in = _row_tile_spec(bm, half, lambda i, te, nu, nt: jnp.minimum(i, nu[0] - 1))
    rows_out = _row_tile_spec(bm, half, lambda i, te, nu, nt: i)
    hbm = pl.BlockSpec(memory_space=pl.ANY)
    return pl.pallas_call(
        functools.partial(_expert_kernel, bm=bm),
        grid_spec=pltpu.PrefetchScalarGridSpec(
            num_scalar_prefetch=3,
            grid=(n_rows // bm,),
            in_specs=[rows_in, hbm, hbm, hbm],
            out_specs=rows_out,
            scratch_shapes=[pltpu.VMEM((2, dm, ff), jnp.float32),
                            pltpu.VMEM((2, dm, ff), jnp.float32),
                            pltpu.VMEM((2, ff, dm), jnp.float32),
                            pltpu.VMEM((dm, 2 * ff), jnp.bfloat16),
                            pltpu.VMEM((ff, dm), jnp.bfloat16),
                            pltpu.SMEM((1,), jnp.int32),
                            pltpu.SemaphoreType.DMA((2,))]),
        out_shape=jax.ShapeDtypeStruct((n_rows * (half // LANES), LANES), jnp.uint32),
        compiler_params=_cparams("arbitrary"),
        name="expert",
    )(tile_e, n_used, tiles_per_expert, xs, w_gate, w_up, w_down)


def _combine_kernel(dest_ref, dnext_ref, w_ref, x1_ref, h2_ref, mods_ref, g_ref, wsg_ref, wsu_ref,
                    wsd_ref, ys_ref, o_ref, gbuf, sems, *, n_s):
    i = pl.program_id(0)
    n_steps = pl.num_programs(0)
    tc = x1_ref.shape[0]

    def issue(d_ref, slot):
        def body(t, carry):
            for k in range(TOP_K):
                pltpu.make_async_copy(_row(ys_ref, d_ref[k, t], n_s), _row(gbuf.at[slot, k], t, n_s),
                                      sems.at[slot]).start(priority=k % 2)
            return carry
        lax.fori_loop(0, tc, body, 0, unroll=2)

    def finish(slot):
        for k in range(TOP_K):
            pltpu.make_async_copy(ys_ref.at[pl.ds(0, tc * n_s), :], gbuf.at[slot, k],
                                  sems.at[slot]).wait()
        hi, lo = _unpack_bf16_pair(_load_row_tiles(h2_ref, tc))
        h = jnp.concatenate([hi, lo], axis=1)
        hid = _silu(_dot(h, wsg_ref[...])) * _dot(h, wsu_ref[...])
        y = _dot(_bf(hid), wsd_ref[...])
        w = w_ref[...]
        acc_hi = acc_lo = None
        for k in range(TOP_K):
            g_hi, g_lo = _unpack_pair_f32(_load_row_tiles(gbuf.at[slot, k], tc))
            wk = w[:, k:k + 1]
            acc_hi = wk * g_hi if acc_hi is None else acc_hi + wk * g_hi
            acc_lo = wk * g_lo if acc_lo is None else acc_lo + wk * g_lo
        y = y + jnp.concatenate([acc_hi, acc_lo], axis=1)
        m = mods_ref[0]
        o_ref[...] = x1_ref[...] + m[5:6, :] * _rms(y, g_ref[...])

    @pl.when(i == 0)
    def _():
        issue(dest_ref, 0)

    for slot in range(2):
        @pl.when(i % 2 == slot)
        def _(slot=slot):
            @pl.when(i + 1 < n_steps)
            def _():
                issue(dnext_ref, 1 - slot)
            finish(slot)


def combine(dest, w_tk, x1, h2p, mods3, g_post, ws_gate_bf, ws_up_bf, ws_down_bf, ys, seq, tc):
    n_tok, dm = x1.shape
    n_s = dm // 2 // LANES
    n_steps = n_tok // tc
    tpb = seq // tc
    full = lambda a: pl.BlockSpec(a.shape, lambda i: (0,) * a.ndim)
    tok = lambda w: pl.BlockSpec((tc, w), lambda i: (i, 0))
    g2 = g_post.reshape(1, dm)
    return pl.pallas_call(
        functools.partial(_combine_kernel, n_s=n_s),
        grid=(n_steps,),
        in_specs=[pl.BlockSpec((TOP_K, tc), lambda i: (0, i), memory_space=pltpu.SMEM),
                  pl.BlockSpec((TOP_K, tc), lambda i: (0, jnp.minimum(i + 1, n_steps - 1)),
                               memory_space=pltpu.SMEM),
                  tok(TOP_K),
                  tok(dm), _row_tile_spec(tc, dm // 2, lambda i: i),
                  pl.BlockSpec((1, N_MODS, dm), lambda i: (i // tpb, 0, 0)),
                  full(g2), full(ws_gate_bf), full(ws_up_bf), full(ws_down_bf),
                  pl.BlockSpec(memory_space=pl.ANY)],
        out_specs=tok(dm),
        out_shape=jax.ShapeDtypeStruct((n_tok, dm), jnp.float32),
        scratch_shapes=[pltpu.VMEM((2, TOP_K, tc * n_s, LANES), jnp.uint32),
                        pltpu.SemaphoreType.DMA((2,))],
        compiler_params=_cparams("arbitrary"),
        name="combine",
    )(dest, dest, w_tk, x1, h2p, mods3, g2, ws_gate_bf, ws_up_bf, ws_down_bf, ys)


def moe_ffn(h2, x1, logits_t, mods3, router_bias, w_gate, w_up, w_down, ws_gate, ws_up, ws_down,
            g_post, seq):
    n_tok, dm = x1.shape
    n_e = w_gate.shape[0]
    bm = EXPERT_ROW_TILE
    e_idx, w_kt, rank, cnt = route(logits_t, router_bias, min(512, n_tok))
    counts = cnt[:, 0].astype(jnp.int32)
    padded = (counts + bm - 1) // bm * bm
    pad_end = jnp.cumsum(padded)
    pad_start = (pad_end - padded).astype(jnp.int32)
    n_tiles = -(-(n_tok * TOP_K) // bm) + n_e
    n_used = (pad_end[-1] // bm).astype(jnp.int32)
    tile_start = jnp.arange(n_tiles, dtype=jnp.int32) * bm
    tile_e = jnp.minimum(jnp.sum((pad_end[None, :] <= tile_start[:, None]).astype(jnp.int32), axis=1),
                         n_e - 1).astype(jnp.int32)
    tile_e = jnp.where(jnp.arange(n_tiles) < n_used, tile_e, tile_e[n_used - 1])
    dest = dest_rows(e_idx, rank, pad_start, min(512, n_tok))
    n_used1 = n_used.reshape(1)
    xs = dispatch(dest, h2, pad_start + counts, padded - counts, n_used1, n_tok, n_tiles,
                  min(MOE_TOKEN_TILE, n_tok), bm)
    ys = expert_ffn(tile_e, n_used1, (padded // bm).astype(jnp.int32), xs, n_tiles * bm,
                    w_gate, w_up, w_down, bm)
    bf = lambda a: a.astype(jnp.bfloat16)
    return combine(dest, w_kt.T, x1, h2, mods3, g_post, bf(ws_gate), bf(ws_up), bf(ws_down), ys,
                   seq, min(MOE_TOKEN_TILE, n_tok))


def kernel(x, c, w_ada, b_ada, g_pre_mix, g_post_mix, g_pre_ffn, g_post_ffn, w_in, mu_shift, rwkv_w0, rwkv_w2, rwkv_a0, rwkv_a2, rwkv_g2, rwkv_k_k, rwkv_k_a, rwkv_r_k, rwkv_ln_w, rwkv_ln_b, s5_log_dt, s5_a_re, s5_a_im, s5_b_re, s5_b_im, s5_c_re, s5_c_im, s5_d, s5_w_glu, s5_b_glu, w_out, w_router, router_bias, w_gate, w_up, w_down, ws_gate, ws_up, ws_down):
    bsz, seq, dm = x.shape
    depth = w_ada.shape[0]
    bf = lambda a: a.astype(jnp.bfloat16)
    tm = min(512, seq)
    for l in range(depth):
        mods3 = ada_mods(c, w_ada[l], b_ada[l]).reshape(bsz, N_MODS, dm)
        n_rwkv = mu_shift.shape[-1]
        p, u = in_proj(x, mods3, g_pre_mix[l], bf(w_in[l]), n_rwkv, tm)
        y_rwkv = rwkv_mixer(p, mu_shift[l], rwkv_w0[l], rwkv_w2[l], rwkv_a0[l], rwkv_a2[l],
                            rwkv_g2[l], rwkv_k_k[l], rwkv_k_a[l], rwkv_r_k[l].reshape(-1),
                            rwkv_ln_w[l], rwkv_ln_b[l])
        y_s5 = s5_core(u, s5_log_dt[l], s5_a_re[l], s5_a_im[l], s5_b_re[l], s5_b_im[l],
                       s5_c_re[l], s5_c_im[l])
        flat = lambda a: a.reshape(bsz * seq, a.shape[-1])
        x1, h2, logits_t = out_proj(flat(y_rwkv), flat(y_s5), flat(u), flat(x), mods3, s5_d[l],
                                    bf(s5_w_glu[l]), s5_b_glu[l], bf(w_out[l]), g_post_mix[l],
                                    g_pre_ffn[l], w_router[l].T, seq, tm)
        out = moe_ffn(h2, x1, logits_t, mods3, router_bias[l], w_gate[l], w_up[l], w_down[l],
                      ws_gate[l], ws_up[l], ws_down[l], g_post_ffn[l], seq)
        x = out.reshape(bsz, seq, dm)
    return x
```
